```python
import math
import jax, jax.numpy as jnp
from jax import lax
import numpy as np

D_MODEL = 1024
BATCH = 32
SEQ = 2048
DEPTH = 1
DEC_BATCH = 16
DEC_SEQ = 4096
PAST_LEN = 128

D_MIX = D_MODEL
D_SSM = D_MIX // 2
D_ATTN = D_MIX - D_SSM
SSM_GROUP = 16
N_SSM_GROUPS = D_SSM // SSM_GROUP
SSM_STATE = 64
HEAD_DIM = 64
N_HEADS = D_ATTN // HEAD_DIM
N_KV_HEADS = 2
GQA_GROUP = N_HEADS // N_KV_HEADS
WINDOW = 128
BLOCK = 128
N_BUCKETS = 32
MAX_DISTANCE = 128
RMS_EPS = 1e-6
NEG_INF = -1e30
STEP_MIN = 0.001
STEP_MAX = 0.1
D_IN_PROJ = 2 * D_SSM + D_ATTN + 2 * N_KV_HEADS * HEAD_DIM + D_ATTN
SPLITS = (D_SSM, 2 * D_SSM, 2 * D_SSM + D_ATTN,
          2 * D_SSM + D_ATTN + N_KV_HEADS * HEAD_DIM,
          2 * D_SSM + D_ATTN + 2 * N_KV_HEADS * HEAD_DIM)

kernel_name = "hymba_s5_window_gqa_encoder"


def _rmsnorm(x, w):
    xf = x.astype(jnp.float32)
    y = xf * lax.rsqrt(jnp.mean(xf * xf, axis=-1, keepdims=True) + RMS_EPS)
    return (y * w.astype(jnp.float32)).astype(x.dtype)


def _t5_buckets(rel):
    half = N_BUCKETS // 2
    max_exact = half // 2
    ret = jnp.where(rel > 0, half, 0)
    n = jnp.abs(rel)
    nf = jnp.maximum(n, 1).astype(jnp.float32)
    large = max_exact + (jnp.log(nf / max_exact) / math.log(MAX_DISTANCE / max_exact)
                         * (half - max_exact)).astype(jnp.int32)
    large = jnp.minimum(large, half - 1)
    return ret + jnp.where(n < max_exact, n, large)


def _ssm_direction(u, lam_re, lam_im, log_step, b_re, b_im, c_re, c_im, reverse):
    f32 = jnp.float32
    lam_re = lam_re.astype(f32); lam_im = lam_im.astype(f32)
    dt = jnp.exp(log_step.astype(f32))[:, None]
    mag = jnp.exp(lam_re * dt)
    lb_re = mag * jnp.cos(lam_im * dt)
    lb_im = mag * jnp.sin(lam_im * dt)
    den = lam_re * lam_re + lam_im * lam_im
    nr = lb_re - 1.0
    coef_re = ((nr * lam_re + lb_im * lam_im) / den)[..., None]
    coef_im = ((lb_im * lam_re - nr * lam_im) / den)[..., None]
    b_re = b_re.astype(f32); b_im = b_im.astype(f32)
    bb_re = coef_re * b_re - coef_im * b_im
    bb_im = coef_re * b_im + coef_im * b_re
    bu_re = jnp.einsum('blgh,gph->blgp', u, bb_re)
    bu_im = jnp.einsum('blgh,gph->blgp', u, bb_im)
    a_re = jnp.broadcast_to(lb_re, bu_re.shape)
    a_im = jnp.broadcast_to(lb_im, bu_im.shape)

    def combine(e1, e2):
        a1r, a1i, b1r, b1i = e1
        a2r, a2i, b2r, b2i = e2
        return (a2r * a1r - a2i * a1i,
                a2r * a1i + a2i * a1r,
                a2r * b1r - a2i * b1i + b2r,
                a2r * b1i + a2i * b1r + b2i)

    _, _, s_re, s_im = lax.associative_scan(combine, (a_re, a_im, bu_re, bu_im),
                                            axis=1, reverse=reverse)
    return (jnp.einsum('blgp,ghp->blgh', s_re, c_re.astype(f32))
            - jnp.einsum('blgp,ghp->blgh', s_im, c_im.astype(f32)))


def _ssm_branch(u, lam_re, lam_im, log_step, b_re, b_im, c_re, c_im, d_skip, w_glu, b_glu):
    b, l, _ = u.shape
    ug = u.reshape(b, l, N_SSM_GROUPS, SSM_GROUP).astype(jnp.float32)
    y = d_skip.astype(jnp.float32) * ug
    for d in range(2):
        y = y + _ssm_direction(ug, lam_re[d], lam_im[d], log_step[d], b_re[d], b_im[d],
                               c_re[d], c_im[d], reverse=(d == 1))
    y = y.reshape(b, l, D_SSM).astype(u.dtype)
    g = jax.nn.gelu(y)
    return g * jax.nn.sigmoid(g @ w_glu + b_glu)


def _attn_branch(q, k, v, sink, rel_bias):
    b, l, _ = q.shape
    nb = l // BLOCK
    qb = q.reshape(b, nb, BLOCK, N_KV_HEADS, GQA_GROUP, HEAD_DIM)

    def band(t):
        tp = jnp.pad(t.reshape(b, l, N_KV_HEADS, HEAD_DIM), ((0, 0), (BLOCK, BLOCK), (0, 0), (0, 0)))
        tp = tp.reshape(b, nb + 2, BLOCK, N_KV_HEADS, HEAD_DIM)
        return jnp.concatenate([tp[:, :-2], tp[:, 1:-1], tp[:, 2:]], axis=2)

    kw = band(k)
    vw = band(v)
    q_idx = jnp.arange(BLOCK, dtype=jnp.int32)[:, None]
    s_idx = jnp.arange(3 * BLOCK, dtype=jnp.int32)[None, :]
    rel = s_idx - BLOCK - q_idx
    bias = rel_bias.astype(jnp.float32)[_t5_buckets(rel)]
    bias = jnp.transpose(bias, (2, 0, 1)).reshape(N_KV_HEADS, GQA_GROUP, BLOCK, 3 * BLOCK)
    kpos = jnp.arange(nb, dtype=jnp.int32)[:, None, None] * BLOCK + s_idx[None] - BLOCK
    valid = (jnp.abs(rel) <= WINDOW)[None] & (kpos >= 0) & (kpos < l)

    scores = jnp.einsum('bnqkgd,bnskd->bnkgqs', qb, kw, preferred_element_type=jnp.float32)
    scores = scores * (HEAD_DIM ** -0.5) + bias
    scores = jnp.where(valid[None, :, None, None], scores, NEG_INF)
    sink_l = sink.astype(jnp.float32).reshape(N_KV_HEADS, GQA_GROUP, 1, 1)
    m = jnp.maximum(jnp.max(scores, axis=-1, keepdims=True), sink_l)
    p = jnp.exp(scores - m)
    denom = jnp.sum(p, axis=-1, keepdims=True) + jnp.exp(sink_l - m)
    probs = (p / denom).astype(v.dtype)
    out = jnp.einsum('bnkgqs,bnskd->bnqkgd', probs, vw)
    return out.reshape(b, l, D_ATTN)


def _layer(x, norm_w, w_in, lam_re, lam_im, log_step, b_re, b_im, c_re, c_im, d_skip,
           w_glu, b_glu, ssm_norm_w, sink, attn_norm_w, w_out, rel_bias):
    h = _rmsnorm(x, norm_w)
    proj = h @ w_in
    u_ssm, z_ssm, q, k, v, z_attn = jnp.split(proj, SPLITS, axis=-1)
    y_ssm = _ssm_branch(u_ssm, lam_re, lam_im, log_step, b_re, b_im, c_re, c_im, d_skip, w_glu, b_glu)
    y_attn = _attn_branch(q, k, v, sink, rel_bias)
    mixed = jnp.concatenate([_rmsnorm(y_ssm, ssm_norm_w) * jax.nn.silu(z_ssm),
                             _rmsnorm(y_attn, attn_norm_w) * jax.nn.silu(z_attn)], axis=-1)
    return x + mixed @ w_out


def _encoder(x, norm_w, w_in, lam_re, lam_im, log_step, b_re, b_im, c_re, c_im, d_skip,
             w_glu, b_glu, ssm_norm_w, sink, attn_norm_w, w_out, rel_bias, final_norm_w):
    for i in range(DEPTH):
        x = _layer(x, norm_w[i], w_in[i], lam_re[i], lam_im[i], log_step[i], b_re[i], b_im[i],
                   c_re[i], c_im[i], d_skip[i], w_glu[i], b_glu[i], ssm_norm_w[i], sink[i],
                   attn_norm_w[i], w_out[i], rel_bias)
    return _rmsnorm(x, final_norm_w)


def setup_inputs(seed: int = 0) -> dict:
    key = jax.random.key(seed)
    ks = jax.random.split(key, 24)
    f32 = jnp.float32
    G, P, H = N_SSM_GROUPS, SSM_STATE, SSM_GROUP
    nrm = lambda k, s: jax.random.normal(k, s, f32)
    x_prompt = nrm(ks[0], (BATCH, SEQ, D_MODEL))
    x_sample = nrm(ks[1], (DEC_BATCH, DEC_SEQ, D_MODEL))
    norm_w = 1.0 + 0.02 * nrm(ks[2], (DEPTH, D_MODEL))
    w_in = nrm(ks[3], (DEPTH, D_MODEL, D_IN_PROJ)) * D_MODEL ** -0.5
    lam_re = -0.5 + 0.01 * nrm(ks[4], (DEPTH, 2, G, P))
    lam_im = math.pi * jnp.arange(P, dtype=f32) + 0.01 * nrm(ks[5], (DEPTH, 2, G, P))
    log_step = jax.random.uniform(ks[6], (DEPTH, 2, G), f32, math.log(STEP_MIN), math.log(STEP_MAX))
    b_re = nrm(ks[7], (DEPTH, 2, G, P, H)) * (2 * H) ** -0.5
    b_im = nrm(ks[8], (DEPTH, 2, G, P, H)) * (2 * H) ** -0.5
    c_re = nrm(ks[9], (DEPTH, 2, G, H, P)) * (2 * P) ** -0.5
    c_im = nrm(ks[10], (DEPTH, 2, G, H, P)) * (2 * P) ** -0.5
    d_skip = nrm(ks[11], (DEPTH, G, H))
    w_glu = nrm(ks[12], (DEPTH, D_SSM, D_SSM)) * D_SSM ** -0.5
    b_glu = 0.02 * nrm(ks[13], (DEPTH, D_SSM))
    ssm_norm_w = 1.0 + 0.02 * nrm(ks[14], (DEPTH, D_SSM))
    sink = 0.5 * nrm(ks[15], (DEPTH, N_HEADS))
    attn_norm_w = 1.0 + 0.02 * nrm(ks[16], (DEPTH, D_ATTN))
    w_out = nrm(ks[17], (DEPTH, D_MIX, D_MODEL)) * D_MIX ** -0.5
    rel_bias = 0.1 * nrm(ks[18], (N_BUCKETS, N_HEADS))
    final_norm_w = 1.0 + 0.02 * nrm(ks[19], (D_MODEL,))
    return {"x_prompt": x_prompt, "x_sample": x_sample, "norm_w": norm_w, "w_in": w_in,
            "lam_re": lam_re, "lam_im": lam_im, "log_step": log_step, "b_re": b_re, "b_im": b_im,
            "c_re": c_re, "c_im": c_im, "d_skip": d_skip, "w_glu": w_glu, "b_glu": b_glu,
            "ssm_norm_w": ssm_norm_w, "sink": sink, "attn_norm_w": attn_norm_w, "w_out": w_out,
            "rel_bias": rel_bias, "final_norm_w": final_norm_w}


def reference(x_prompt, x_sample, norm_w, w_in, lam_re, lam_im, log_step, b_re, b_im, c_re, c_im,
              d_skip, w_glu, b_glu, ssm_norm_w, sink, attn_norm_w, w_out, rel_bias, final_norm_w):
    y_prompt = _encoder(x_prompt, norm_w, w_in, lam_re, lam_im, log_step, b_re, b_im, c_re, c_im,
                        d_skip, w_glu, b_glu, ssm_norm_w, sink, attn_norm_w, w_out, rel_bias, final_norm_w)
    y_sample = _encoder(x_sample, norm_w, w_in, lam_re, lam_im, log_step, b_re, b_im, c_re, c_im,
                        d_skip, w_glu, b_glu, ssm_norm_w, sink, attn_norm_w, w_out, rel_bias, final_norm_w)
    return (y_prompt, y_sample)
```

```python
import functools
import math

import numpy as np
import jax
import jax.numpy as jnp
from jax import lax
from jax.experimental import pallas as pl
from jax.experimental.pallas import tpu as pltpu

F32 = jnp.float32
BF16 = jnp.bfloat16

D_MODEL = 1024
D_SSM = 512
D_ATTN = 512
SSM_GROUP = 16
N_GROUPS = D_SSM // SSM_GROUP
SSM_STATE = 64
HEAD_DIM = 64
N_HEADS = D_ATTN // HEAD_DIM
N_KV_HEADS = 2
WINDOW = 128
BLOCK = 128
N_BUCKETS = 32
MAX_DISTANCE = 128
RMS_EPS = 1e-6
NEG_INF = -1e30

LANES = 128
CHUNK = 16
CHUNK_W = CHUNK * SSM_GROUP
STATE_W = 4 * SSM_STATE
TILE_CHUNKS = LANES
TILE = TILE_CHUNKS * CHUNK
SUB = 512
N_SUB = TILE // SUB
LANE_TILES = D_SSM // LANES
GROUPS_PER_LANE_TILE = LANES // SSM_GROUP
KEYS = 3 * BLOCK
Q_ROWS = 512
VMEM_LIMIT = 56 * 1024 * 1024

_C_U, _C_ZS, _C_Q, _C_K, _C_V, _C_ZA, _C_END = 0, 512, 1024, 1536, 1792, 2048, 2560


def _rms(x, w):
    return (x * lax.rsqrt(jnp.mean(x * x, axis=-1, keepdims=True) + RMS_EPS)) * w


def _proj_kernel(x_ref, nw_ref, w_ref, up_ref, zs_ref, q_ref, kd_ref, vd_ref, za_ref, u_scr, t_scr):
    s = pl.program_id(1)
    hb = _rms(x_ref[...], nw_ref[...]).astype(BF16)

    def mm(lo, hi):
        return jnp.dot(hb, w_ref[:, lo:hi], preferred_element_type=F32)

    u = mm(_C_U, _C_ZS)
    for j in range(LANE_TILES):
        u_scr[j, pl.ds(pl.multiple_of(s * SUB, SUB), SUB), :] = u[:, j * LANES:(j + 1) * LANES]
    zs_ref[...] = mm(_C_ZS, _C_Q).astype(BF16)
    q_ref[...] = (mm(_C_Q, _C_K) * (HEAD_DIM ** -0.5)).astype(BF16)
    kd_ref[...] = mm(_C_K, _C_V).astype(BF16)
    vd_ref[...] = mm(_C_V, _C_ZA).astype(BF16)
    za_ref[...] = mm(_C_ZA, _C_END).astype(BF16)

    @pl.when(s == N_SUB - 1)
    def _():
        for t in range(CHUNK):
            for j in range(LANE_TILES):
                blk = u_scr[j, pl.ds(t, TILE_CHUNKS, stride=CHUNK), :]
                t_scr[t, j] = blk.T
        for g in range(N_GROUPS):
            j, g3 = divmod(g, GROUPS_PER_LANE_TILE)
            for th in range(CHUNK_W // LANES):
                rows = jnp.concatenate(
                    [t_scr[th * 8 + t3, j, pl.ds(g3 * SSM_GROUP, SSM_GROUP), :] for t3 in range(8)], axis=0)
                up_ref[g, :, pl.ds(th * LANES, LANES)] = rows.T.astype(BF16)


def _proj_call(x2, norm_w, w_cat):
    n_tok = x2.shape[0]
    n_tiles = n_tok // TILE
    row = lambda i, s: (i * N_SUB + s, 0)
    const = lambda i, s: (0, 0)
    out_shapes = (
        jax.ShapeDtypeStruct((N_GROUPS, n_tiles * TILE_CHUNKS, CHUNK_W), BF16),
        jax.ShapeDtypeStruct((n_tok, D_SSM), BF16),
        jax.ShapeDtypeStruct((n_tok, D_ATTN), BF16),
        jax.ShapeDtypeStruct((n_tok, 2 * LANES), BF16),
        jax.ShapeDtypeStruct((n_tok, 2 * LANES), BF16),
        jax.ShapeDtypeStruct((n_tok, D_ATTN), BF16),
    )
    return pl.pallas_call(
        _proj_kernel,
        grid=(n_tiles, N_SUB),
        in_specs=[
            pl.BlockSpec((SUB, D_MODEL), row),
            pl.BlockSpec((1, D_MODEL), const),
            pl.BlockSpec((D_MODEL, _C_END), const),
        ],
        out_specs=(
            pl.BlockSpec((N_GROUPS, TILE_CHUNKS, CHUNK_W), lambda i, s: (0, i, 0)),
            pl.BlockSpec((SUB, D_SSM), row),
            pl.BlockSpec((SUB, D_ATTN), row),
            pl.BlockSpec((SUB, 2 * LANES), row),
            pl.BlockSpec((SUB, 2 * LANES), row),
            pl.BlockSpec((SUB, D_ATTN), row),
        ),
        out_shape=out_shapes,
        scratch_shapes=[
            pltpu.VMEM((LANE_TILES, TILE, LANES), F32),
            pltpu.VMEM((CHUNK, LANE_TILES, LANES, LANES), F32),
        ],
        compiler_params=pltpu.CompilerParams(
            dimension_semantics=("arbitrary", "arbitrary"), vmem_limit_bytes=VMEM_LIMIT),
        name="proj",
    )(x2, norm_w, w_cat)


def _ssm_kernel(u_ref, toep_ref, bs_ref, cs_ref, are_ref, aim_ref, y_ref, s_scr, xf_scr, xb_scr,
                *, nseq, nch):
    n_rows = nseq * nch
    rb = math.gcd(n_rows, 512)
    for r in range(n_rows // rb):
        rows = pl.ds(r * rb, rb)
        st = jnp.dot(u_ref[0, rows, :], bs_ref[0], preferred_element_type=F32)
        s_scr[0, rows, :] = st[:, :LANES]
        s_scr[1, rows, :] = st[:, LANES:]

    are = jnp.broadcast_to(are_ref[0], (nseq, LANES))
    aim = jnp.broadcast_to(aim_ref[0], (nseq, LANES))
    is_fwd = lax.broadcasted_iota(jnp.int32, (nseq, LANES), 1) < SSM_STATE

    def step(i, carry):
        xr, xi = carry
        fwd_rows = pl.ds(i, nseq, stride=nch)
        bwd_rows = pl.ds(nch - 1 - i, nseq, stride=nch)
        xf_scr[0, fwd_rows, :] = xr
        xf_scr[1, fwd_rows, :] = xi
        xb_scr[0, bwd_rows, :] = xr
        xb_scr[1, bwd_rows, :] = xi
        s_re = jnp.where(is_fwd, s_scr[0, fwd_rows, :], s_scr[0, bwd_rows, :])
        s_im = jnp.where(is_fwd, s_scr[1, fwd_rows, :], s_scr[1, bwd_rows, :])
        return (are * xr - aim * xi + s_re, are * xi + aim * xr + s_im)

    zero = jnp.zeros((nseq, LANES), F32)
    lax.fori_loop(0, nch, step, (zero, zero))

    fwd_lane = lax.broadcasted_iota(jnp.int32, (rb, LANES), 1) < SSM_STATE
    for r in range(n_rows // rb):
        rows = pl.ds(r * rb, rb)
        xin = jnp.concatenate(
            [jnp.where(fwd_lane, xf_scr[c, rows, :], xb_scr[c, rows, :]) for c in range(2)], axis=1).astype(BF16)
        y_ref[0, rows, :] = (jnp.dot(u_ref[0, rows, :], toep_ref[0], preferred_element_type=F32)
                             + jnp.dot(xin, cs_ref[0], preferred_element_type=F32))


def _ssm_call(up, ops, nseq, nch):
    toep, bs, cs, are, aim = ops
    n_rows = nseq * nch
    grp = lambda g: (g, 0, 0)
    return pl.pallas_call(
        functools.partial(_ssm_kernel, nseq=nseq, nch=nch),
        grid=(N_GROUPS,),
        in_specs=[
            pl.BlockSpec((1, n_rows, CHUNK_W), grp),
            pl.BlockSpec((1, CHUNK_W, CHUNK_W), grp),
            pl.BlockSpec((1, CHUNK_W, STATE_W), grp),
            pl.BlockSpec((1, STATE_W, CHUNK_W), grp),
            pl.BlockSpec((1, 1, LANES), grp),
            pl.BlockSpec((1, 1, LANES), grp),
        ],
        out_specs=pl.BlockSpec((1, n_rows, CHUNK_W), grp),
        out_shape=jax.ShapeDtypeStruct((N_GROUPS, n_rows, CHUNK_W), F32),
        scratch_shapes=[
            pltpu.VMEM((2, n_rows, LANES), F32),
            pltpu.VMEM((2, n_rows, LANES), F32),
            pltpu.VMEM((2, n_rows, LANES), F32),
        ],
        compiler_params=pltpu.CompilerParams(
            dimension_semantics=("arbitrary",), vmem_limit_bytes=VMEM_LIMIT),
        name="ssm",
    )(up, toep, bs, cs, are, aim)


def _attn_kernel(sink_ref, q_ref, kd_ref, vd_ref, bias_ref, mask_ref, o_ref, *, seq_len):
    jb = pl.program_id(1)
    nb = seq_len // BLOCK
    low_kv = lax.broadcasted_iota(jnp.int32, (KEYS, LANES), 1) < HEAD_DIM
    low_q = lax.broadcasted_iota(jnp.int32, (BLOCK, LANES), 1) < HEAD_DIM
    for sb in range(Q_ROWS // BLOCK):
        n = jb * (Q_ROWS // BLOCK) + sb
        ks = pl.multiple_of(jnp.clip((n - 1) * BLOCK, 0, seq_len - KEYS), BLOCK)
        var = jnp.where(n == 0, 0, jnp.where(n == nb - 1, 2, 1))
        mk = mask_ref[var]
        qrows = pl.ds(sb * BLOCK, BLOCK)
        for kh in range(N_KV_HEADS):
            kd = kd_ref[pl.ds(ks, KEYS), pl.ds(kh * LANES, LANES)]
            vd = vd_ref[pl.ds(ks, KEYS), pl.ds(kh * LANES, LANES)]
            zero = jnp.zeros_like(kd)
            kcat = jnp.concatenate([jnp.where(low_kv, kd, zero), jnp.where(low_kv, zero, kd)], axis=0)
            vcat = jnp.concatenate([jnp.where(low_kv, vd, zero), jnp.where(low_kv, zero, vd)], axis=0)
            for jj in range(2):
                j = kh * 2 + jj
                qj = q_ref[qrows, pl.ds(j * LANES, LANES)]
                sc = lax.dot_general(qj, kcat, (((1,), (1,)), ((), ())), preferred_element_type=F32)
                sc = jnp.minimum(sc + bias_ref[var, j], mk)
                ps, invs = [], []
                for hh in range(2):
                    t = sc[:, hh * KEYS:(hh + 1) * KEYS]
                    sink = sink_ref[2 * j + hh]
                    m = jnp.maximum(jnp.max(t, axis=1, keepdims=True), sink)
                    p = jnp.exp(t - m)
                    denom = jnp.sum(p, axis=1, keepdims=True) + jnp.exp(sink - m)
                    ps.append(p)
                    invs.append(1.0 / denom)
                p2 = jnp.concatenate(ps, axis=1).astype(BF16)
                o = jnp.dot(p2, vcat, preferred_element_type=F32)
                inv = jnp.where(low_q, invs[0], invs[1])
                o_ref[qrows, pl.ds(j * LANES, LANES)] = (o * inv).astype(BF16)


def _attn_call(q, kd, vd, bias, mask, sink, batch, seq_len):
    assert seq_len % Q_ROWS == 0 and seq_len >= KEYS
    nq = seq_len // Q_ROWS
    return pl.pallas_call(
        functools.partial(_attn_kernel, seq_len=seq_len),
        grid_spec=pltpu.PrefetchScalarGridSpec(
            num_scalar_prefetch=1,
            grid=(batch, nq),
            in_specs=[
                pl.BlockSpec((Q_ROWS, D_ATTN), lambda b, j, s: (b * nq + j, 0)),
                pl.BlockSpec((seq_len, 2 * LANES), lambda b, j, s: (b, 0)),
                pl.BlockSpec((seq_len, 2 * LANES), lambda b, j, s: (b, 0)),
                pl.BlockSpec((3, N_HEADS // 2, BLOCK, 2 * KEYS), lambda b, j, s: (0, 0, 0, 0)),
                pl.BlockSpec((3, BLOCK, 2 * KEYS), lambda b, j, s: (0, 0, 0)),
            ],
            out_specs=pl.BlockSpec((Q_ROWS, D_ATTN), lambda b, j, s: (b * nq + j, 0)),
        ),
        out_shape=jax.ShapeDtypeStruct((batch * seq_len, D_ATTN), BF16),
        compiler_params=pltpu.CompilerParams(
            dimension_semantics=("arbitrary", "arbitrary"), vmem_limit_bytes=VMEM_LIMIT),
        name="attn",
    )(sink, q, kd, vd, bias, mask)


def _gelu_tanh(x):
    c = math.sqrt(2.0 / math.pi)
    return x * (0.5 * (1.0 + jnp.tanh(c * (x + 0.044715 * (x * x * x)))))


def _silu(x):
    return x * jax.nn.sigmoid(x)


def _post_kernel(x_ref, yp_ref, zs_ref, ao_ref, za_ref, wglu_ref, bglu_ref, snw_ref, anw_ref,
                 wout_ref, fnw_ref, o_ref, y_scr, t_scr):
    s = pl.program_id(1)

    @pl.when(s == 0)
    def _():
        for g in range(N_GROUPS):
            j, g3 = divmod(g, GROUPS_PER_LANE_TILE)
            for th in range(CHUNK_W // LANES):
                tt = yp_ref[g, :, pl.ds(th * LANES, LANES)].T
                for t3 in range(8):
                    t_scr[th * 8 + t3, j, pl.ds(g3 * SSM_GROUP, SSM_GROUP), :] = (
                        tt[t3 * SSM_GROUP:(t3 + 1) * SSM_GROUP, :])
        for t in range(CHUNK):
            for j in range(LANE_TILES):
                y_scr[j, pl.ds(t, TILE_CHUNKS, stride=CHUNK), :] = t_scr[t, j].T

    rows = pl.ds(pl.multiple_of(s * SUB, SUB), SUB)
    y = jnp.concatenate([y_scr[j, rows, :] for j in range(LANE_TILES)], axis=1)
    g = _gelu_tanh(y)
    gate = jax.nn.sigmoid(
        jnp.dot(g.astype(BF16), wglu_ref[...], preferred_element_type=F32) + bglu_ref[...])
    n_ssm = _rms(g * gate, snw_ref[...]) * _silu(zs_ref[...].astype(F32))
    n_attn = _rms(ao_ref[...].astype(F32), anw_ref[...]) * _silu(za_ref[...].astype(F32))
    mixed = jnp.concatenate([n_ssm, n_attn], axis=1).astype(BF16)
    res = x_ref[...] + jnp.dot(mixed, wout_ref[...], preferred_element_type=F32)
    o_ref[...] = _rms(res, fnw_ref[...])


def _post_call(x2, yp, zs, ao, za, w_glu, b_glu, ssm_norm_w, attn_norm_w, w_out, final_norm_w):
    n_tok = x2.shape[0]
    n_tiles = n_tok // TILE
    row = lambda i, s: (i * N_SUB + s, 0)
    const = lambda i, s: (0, 0)
    return pl.pallas_call(
        _post_kernel,
        grid=(n_tiles, N_SUB),
        in_specs=[
            pl.BlockSpec((SUB, D_MODEL), row),
            pl.BlockSpec((N_GROUPS, TILE_CHUNKS, CHUNK_W), lambda i, s: (0, i, 0)),
            pl.BlockSpec((SUB, D_SSM), row),
            pl.BlockSpec((SUB, D_ATTN), row),
            pl.BlockSpec((SUB, D_ATTN), row),
            pl.BlockSpec((D_SSM, D_SSM), const),
            pl.BlockSpec((1, D_SSM), const),
            pl.BlockSpec((1, D_SSM), const),
            pl.BlockSpec((1, D_ATTN), const),
            pl.BlockSpec((D_MODEL, D_MODEL), const),
            pl.BlockSpec((1, D_MODEL), const),
        ],
        out_specs=pl.BlockSpec((SUB, D_MODEL), row),
        out_shape=jax.ShapeDtypeStruct((n_tok, D_MODEL), F32),
        scratch_shapes=[
            pltpu.VMEM((LANE_TILES, TILE, LANES), F32),
            pltpu.VMEM((CHUNK, LANE_TILES, LANES, LANES), F32),
        ],
        compiler_params=pltpu.CompilerParams(
            dimension_semantics=("arbitrary", "arbitrary"), vmem_limit_bytes=VMEM_LIMIT),
        name="post",
    )(x2, yp, zs, ao, za, w_glu, b_glu, ssm_norm_w, attn_norm_w, w_out, final_norm_w)


def _ssm_operators(lam_re, lam_im, log_step, b_re, b_im, c_re, c_im, d_skip):
    hp = lax.Precision.HIGHEST
    lr, li = lam_re.astype(F32), lam_im.astype(F32)
    dt = jnp.exp(log_step.astype(F32))[..., None]
    k = jnp.arange(CHUNK + 1, dtype=F32)[:, None, None, None]
    mag = jnp.exp(k * (lr * dt))
    ph = k * (li * dt)
    pw_re, pw_im = mag * jnp.cos(ph), mag * jnp.sin(ph)
    lb_re, lb_im = pw_re[1], pw_im[1]
    den = lr * lr + li * li
    nr = lb_re - 1.0
    coef_re = ((nr * lr + lb_im * li) / den)[..., None]
    coef_im = ((lb_im * lr - nr * li) / den)[..., None]
    b_re, b_im = b_re.astype(F32), b_im.astype(F32)
    bb_re = coef_re * b_re - coef_im * b_im
    bb_im = coef_re * b_im + coef_im * b_re
    wb_re = pw_re[..., None] * bb_re - pw_im[..., None] * bb_im
    wb_im = pw_re[..., None] * bb_im + pw_im[..., None] * bb_re
    c_re, c_im = c_re.astype(F32), c_im.astype(F32)
    cp_re = c_re * pw_re[:, :, :, None, :] - c_im * pw_im[:, :, :, None, :]
    cp_im = c_re * pw_im[:, :, :, None, :] + c_im * pw_re[:, :, :, None, :]

    kmat = (jnp.einsum('dgop,kdgpi->kdgoi', c_re, wb_re, precision=hp)
            - jnp.einsum('dgop,kdgpi->kdgoi', c_im, wb_im, precision=hp))
    kf, kb = kmat[:CHUNK, 0], kmat[:CHUNK, 1]
    k0 = kf[0] + kb[0] + d_skip.astype(F32)[:, :, None] * jnp.eye(SSM_GROUP, dtype=F32)
    kall = jnp.concatenate([kb[:0:-1], k0[None], kf[1:]], axis=0)
    lag = np.arange(CHUNK)[None, :] - np.arange(CHUNK)[:, None] + (CHUNK - 1)
    toep = kall[lag]
    toep = jnp.transpose(toep, (2, 0, 4, 1, 3)).reshape(N_GROUPS, CHUNK_W, CHUNK_W)

    def in_rows(w):
        return jnp.transpose(w, (1, 0, 3, 2))

    bs = jnp.concatenate([
        in_rows(wb_re[CHUNK - 1::-1, 0]), in_rows(wb_re[:CHUNK, 1]),
        in_rows(wb_im[CHUNK - 1::-1, 0]), in_rows(wb_im[:CHUNK, 1])], axis=-1)
    bs = bs.reshape(N_GROUPS, CHUNK_W, STATE_W)

    def out_cols(w):
        return jnp.transpose(w, (1, 3, 0, 2))

    cs = jnp.concatenate([
        out_cols(cp_re[1:, 0]), out_cols(cp_re[:0:-1, 1]),
        out_cols(-cp_im[1:, 0]), out_cols(-cp_im[:0:-1, 1])], axis=1)
    cs = cs.reshape(N_GROUPS, STATE_W, CHUNK_W)

    are = jnp.concatenate([pw_re[CHUNK, 0], pw_re[CHUNK, 1]], axis=-1)[:, None, :]
    aim = jnp.concatenate([pw_im[CHUNK, 0], pw_im[CHUNK, 1]], axis=-1)[:, None, :]
    return toep.astype(BF16), bs.astype(BF16), cs.astype(BF16), are, aim


def _t5_buckets_np(rel):
    half = N_BUCKETS // 2
    max_exact = half // 2
    ret = np.where(rel > 0, half, 0)
    n = np.abs(rel)
    nf = np.maximum(n, 1).astype(np.float64)
    large = max_exact + (np.log(nf / max_exact) / math.log(MAX_DISTANCE / max_exact)
                         * (half - max_exact)).astype(np.int32)
    large = np.minimum(large, half - 1)
    return ret + np.where(n < max_exact, n, large)


def _attn_tables(rel_bias):
    q_idx = np.arange(BLOCK)[:, None]
    s_idx = np.arange(KEYS)[None, :]
    rel = np.stack([s_idx + off - q_idx for off in (0, -BLOCK, -2 * BLOCK)])
    valid = np.abs(rel) <= WINDOW
    bias = rel_bias.astype(F32)[_t5_buckets_np(rel)]
    bias = jnp.transpose(bias, (0, 3, 1, 2)).reshape(3, N_HEADS // 2, 2, BLOCK, KEYS)
    bias = jnp.concatenate([bias[:, :, 0], bias[:, :, 1]], axis=-1)
    mask = np.where(valid, np.inf, NEG_INF).astype(np.float32)
    mask = jnp.asarray(np.concatenate([mask, mask], axis=-1))
    return bias, mask


def _pack_w_in(w):
    u, zs, q, k, v, za = jnp.split(w, (512, 1024, 1536, 1664, 1792), axis=1)
    dup = lambda t: jnp.concatenate([t[:, :HEAD_DIM], t[:, :HEAD_DIM], t[:, HEAD_DIM:], t[:, HEAD_DIM:]], axis=1)
    return jnp.concatenate([u, zs, q, dup(k), dup(v), za], axis=1).astype(BF16)


def kernel(x_prompt, x_sample, norm_w, w_in, lam_re, lam_im, log_step, b_re, b_im, c_re, c_im, d_skip,
           w_glu, b_glu, ssm_norm_w, sink, attn_norm_w, w_out, rel_bias, final_norm_w):
    assert norm_w.shape[0] == 1, "single-layer encoder"
    ops = _ssm_operators(lam_re[0], lam_im[0], log_step[0], b_re[0], b_im[0], c_re[0], c_im[0], d_skip[0])
    bias, mask = _attn_tables(rel_bias)
    w_cat = _pack_w_in(w_in[0])
    w_glu_b = w_glu[0].astype(BF16)
    w_out_b = w_out[0].astype(BF16)
    sink_f = sink[0].astype(F32)

    outs = []
    for x in (x_prompt, x_sample):
        batch, seq_len, _ = x.shape
        assert seq_len % TILE == 0
        x2 = x.reshape(batch * seq_len, D_MODEL)
        up, zs, q, kd, vd, za = _proj_call(x2, norm_w, w_cat)
        yp = _ssm_call(up, ops, batch, seq_len // CHUNK)
        ao = _attn_call(q, kd, vd, bias, mask, sink_f, batch, seq_len)
        y = _post_call(x2, yp, zs, ao, za, w_glu_b, b_glu, ssm_norm_w, attn_norm_w, w_out_b,
                       final_norm_w.reshape(1, D_MODEL))
        outs.append(y.reshape(batch, seq_len, D_MODEL))
    return tuple(outs)
```

```python
import functools
import math

import numpy as np
import jax
import jax.numpy as jnp
from jax import lax
from jax.experimental import pallas as pl
from jax.experimental.pallas import tpu as pltpu

F32 = jnp.float32
BF16 = jnp.bfloat16

D_MODEL = 1024
D_SSM = 512
D_ATTN = 512
SSM_GROUP = 16
N_GROUPS = D_SSM // SSM_GROUP
SSM_STATE = 64
HEAD_DIM = 64
N_HEADS = D_ATTN // HEAD_DIM
N_KV_HEADS = 2
WINDOW = 128
BLOCK = 128
N_BUCKETS = 32
MAX_DISTANCE = 128
RMS_EPS = 1e-6
NEG_INF = -1e30

LANES = 128
SUBLANES = 8
CHUNK = 16
CHUNK_W = CHUNK * SSM_GROUP
STATE_W = 4 * SSM_STATE
SEQ_TILE = SUBLANES
TILE_CHUNKS = LANES // SEQ_TILE
TOK_TILE = TILE_CHUNKS * CHUNK
TILE_ROWS = SEQ_TILE * TOK_TILE
SUB_SEQS = 2
SUB = SUB_SEQS * TOK_TILE
N_SUB = SEQ_TILE // SUB_SEQS
LANE_TILES = D_SSM // LANES
GROUPS_PER_LANE_TILE = LANES // SSM_GROUP
KEYS = 3 * BLOCK
Q_ROWS = 512
MM_ROWS = 512
VMEM_LIMIT = 56 * 1024 * 1024

_C_U, _C_ZS, _C_Q, _C_K, _C_V, _C_ZA, _C_END = 0, 512, 1024, 1536, 1792, 2048, 2560


def _rms(x, w):
    return (x * lax.rsqrt(jnp.mean(x * x, axis=-1, keepdims=True) + RMS_EPS)) * w


def _tile_specs(width):
    return pl.BlockSpec((SUB_SEQS, TOK_TILE, width), lambda sb, cb, s: (sb * N_SUB + s, cb, 0))


_CHUNK_TILE_SPEC = pl.BlockSpec((N_GROUPS, None, None, LANES, CHUNK_W), lambda sb, cb, s: (0, sb, cb, 0, 0))


def _const_spec(shape):
    return pl.BlockSpec(shape, lambda sb, cb, s: (0,) * len(shape))


def _proj_kernel(x_ref, nw_ref, w_ref, up_ref, zs_ref, q_ref, kd_ref, vd_ref, za_ref, u_scr, t_scr):
    s = pl.program_id(2)
    hb = _rms(x_ref[...].reshape(SUB, D_MODEL), nw_ref[...]).astype(BF16)

    def mm(lo, hi):
        return jnp.dot(hb, w_ref[:, lo:hi], preferred_element_type=F32)

    def put(ref, val):
        ref[...] = val.astype(BF16).reshape(ref.shape)

    u = mm(_C_U, _C_ZS)
    for j in range(LANE_TILES):
        u_scr[j, pl.ds(pl.multiple_of(s * SUB, SUB), SUB), :] = u[:, j * LANES:(j + 1) * LANES]
    put(zs_ref, mm(_C_ZS, _C_Q))
    put(q_ref, mm(_C_Q, _C_K) * (HEAD_DIM ** -0.5))
    put(kd_ref, mm(_C_K, _C_V))
    put(vd_ref, mm(_C_V, _C_ZA))
    put(za_ref, mm(_C_ZA, _C_END))

    @pl.when(s == N_SUB - 1)
    def _():
        for t in range(CHUNK):
            for j in range(LANE_TILES):
                blk = jnp.concatenate(
                    [u_scr[j, pl.ds(c * CHUNK + t, SEQ_TILE, stride=TOK_TILE), :] for c in range(TILE_CHUNKS)],
                    axis=0)
                t_scr[t, j] = blk.T
        for g in range(N_GROUPS):
            j, g3 = divmod(g, GROUPS_PER_LANE_TILE)
            for th in range(CHUNK_W // LANES):
                rows = jnp.concatenate(
                    [t_scr[th * 8 + t3, j, pl.ds(g3 * SSM_GROUP, SSM_GROUP), :] for t3 in range(8)], axis=0)
                up_ref[g, :, pl.ds(th * LANES, LANES)] = rows.T.astype(BF16)


def _proj_call(x, norm_w, w_cat):
    batch, seq_len, _ = x.shape
    nsb, ncb = batch // SEQ_TILE, seq_len // TOK_TILE
    tok = lambda width: jax.ShapeDtypeStruct((batch, seq_len, width), BF16)
    out_shapes = (
        jax.ShapeDtypeStruct((N_GROUPS, nsb, ncb, LANES, CHUNK_W), BF16),
        tok(D_SSM),
        tok(D_ATTN),
        tok(2 * LANES),
        tok(2 * LANES),
        tok(D_ATTN),
    )
    return pl.pallas_call(
        _proj_kernel,
        grid=(nsb, ncb, N_SUB),
        in_specs=[_tile_specs(D_MODEL), _const_spec((1, D_MODEL)), _const_spec((D_MODEL, _C_END))],
        out_specs=(_CHUNK_TILE_SPEC, _tile_specs(D_SSM), _tile_specs(D_ATTN), _tile_specs(2 * LANES),
                   _tile_specs(2 * LANES), _tile_specs(D_ATTN)),
        out_shape=out_shapes,
        scratch_shapes=[
            pltpu.VMEM((LANE_TILES, TILE_ROWS, LANES), F32),
            pltpu.VMEM((CHUNK, LANE_TILES, LANES, LANES), F32),
        ],
        compiler_params=pltpu.CompilerParams(
            dimension_semantics=("arbitrary", "arbitrary", "arbitrary"), vmem_limit_bytes=VMEM_LIMIT),
        name="proj",
    )(x, norm_w, w_cat)


def _ssm_kernel(u_ref, toep_ref, bs_ref, cs_ref, are_ref, aim_ref, y_ref, s_scr, xf_scr, xb_scr,
                *, nsb, ncb):
    nch = ncb * TILE_CHUNKS
    tiles_per_mm = MM_ROWS // LANES
    seq_rows = nsb * SEQ_TILE

    def mm_blocks():
        for sb in range(nsb):
            for cq in range(ncb // tiles_per_mm):
                yield sb, cq * tiles_per_mm, pl.ds((sb * ncb + cq * tiles_per_mm) * LANES, MM_ROWS)

    def load_u(sb, cb0):
        return jnp.concatenate([u_ref[sb, cb0 + i] for i in range(tiles_per_mm)], axis=0)

    for sb, cb0, rows in mm_blocks():
        st = jnp.dot(load_u(sb, cb0), bs_ref[...], preferred_element_type=F32)
        s_scr[0, rows, :] = st[:, :LANES]
        s_scr[1, rows, :] = st[:, LANES:]

    def chunk_rows(sb, ch):
        return pl.ds(pl.multiple_of(sb * nch * SEQ_TILE + ch * SEQ_TILE, SEQ_TILE), SEQ_TILE)

    def load(ref, part, ch):
        return jnp.concatenate([ref[part, chunk_rows(sb, ch), :] for sb in range(nsb)], axis=0)

    def store(ref, part, ch, val):
        for sb in range(nsb):
            ref[part, chunk_rows(sb, ch), :] = val[sb * SEQ_TILE:(sb + 1) * SEQ_TILE, :]

    are = jnp.broadcast_to(are_ref[...], (seq_rows, LANES))
    aim = jnp.broadcast_to(aim_ref[...], (seq_rows, LANES))
    is_fwd = lax.broadcasted_iota(jnp.int32, (seq_rows, LANES), 1) < SSM_STATE

    def step(i, carry):
        xr, xi = carry
        j = nch - 1 - i
        store(xf_scr, 0, i, xr)
        store(xf_scr, 1, i, xi)
        store(xb_scr, 0, j, xr)
        store(xb_scr, 1, j, xi)
        s_re = jnp.where(is_fwd, load(s_scr, 0, i), load(s_scr, 0, j))
        s_im = jnp.where(is_fwd, load(s_scr, 1, i), load(s_scr, 1, j))
        return (are * xr - aim * xi + s_re, are * xi + aim * xr + s_im)

    zero = jnp.zeros((seq_rows, LANES), F32)
    lax.fori_loop(0, nch, step, (zero, zero), unroll=4)

    fwd_lane = lax.broadcasted_iota(jnp.int32, (MM_ROWS, LANES), 1) < SSM_STATE
    for sb, cb0, rows in mm_blocks():
        xin = jnp.concatenate(
            [jnp.where(fwd_lane, xf_scr[c, rows, :], xb_scr[c, rows, :]) for c in range(2)], axis=1).astype(BF16)
        y = (jnp.dot(load_u(sb, cb0), toep_ref[...], preferred_element_type=F32)
             + jnp.dot(xin, cs_ref[...], preferred_element_type=F32))
        for i in range(tiles_per_mm):
            y_ref[sb, cb0 + i] = y[i * LANES:(i + 1) * LANES, :]


def _ssm_call(up, ops):
    toep, bs, cs, are, aim = ops
    _, nsb, ncb, _, _ = up.shape
    n_rows = nsb * ncb * LANES
    assert ncb % (MM_ROWS // LANES) == 0
    data_spec = pl.BlockSpec((None, nsb, ncb, LANES, CHUNK_W), lambda g: (g, 0, 0, 0, 0))
    grp = lambda *shape: pl.BlockSpec((None,) + shape, lambda g: (g, 0, 0))
    return pl.pallas_call(
        functools.partial(_ssm_kernel, nsb=nsb, ncb=ncb),
        grid=(N_GROUPS,),
        in_specs=[data_spec, grp(CHUNK_W, CHUNK_W), grp(CHUNK_W, STATE_W), grp(STATE_W, CHUNK_W),
                  grp(1, LANES), grp(1, LANES)],
        out_specs=data_spec,
        out_shape=jax.ShapeDtypeStruct(up.shape, F32),
        scratch_shapes=[pltpu.VMEM((2, n_rows, LANES), F32)] * 3,
        compiler_params=pltpu.CompilerParams(
            dimension_semantics=("arbitrary",), vmem_limit_bytes=VMEM_LIMIT),
        name="ssm",
    )(up, toep, bs, cs, are, aim)


def _attn_kernel(sink_ref, q_ref, kd_ref, vd_ref, bias_ref, mask_ref, o_ref, *, seq_len):
    jb = pl.program_id(1)
    nb = seq_len // BLOCK
    low_kv = lax.broadcasted_iota(jnp.int32, (KEYS, LANES), 1) < HEAD_DIM
    low_q = lax.broadcasted_iota(jnp.int32, (BLOCK, LANES), 1) < HEAD_DIM
    for sb in range(Q_ROWS // BLOCK):
        n = jb * (Q_ROWS // BLOCK) + sb
        ks = pl.multiple_of(jnp.clip((n - 1) * BLOCK, 0, seq_len - KEYS), BLOCK)
        var = jnp.where(n == 0, 0, jnp.where(n == nb - 1, 2, 1))
        mk = mask_ref[var]
        qrows = pl.ds(sb * BLOCK, BLOCK)
        for kh in range(N_KV_HEADS):
            kd = kd_ref[pl.ds(ks, KEYS), pl.ds(kh * LANES, LANES)]
            vd = vd_ref[pl.ds(ks, KEYS), pl.ds(kh * LANES, LANES)]
            zero = jnp.zeros_like(kd)
            kcat = jnp.concatenate([jnp.where(low_kv, kd, zero), jnp.where(low_kv, zero, kd)], axis=0)
            vcat = jnp.concatenate([jnp.where(low_kv, vd, zero), jnp.where(low_kv, zero, vd)], axis=0)
            for jj in range(2):
                j = kh * 2 + jj
                qj = q_ref[qrows, pl.ds(j * LANES, LANES)]
                sc = lax.dot_general(qj, kcat, (((1,), (1,)), ((), ())), preferred_element_type=F32)
                sc = jnp.minimum(sc + bias_ref[var, j], mk)
                ps, invs = [], []
                for hh in range(2):
                    t = sc[:, hh * KEYS:(hh + 1) * KEYS]
                    sink = sink_ref[2 * j + hh]
                    m = jnp.maximum(jnp.max(t, axis=1, keepdims=True), sink)
                    p = jnp.exp(t - m)
                    denom = jnp.sum(p, axis=1, keepdims=True) + jnp.exp(sink - m)
                    ps.append(p)
                    invs.append(1.0 / denom)
                p2 = jnp.concatenate(ps, axis=1).astype(BF16)
                o = jnp.dot(p2, vcat, preferred_element_type=F32)
                inv = jnp.where(low_q, invs[0], invs[1])
                o_ref[qrows, pl.ds(j * LANES, LANES)] = (o * inv).astype(BF16)


def _attn_call(q, kd, vd, bias, mask, sink, batch, seq_len):
    assert seq_len % Q_ROWS == 0 and seq_len >= KEYS
    nq = seq_len // Q_ROWS
    return pl.pallas_call(
        functools.partial(_attn_kernel, seq_len=seq_len),
        grid_spec=pltpu.PrefetchScalarGridSpec(
            num_scalar_prefetch=1,
            grid=(batch, nq),
            in_specs=[
                pl.BlockSpec((Q_ROWS, D_ATTN), lambda b, j, s: (b * nq + j, 0)),
                pl.BlockSpec((seq_len, 2 * LANES), lambda b, j, s: (b, 0)),
                pl.BlockSpec((seq_len, 2 * LANES), lambda b, j, s: (b, 0)),
                pl.BlockSpec((3, N_HEADS // 2, BLOCK, 2 * KEYS), lambda b, j, s: (0, 0, 0, 0)),
                pl.BlockSpec((3, BLOCK, 2 * KEYS), lambda b, j, s: (0, 0, 0)),
            ],
            out_specs=pl.BlockSpec((Q_ROWS, D_ATTN), lambda b, j, s: (b * nq + j, 0)),
        ),
        out_shape=jax.ShapeDtypeStruct((batch * seq_len, D_ATTN), BF16),
        compiler_params=pltpu.CompilerParams(
            dimension_semantics=("arbitrary", "arbitrary"), vmem_limit_bytes=VMEM_LIMIT),
        name="attn",
    )(sink, q, kd, vd, bias, mask)


def _gelu_tanh(x):
    c = math.sqrt(2.0 / math.pi)
    return x * (0.5 * (1.0 + jnp.tanh(c * (x + 0.044715 * (x * x * x)))))


def _silu(x):
    return x * jax.nn.sigmoid(x)


def _post_kernel(x_ref, yp_ref, zs_ref, ao_ref, za_ref, wglu_ref, bglu_ref, snw_ref, anw_ref,
                 wout_ref, fnw_ref, o_ref, y_scr, t_scr):
    s = pl.program_id(2)

    @pl.when(s == 0)
    def _():
        for g in range(N_GROUPS):
            j, g3 = divmod(g, GROUPS_PER_LANE_TILE)
            for th in range(CHUNK_W // LANES):
                tt = yp_ref[g, :, pl.ds(th * LANES, LANES)].T
                for t3 in range(8):
                    t_scr[th * 8 + t3, j, pl.ds(g3 * SSM_GROUP, SSM_GROUP), :] = (
                        tt[t3 * SSM_GROUP:(t3 + 1) * SSM_GROUP, :])
        for t in range(CHUNK):
            for j in range(LANE_TILES):
                blk = t_scr[t, j].T
                for c in range(TILE_CHUNKS):
                    y_scr[j, pl.ds(c * CHUNK + t, SEQ_TILE, stride=TOK_TILE), :] = (
                        blk[c * SEQ_TILE:(c + 1) * SEQ_TILE, :])

    def get(ref):
        return ref[...].reshape(SUB, ref.shape[-1]).astype(F32)

    rows = pl.ds(pl.multiple_of(s * SUB, SUB), SUB)
    y = jnp.concatenate([y_scr[j, rows, :] for j in range(LANE_TILES)], axis=1)
    g = _gelu_tanh(y)
    gate = jax.nn.sigmoid(
        jnp.dot(g.astype(BF16), wglu_ref[...], preferred_element_type=F32) + bglu_ref[...])
    n_ssm = _rms(g * gate, snw_ref[...]) * _silu(get(zs_ref))
    n_attn = _rms(get(ao_ref), anw_ref[...]) * _silu(get(za_ref))
    mixed = jnp.concatenate([n_ssm, n_attn], axis=1).astype(BF16)
    res = get(x_ref) + jnp.dot(mixed, wout_ref[...], preferred_element_type=F32)
    o_ref[...] = _rms(res, fnw_ref[...]).reshape(o_ref.shape)


def _post_call(x, yp, zs, ao, za, w_glu, b_glu, ssm_norm_w, attn_norm_w, w_out, final_norm_w):
    batch, seq_len, _ = x.shape
    nsb, ncb = batch // SEQ_TILE, seq_len // TOK_TILE
    return pl.pallas_call(
        _post_kernel,
        grid=(nsb, ncb, N_SUB),
        in_specs=[
            _tile_specs(D_MODEL), _CHUNK_TILE_SPEC, _tile_specs(D_SSM), _tile_specs(D_ATTN), _tile_specs(D_ATTN),
            _const_spec((D_SSM, D_SSM)), _const_spec((1, D_SSM)), _const_spec((1, D_SSM)),
            _const_spec((1, D_ATTN)), _const_spec((D_MODEL, D_MODEL)), _const_spec((1, D_MODEL)),
        ],
        out_specs=_tile_specs(D_MODEL),
        out_shape=jax.ShapeDtypeStruct((batch, seq_len, D_MODEL), F32),
        scratch_shapes=[
            pltpu.VMEM((LANE_TILES, TILE_ROWS, LANES), F32),
            pltpu.VMEM((CHUNK, LANE_TILES, LANES, LANES), F32),
        ],
        compiler_params=pltpu.CompilerParams(
            dimension_semantics=("arbitrary", "arbitrary", "arbitrary"), vmem_limit_bytes=VMEM_LIMIT),
        name="post",
    )(x, yp, zs, ao, za, w_glu, b_glu, ssm_norm_w, attn_norm_w, w_out, final_norm_w)


def _ssm_operators(lam_re, lam_im, log_step, b_re, b_im, c_re, c_im, d_skip):
    hp = lax.Precision.HIGHEST
    lr, li = lam_re.astype(F32), lam_im.astype(F32)
    dt = jnp.exp(log_step.astype(F32))[..., None]
    k = jnp.arange(CHUNK + 1, dtype=F32)[:, None, None, None]
    mag = jnp.exp(k * (lr * dt))
    ph = k * (li * dt)
    pw_re, pw_im = mag * jnp.cos(ph), mag * jnp.sin(ph)
    lb_re, lb_im = pw_re[1], pw_im[1]
    den = lr * lr + li * li
    nr = lb_re - 1.0
    coef_re = ((nr * lr + lb_im * li) / den)[..., None]
    coef_im = ((lb_im * lr - nr * li) / den)[..., None]
    b_re, b_im = b_re.astype(F32), b_im.astype(F32)
    bb_re = coef_re * b_re - coef_im * b_im
    bb_im = coef_re * b_im + coef_im * b_re
    wb_re = pw_re[..., None] * bb_re - pw_im[..., None] * bb_im
    wb_im = pw_re[..., None] * bb_im + pw_im[..., None] * bb_re
    c_re, c_im = c_re.astype(F32), c_im.astype(F32)
    cp_re = c_re * pw_re[:, :, :, None, :] - c_im * pw_im[:, :, :, None, :]
    cp_im = c_re * pw_im[:, :, :, None, :] + c_im * pw_re[:, :, :, None, :]

    kmat = (jnp.einsum('dgop,kdgpi->kdgoi', c_re, wb_re, precision=hp)
            - jnp.einsum('dgop,kdgpi->kdgoi', c_im, wb_im, precision=hp))
    kf, kb = kmat[:CHUNK, 0], kmat[:CHUNK, 1]
    k0 = kf[0] + kb[0] + d_skip.astype(F32)[:, :, None] * jnp.eye(SSM_GROUP, dtype=F32)
    kall = jnp.concatenate([kb[:0:-1], k0[None], kf[1:]], axis=0)
    toep = jnp.stack([kall[CHUNK - 1 - t:2 * CHUNK - 1 - t] for t in range(CHUNK)])
    toep = jnp.transpose(toep, (2, 0, 4, 1, 3)).reshape(N_GROUPS, CHUNK_W, CHUNK_W)

    def in_rows(w):
        return jnp.transpose(w, (1, 0, 3, 2))

    bs = jnp.concatenate([
        in_rows(wb_re[CHUNK - 1::-1, 0]), in_rows(wb_re[:CHUNK, 1]),
        in_rows(wb_im[CHUNK - 1::-1, 0]), in_rows(wb_im[:CHUNK, 1])], axis=-1)
    bs = bs.reshape(N_GROUPS, CHUNK_W, STATE_W)

    def out_cols(w):
        return jnp.transpose(w, (1, 3, 0, 2))

    cs = jnp.concatenate([
        out_cols(cp_re[1:, 0]), out_cols(cp_re[:0:-1, 1]),
        out_cols(-cp_im[1:, 0]), out_cols(-cp_im[:0:-1, 1])], axis=1)
    cs = cs.reshape(N_GROUPS, STATE_W, CHUNK_W)

    are = jnp.concatenate([pw_re[CHUNK, 0], pw_re[CHUNK, 1]], axis=-1)[:, None, :]
    aim = jnp.concatenate([pw_im[CHUNK, 0], pw_im[CHUNK, 1]], axis=-1)[:, None, :]
    return toep.astype(BF16), bs.astype(BF16), cs.astype(BF16), are, aim


def _t5_buckets_np(rel):
    half = N_BUCKETS // 2
    max_exact = half // 2
    ret = np.where(rel > 0, half, 0)
    n = np.abs(rel)
    nf = np.maximum(n, 1).astype(np.float64)
    large = max_exact + (np.log(nf / max_exact) / math.log(MAX_DISTANCE / max_exact)
                         * (half - max_exact)).astype(np.int32)
    large = np.minimum(large, half - 1)
    return ret + np.where(n < max_exact, n, large)


def _attn_tables(rel_bias):
    offsets = (0, -BLOCK, -2 * BLOCK)
    q_idx = np.arange(BLOCK)[:, None]
    s_idx = np.arange(KEYS)[None, :]
    rel = np.stack([s_idx + off - q_idx for off in offsets])
    valid = np.abs(rel) <= WINDOW
    period = 4 * BLOCK
    dist = np.arange(period)
    dist = np.where(dist < KEYS, dist, dist - period)
    per_dist = rel_bias.astype(F32)[_t5_buckets_np(np.stack([dist + off for off in offsets]))]
    per_dist = jnp.transpose(per_dist, (0, 2, 1))
    skew = jnp.tile(per_dist, (1, 1, BLOCK))[:, :, :BLOCK * (period - 1)]
    bias = skew.reshape(3, N_HEADS, BLOCK, period - 1)[:, :, :, :KEYS]
    bias = bias.reshape(3, N_HEADS // 2, 2, BLOCK, KEYS)
    bias = jnp.concatenate([bias[:, :, 0], bias[:, :, 1]], axis=-1)
    mask = np.where(valid, np.inf, NEG_INF).astype(np.float32)
    mask = jnp.asarray(np.concatenate([mask, mask], axis=-1))
    return bias, mask


def _pack_w_in(w):
    u, zs, q, k, v, za = jnp.split(w, (512, 1024, 1536, 1664, 1792), axis=1)
    dup = lambda t: jnp.concatenate([t[:, :HEAD_DIM], t[:, :HEAD_DIM], t[:, HEAD_DIM:], t[:, HEAD_DIM:]], axis=1)
    return jnp.concatenate([u, zs, q, dup(k), dup(v), za], axis=1).astype(BF16)


def kernel(x_prompt, x_sample, norm_w, w_in, lam_re, lam_im, log_step, b_re, b_im, c_re, c_im, d_skip,
           w_glu, b_glu, ssm_norm_w, sink, attn_norm_w, w_out, rel_bias, final_norm_w):
    assert norm_w.shape[0] == 1, "single-layer encoder"
    ops = _ssm_operators(lam_re[0], lam_im[0], log_step[0], b_re[0], b_im[0], c_re[0], c_im[0], d_skip[0])
    bias, mask = _attn_tables(rel_bias)
    w_cat = _pack_w_in(w_in[0])
    w_glu_b = w_glu[0].astype(BF16)
    w_out_b = w_out[0].astype(BF16)
    sink_f = sink[0].astype(F32)

    outs = []
    for x in (x_prompt, x_sample):
        batch, seq_len, _ = x.shape
        assert batch % SEQ_TILE == 0 and seq_len % TOK_TILE == 0
        n_tok = batch * seq_len
        up, zs, q, kd, vd, za = _proj_call(x, norm_w, w_cat)
        yp = _ssm_call(up, ops)
        ao = _attn_call(q.reshape(n_tok, D_ATTN), kd.reshape(n_tok, 2 * LANES), vd.reshape(n_tok, 2 * LANES),
                        bias, mask, sink_f, batch, seq_len)
        outs.append(_post_call(x, yp, zs, ao.reshape(batch, seq_len, D_ATTN), za, w_glu_b, b_glu, ssm_norm_w,
                               attn_norm_w, w_out_b, final_norm_w.reshape(1, D_MODEL)))
    return tuple(outs)
```

```python
import functools
import math

import numpy as np
import jax
import jax.numpy as jnp
from jax import lax
from jax.experimental import pallas as pl
from jax.experimental.pallas import tpu as pltpu

F32 = jnp.float32
BF16 = jnp.bfloat16

D_MODEL = 1024
D_SSM = 512
D_ATTN = 512
SSM_GROUP = 16
N_GROUPS = D_SSM // SSM_GROUP
SSM_STATE = 64
HEAD_DIM = 64
N_HEADS = D_ATTN // HEAD_DIM
N_KV_HEADS = 2
WINDOW = 128
BLOCK = 128
N_BUCKETS = 32
MAX_DISTANCE = 128
RMS_EPS = 1e-6
NEG_INF = -1e30
LOG2_E = math.log2(math.e)

LANES = 128
SUBLANES = 8
CHUNK = 16
CHUNK_W = CHUNK * SSM_GROUP
STATE_W = 4 * SSM_STATE
SEQ_TILE = SUBLANES
TILE_CHUNKS = LANES // SEQ_TILE
TOK_TILE = TILE_CHUNKS * CHUNK
TILE_ROWS = SEQ_TILE * TOK_TILE
SUB_SEQS = 2
SUB = SUB_SEQS * TOK_TILE
N_SUB = SEQ_TILE // SUB_SEQS
LANE_TILES = D_SSM // LANES
GROUPS_PER_LANE_TILE = LANES // SSM_GROUP
KEYS = 3 * BLOCK
Q_ROWS = 512
MM_ROWS = 512
VMEM_LIMIT = 56 * 1024 * 1024

_C_U, _C_ZS, _C_Q, _C_K, _C_V, _C_ZA, _C_END = 0, 512, 1024, 1536, 1792, 2048, 2560


def _rms(x, w):
    return (x * lax.rsqrt(jnp.mean(x * x, axis=-1, keepdims=True) + RMS_EPS)) * w


def _tile_specs(width):
    return pl.BlockSpec((SUB_SEQS, TOK_TILE, width), lambda sb, cb, s: (sb * N_SUB + s, cb, 0))


_CHUNK_TILE_SPEC = pl.BlockSpec((N_GROUPS, None, None, LANES, CHUNK_W), lambda sb, cb, s: (0, sb, cb, 0, 0))


def _const_spec(shape):
    return pl.BlockSpec(shape, lambda sb, cb, s: (0,) * len(shape))


def _proj_kernel(x_ref, nw_ref, w_ref, up_ref, zs_ref, q_ref, kd_ref, vd_ref, za_ref, u_scr, t_scr):
    s = pl.program_id(2)
    hb = _rms(x_ref[...].reshape(SUB, D_MODEL), nw_ref[...]).astype(BF16)

    def mm(lo, hi):
        return jnp.dot(hb, w_ref[:, lo:hi], preferred_element_type=F32)

    def put(ref, val):
        ref[...] = val.astype(BF16).reshape(ref.shape)

    u = mm(_C_U, _C_ZS)
    for j in range(LANE_TILES):
        u_scr[j, pl.ds(pl.multiple_of(s * SUB, SUB), SUB), :] = u[:, j * LANES:(j + 1) * LANES]
    put(zs_ref, mm(_C_ZS, _C_Q))
    put(q_ref, mm(_C_Q, _C_K) * (HEAD_DIM ** -0.5 * LOG2_E))
    put(kd_ref, mm(_C_K, _C_V))
    put(vd_ref, mm(_C_V, _C_ZA))
    put(za_ref, mm(_C_ZA, _C_END))

    @pl.when(s == N_SUB - 1)
    def _():
        for t in range(CHUNK):
            for j in range(LANE_TILES):
                blk = jnp.concatenate(
                    [u_scr[j, pl.ds(c * CHUNK + t, SEQ_TILE, stride=TOK_TILE), :] for c in range(TILE_CHUNKS)],
                    axis=0)
                t_scr[t, j] = blk.T
        for g in range(N_GROUPS):
            j, g3 = divmod(g, GROUPS_PER_LANE_TILE)
            for th in range(CHUNK_W // LANES):
                rows = jnp.concatenate(
                    [t_scr[th * 8 + t3, j, pl.ds(g3 * SSM_GROUP, SSM_GROUP), :] for t3 in range(8)], axis=0)
                up_ref[g, :, pl.ds(th * LANES, LANES)] = rows.T.astype(BF16)


def _proj_call(x, norm_w, w_cat):
    batch, seq_len, _ = x.shape
    nsb, ncb = batch // SEQ_TILE, seq_len // TOK_TILE
    tok = lambda width: jax.ShapeDtypeStruct((batch, seq_len, width), BF16)
    out_shapes = (
        jax.ShapeDtypeStruct((N_GROUPS, nsb, ncb, LANES, CHUNK_W), BF16),
        tok(D_SSM),
        tok(D_ATTN),
        tok(2 * LANES),
        tok(2 * LANES),
        tok(D_ATTN),
    )
    return pl.pallas_call(
        _proj_kernel,
        grid=(nsb, ncb, N_SUB),
        in_specs=[_tile_specs(D_MODEL), _const_spec((1, D_MODEL)), _const_spec((D_MODEL, _C_END))],
        out_specs=(_CHUNK_TILE_SPEC, _tile_specs(D_SSM), _tile_specs(D_ATTN), _tile_specs(2 * LANES),
                   _tile_specs(2 * LANES), _tile_specs(D_ATTN)),
        out_shape=out_shapes,
        scratch_shapes=[
            pltpu.VMEM((LANE_TILES, TILE_ROWS, LANES), F32),
            pltpu.VMEM((CHUNK, LANE_TILES, LANES, LANES), F32),
        ],
        compiler_params=pltpu.CompilerParams(
            dimension_semantics=("arbitrary", "arbitrary", "arbitrary"), vmem_limit_bytes=VMEM_LIMIT),
        name="proj",
    )(x, norm_w, w_cat)


def _ssm_kernel(u_ref, toep_ref, bs_ref, cs_ref, are_ref, aim_ref, y_ref, s_scr, xf_scr, xb_scr,
                *, nsb, ncb):
    nch = ncb * TILE_CHUNKS
    tiles_per_mm = MM_ROWS // LANES
    seq_rows = nsb * SEQ_TILE

    def mm_blocks():
        for sb in range(nsb):
            for cq in range(ncb // tiles_per_mm):
                yield sb, cq * tiles_per_mm, pl.ds((sb * ncb + cq * tiles_per_mm) * LANES, MM_ROWS)

    def load_u(sb, cb0):
        return jnp.concatenate([u_ref[sb, cb0 + i] for i in range(tiles_per_mm)], axis=0)

    for sb, cb0, rows in mm_blocks():
        st = jnp.dot(load_u(sb, cb0), bs_ref[...], preferred_element_type=F32)
        s_scr[0, rows, :] = st[:, :LANES]
        s_scr[1, rows, :] = st[:, LANES:]

    def chunk_rows(sb, ch):
        return pl.ds(pl.multiple_of(sb * nch * SEQ_TILE + ch * SEQ_TILE, SEQ_TILE), SEQ_TILE)

    def load(ref, part, ch):
        return jnp.concatenate([ref[part, chunk_rows(sb, ch), :] for sb in range(nsb)], axis=0)

    def store(ref, part, ch, val):
        for sb in range(nsb):
            ref[part, chunk_rows(sb, ch), :] = val[sb * SEQ_TILE:(sb + 1) * SEQ_TILE, :]

    are = jnp.broadcast_to(are_ref[...], (seq_rows, LANES))
    aim = jnp.broadcast_to(aim_ref[...], (seq_rows, LANES))
    is_fwd = lax.broadcasted_iota(jnp.int32, (seq_rows, LANES), 1) < SSM_STATE

    def step(i, carry):
        xr, xi = carry
        j = nch - 1 - i
        store(xf_scr, 0, i, xr)
        store(xf_scr, 1, i, xi)
        store(xb_scr, 0, j, xr)
        store(xb_scr, 1, j, xi)
        s_re = jnp.where(is_fwd, load(s_scr, 0, i), load(s_scr, 0, j))
        s_im = jnp.where(is_fwd, load(s_scr, 1, i), load(s_scr, 1, j))
        return (are * xr - aim * xi + s_re, are * xi + aim * xr + s_im)

    zero = jnp.zeros((seq_rows, LANES), F32)
    lax.fori_loop(0, nch, step, (zero, zero), unroll=4)

    fwd_lane = lax.broadcasted_iota(jnp.int32, (MM_ROWS, LANES), 1) < SSM_STATE
    for sb, cb0, rows in mm_blocks():
        xin = jnp.concatenate(
            [jnp.where(fwd_lane, xf_scr[c, rows, :], xb_scr[c, rows, :]) for c in range(2)], axis=1).astype(BF16)
        y = (jnp.dot(load_u(sb, cb0), toep_ref[...], preferred_element_type=F32)
             + jnp.dot(xin, cs_ref[...], preferred_element_type=F32))
        for i in range(tiles_per_mm):
            y_ref[sb, cb0 + i] = y[i * LANES:(i + 1) * LANES, :]


def _ssm_call(up, ops):
    toep, bs, cs, are, aim = ops
    _, nsb, ncb, _, _ = up.shape
    n_rows = nsb * ncb * LANES
    assert ncb % (MM_ROWS // LANES) == 0
    data_spec = pl.BlockSpec((None, nsb, ncb, LANES, CHUNK_W), lambda g: (g, 0, 0, 0, 0))
    grp = lambda *shape: pl.BlockSpec((None,) + shape, lambda g: (g, 0, 0))
    return pl.pallas_call(
        functools.partial(_ssm_kernel, nsb=nsb, ncb=ncb),
        grid=(N_GROUPS,),
        in_specs=[data_spec, grp(CHUNK_W, CHUNK_W), grp(CHUNK_W, STATE_W), grp(STATE_W, CHUNK_W),
                  grp(1, LANES), grp(1, LANES)],
        out_specs=data_spec,
        out_shape=jax.ShapeDtypeStruct(up.shape, F32),
        scratch_shapes=[pltpu.VMEM((2, n_rows, LANES), F32)] * 3,
        compiler_params=pltpu.CompilerParams(
            dimension_semantics=("arbitrary",), vmem_limit_bytes=VMEM_LIMIT),
        name="ssm",
    )(up, toep, bs, cs, are, aim)


def _attn_kernel(sink_ref, q_ref, kd_ref, vd_ref, bias_ref, o_ref, ka_scr, kb_scr, wa_scr, wb_scr, *, seq_len):
    jb = pl.program_id(1)
    nb = seq_len // BLOCK

    @pl.when(jb == 0)
    def _():
        low = lax.broadcasted_iota(jnp.int32, (BLOCK, LANES), 1) < HEAD_DIM
        top = lax.broadcasted_iota(jnp.int32, (BLOCK, LANES), 0) < HEAD_DIM
        one_a = jnp.where(top, 1.0, 0.0)
        one_b = 1.0 - one_a

        def per_block(r, carry):
            rows = pl.ds(pl.multiple_of(r * BLOCK, BLOCK), BLOCK)
            for kh in range(N_KV_HEADS):
                kd = kd_ref[rows, pl.ds(kh * LANES, LANES)].astype(F32)
                ka_scr[kh, rows, :] = jnp.where(low, kd, 0.0).astype(BF16)
                kb_scr[kh, rows, :] = jnp.where(low, 0.0, kd).astype(BF16)
                vt = vd_ref[rows, pl.ds(kh * LANES, LANES)].astype(F32).T
                wa_scr[kh, r] = jnp.concatenate([jnp.where(top, vt, 0.0), one_a], axis=0).astype(BF16)
                wb_scr[kh, r] = jnp.concatenate([jnp.where(top, 0.0, vt), one_b], axis=0).astype(BF16)
            return carry

        lax.fori_loop(0, nb, per_block, 0)

    col_tile0 = lax.broadcasted_iota(jnp.int32, (1, 2 * LANES), 1) < LANES
    row_slot0 = lax.broadcasted_iota(jnp.int32, (BLOCK, 2 * LANES), 0) < HEAD_DIM
    for sb in range(Q_ROWS // BLOCK):
        n = jb * (Q_ROWS // BLOCK) + sb
        kb0 = jnp.clip(n - 1, 0, nb - KEYS // BLOCK)
        ks = pl.multiple_of(kb0 * BLOCK, BLOCK)
        var = jnp.where(n == 0, 0, jnp.where(n == nb - 1, 2, 1))
        qrows = pl.ds(sb * BLOCK, BLOCK)
        for kh in range(N_KV_HEADS):
            kcat = jnp.concatenate([ka_scr[kh, pl.ds(ks, KEYS), :], kb_scr[kh, pl.ds(ks, KEYS), :]], axis=0)
            q2 = jnp.concatenate([q_ref[qrows, pl.ds((2 * kh + jj) * LANES, LANES)] for jj in range(2)], axis=0)
            t = lax.dot_general(kcat, q2, (((1,), (1,)), ((), ())), preferred_element_type=F32)
            t = t + bias_ref[var, kh]
            ps, sink_terms = [], []
            for hh in range(2):
                th = t[hh * KEYS:(hh + 1) * KEYS]
                sink = jnp.where(col_tile0, sink_ref[4 * kh + hh], sink_ref[4 * kh + 2 + hh])
                m = jnp.maximum(jnp.max(th, axis=0, keepdims=True), sink)
                ps.append(jnp.exp2(th - m).astype(BF16))
                sink_terms.append(jnp.exp2(sink - m))
            w = jnp.concatenate([wa_scr[kh, kb0 + i] for i in range(KEYS // BLOCK)]
                                + [wb_scr[kh, kb0 + i] for i in range(KEYS // BLOCK)], axis=1)
            o2 = jnp.dot(w, jnp.concatenate(ps, axis=0), preferred_element_type=F32)
            denom = o2[LANES:] + jnp.where(row_slot0, sink_terms[0], sink_terms[1])
            on = o2[:LANES] / denom
            for jj in range(2):
                o_ref[qrows, pl.ds((2 * kh + jj) * LANES, LANES)] = (
                    on[:, jj * LANES:(jj + 1) * LANES].T.astype(BF16))


def _attn_call(q, kd, vd, bias, sink, batch, seq_len):
    assert seq_len % Q_ROWS == 0 and seq_len >= KEYS
    nq = seq_len // Q_ROWS
    nb = seq_len // BLOCK
    return pl.pallas_call(
        functools.partial(_attn_kernel, seq_len=seq_len),
        grid_spec=pltpu.PrefetchScalarGridSpec(
            num_scalar_prefetch=1,
            grid=(batch, nq),
            in_specs=[
                pl.BlockSpec((Q_ROWS, D_ATTN), lambda b, j, s: (b * nq + j, 0)),
                pl.BlockSpec((seq_len, 2 * LANES), lambda b, j, s: (b, 0)),
                pl.BlockSpec((seq_len, 2 * LANES), lambda b, j, s: (b, 0)),
                pl.BlockSpec((3, N_KV_HEADS, 2 * KEYS, 2 * LANES), lambda b, j, s: (0, 0, 0, 0)),
            ],
            out_specs=pl.BlockSpec((Q_ROWS, D_ATTN), lambda b, j, s: (b * nq + j, 0)),
            scratch_shapes=[
                pltpu.VMEM((N_KV_HEADS, seq_len, LANES), BF16),
                pltpu.VMEM((N_KV_HEADS, seq_len, LANES), BF16),
                pltpu.VMEM((N_KV_HEADS, nb, 2 * LANES, BLOCK), BF16),
                pltpu.VMEM((N_KV_HEADS, nb, 2 * LANES, BLOCK), BF16),
            ],
        ),
        out_shape=jax.ShapeDtypeStruct((batch * seq_len, D_ATTN), BF16),
        compiler_params=pltpu.CompilerParams(
            dimension_semantics=("arbitrary", "arbitrary"), vmem_limit_bytes=VMEM_LIMIT),
        name="attn",
    )(sink, q, kd, vd, bias)


def _gelu_tanh(x):
    c = math.sqrt(2.0 / math.pi)
    return x * (0.5 * (1.0 + jnp.tanh(c * (x + 0.044715 * (x * x * x)))))


def _silu(x):
    return x * jax.nn.sigmoid(x)


def _post_kernel(x_ref, yp_ref, zs_ref, ao_ref, za_ref, wglu_ref, bglu_ref, snw_ref, anw_ref,
                 wout_ref, fnw_ref, o_ref, y_scr, t_scr):
    s = pl.program_id(2)

    @pl.when(s == 0)
    def _():
        for g in range(N_GROUPS):
            j, g3 = divmod(g, GROUPS_PER_LANE_TILE)
            for th in range(CHUNK_W // LANES):
                tt = yp_ref[g, :, pl.ds(th * LANES, LANES)].T
                for t3 in range(8):
                    t_scr[th * 8 + t3, j, pl.ds(g3 * SSM_GROUP, SSM_GROUP), :] = (
                        tt[t3 * SSM_GROUP:(t3 + 1) * SSM_GROUP, :])
        for t in range(CHUNK):
            for j in range(LANE_TILES):
                blk = t_scr[t, j].T
                for c in range(TILE_CHUNKS):
                    y_scr[j, pl.ds(c * CHUNK + t, SEQ_TILE, stride=TOK_TILE), :] = (
                        blk[c * SEQ_TILE:(c + 1) * SEQ_TILE, :])

    def get(ref):
        return ref[...].reshape(SUB, ref.shape[-1]).astype(F32)

    rows = pl.ds(pl.multiple_of(s * SUB, SUB), SUB)
    y = jnp.concatenate([y_scr[j, rows, :] for j in range(LANE_TILES)], axis=1)
    g = _gelu_tanh(y)
    gate = jax.nn.sigmoid(
        jnp.dot(g.astype(BF16), wglu_ref[...], preferred_element_type=F32) + bglu_ref[...])
    n_ssm = _rms(g * gate, snw_ref[...]) * _silu(get(zs_ref))
    n_attn = _rms(get(ao_ref), anw_ref[...]) * _silu(get(za_ref))
    mixed = jnp.concatenate([n_ssm, n_attn], axis=1).astype(BF16)
    res = get(x_ref) + jnp.dot(mixed, wout_ref[...], preferred_element_type=F32)
    o_ref[...] = _rms(res, fnw_ref[...]).reshape(o_ref.shape)


def _post_call(x, yp, zs, ao, za, w_glu, b_glu, ssm_norm_w, attn_norm_w, w_out, final_norm_w):
    batch, seq_len, _ = x.shape
    nsb, ncb = batch // SEQ_TILE, seq_len // TOK_TILE
    return pl.pallas_call(
        _post_kernel,
        grid=(nsb, ncb, N_SUB),
        in_specs=[
            _tile_specs(D_MODEL), _CHUNK_TILE_SPEC, _tile_specs(D_SSM), _tile_specs(D_ATTN), _tile_specs(D_ATTN),
            _const_spec((D_SSM, D_SSM)), _const_spec((1, D_SSM)), _const_spec((1, D_SSM)),
            _const_spec((1, D_ATTN)), _const_spec((D_MODEL, D_MODEL)), _const_spec((1, D_MODEL)),
        ],
        out_specs=_tile_specs(D_MODEL),
        out_shape=jax.ShapeDtypeStruct((batch, seq_len, D_MODEL), F32),
        scratch_shapes=[
            pltpu.VMEM((LANE_TILES, TILE_ROWS, LANES), F32),
            pltpu.VMEM((CHUNK, LANE_TILES, LANES, LANES), F32),
        ],
        compiler_params=pltpu.CompilerParams(
            dimension_semantics=("arbitrary", "arbitrary", "arbitrary"), vmem_limit_bytes=VMEM_LIMIT),
        name="post",
    )(x, yp, zs, ao, za, w_glu, b_glu, ssm_norm_w, attn_norm_w, w_out, final_norm_w)


def _ssm_operators(lam_re, lam_im, log_step, b_re, b_im, c_re, c_im, d_skip):
    hp = lax.Precision.HIGHEST
    lr, li = lam_re.astype(F32), lam_im.astype(F32)
    dt = jnp.exp(log_step.astype(F32))[..., None]
    k = jnp.arange(CHUNK + 1, dtype=F32)[:, None, None, None]
    mag = jnp.exp(k * (lr * dt))
    ph = k * (li * dt)
    pw_re, pw_im = mag * jnp.cos(ph), mag * jnp.sin(ph)
    lb_re, lb_im = pw_re[1], pw_im[1]
    den = lr * lr + li * li
    nr = lb_re - 1.0
    coef_re = ((nr * lr + lb_im * li) / den)[..., None]
    coef_im = ((lb_im * lr - nr * li) / den)[..., None]
    b_re, b_im = b_re.astype(F32), b_im.astype(F32)
    bb_re = coef_re * b_re - coef_im * b_im
    bb_im = coef_re * b_im + coef_im * b_re
    wb_re = pw_re[..., None] * bb_re - pw_im[..., None] * bb_im
    wb_im = pw_re[..., None] * bb_im + pw_im[..., None] * bb_re
    c_re, c_im = c_re.astype(F32), c_im.astype(F32)
    cp_re = c_re * pw_re[:, :, :, None, :] - c_im * pw_im[:, :, :, None, :]
    cp_im = c_re * pw_im[:, :, :, None, :] + c_im * pw_re[:, :, :, None, :]

    kmat = (jnp.einsum('dgop,kdgpi->kdgoi', c_re, wb_re, precision=hp)
            - jnp.einsum('dgop,kdgpi->kdgoi', c_im, wb_im, precision=hp))
    kf, kb = kmat[:CHUNK, 0], kmat[:CHUNK, 1]
    k0 = kf[0] + kb[0] + d_skip.astype(F32)[:, :, None] * jnp.eye(SSM_GROUP, dtype=F32)
    kall = jnp.concatenate([kb[:0:-1], k0[None], kf[1:]], axis=0)
    toep = jnp.stack([kall[CHUNK - 1 - t:2 * CHUNK - 1 - t] for t in range(CHUNK)])
    toep = jnp.transpose(toep, (2, 0, 4, 1, 3)).reshape(N_GROUPS, CHUNK_W, CHUNK_W)

    def in_rows(w):
        return jnp.transpose(w, (1, 0, 3, 2))

    bs = jnp.concatenate([
        in_rows(wb_re[CHUNK - 1::-1, 0]), in_rows(wb_re[:CHUNK, 1]),
        in_rows(wb_im[CHUNK - 1::-1, 0]), in_rows(wb_im[:CHUNK, 1])], axis=-1)
    bs = bs.reshape(N_GROUPS, CHUNK_W, STATE_W)

    def out_cols(w):
        return jnp.transpose(w, (1, 3, 0, 2))

    cs = jnp.concatenate([
        out_cols(cp_re[1:, 0]), out_cols(cp_re[:0:-1, 1]),
        out_cols(-cp_im[1:, 0]), out_cols(-cp_im[:0:-1, 1])], axis=1)
    cs = cs.reshape(N_GROUPS, STATE_W, CHUNK_W)

    are = jnp.concatenate([pw_re[CHUNK, 0], pw_re[CHUNK, 1]], axis=-1)[:, None, :]
    aim = jnp.concatenate([pw_im[CHUNK, 0], pw_im[CHUNK, 1]], axis=-1)[:, None, :]
    return toep.astype(BF16), bs.astype(BF16), cs.astype(BF16), are, aim


def _t5_buckets_np(rel):
    half = N_BUCKETS // 2
    max_exact = half // 2
    ret = np.where(rel > 0, half, 0)
    n = np.abs(rel)
    nf = np.maximum(n, 1).astype(np.float64)
    large = max_exact + (np.log(nf / max_exact) / math.log(MAX_DISTANCE / max_exact)
                         * (half - max_exact)).astype(np.int32)
    large = np.minimum(large, half - 1)
    return ret + np.where(n < max_exact, n, large)


def _attn_tables(rel_bias):
    offsets = (0, -BLOCK, -2 * BLOCK)
    period = 4 * BLOCK
    dist = np.arange(period)
    dist = np.where(dist < KEYS, dist, dist - period)
    rel = np.stack([dist + off for off in offsets])
    per_dist = rel_bias.astype(F32)[_t5_buckets_np(rel)] * LOG2_E
    per_dist = jnp.where((np.abs(rel) <= WINDOW)[:, :, None], per_dist, NEG_INF)
    per_dist = jnp.transpose(per_dist, (0, 2, 1))
    neg = jnp.concatenate([per_dist[:, :, :1], per_dist[:, :, :0:-1]], axis=-1)
    skew = jnp.tile(neg, (1, 1, KEYS))[:, :, :KEYS * (period - 1)]
    bias = skew.reshape(3, N_HEADS, KEYS, period - 1)[:, :, :, :BLOCK]
    bias = bias.reshape(3, N_KV_HEADS, 2, 2, KEYS, BLOCK)
    bias = jnp.concatenate([bias[:, :, 0], bias[:, :, 1]], axis=-1)
    return bias.reshape(3, N_KV_HEADS, 2 * KEYS, 2 * LANES)


def _pack_w_in(w):
    u, zs, q, k, v, za = jnp.split(w, (512, 1024, 1536, 1664, 1792), axis=1)
    dup = lambda t: jnp.concatenate([t[:, :HEAD_DIM], t[:, :HEAD_DIM], t[:, HEAD_DIM:], t[:, HEAD_DIM:]], axis=1)
    return jnp.concatenate([u, zs, q, dup(k), dup(v), za], axis=1).astype(BF16)


def kernel(x_prompt, x_sample, norm_w, w_in, lam_re, lam_im, log_step, b_re, b_im, c_re, c_im, d_skip,
           w_glu, b_glu, ssm_norm_w, sink, attn_norm_w, w_out, rel_bias, final_norm_w):
    assert norm_w.shape[0] == 1, "single-layer encoder"
    ops = _ssm_operators(lam_re[0], lam_im[0], log_step[0], b_re[0], b_im[0], c_re[0], c_im[0], d_skip[0])
    bias = _attn_tables(rel_bias)
    w_cat = _pack_w_in(w_in[0])
    w_glu_b = w_glu[0].astype(BF16)
    w_out_b = w_out[0].astype(BF16)
    sink_f = sink[0].astype(F32) * LOG2_E

    outs = []
    for x in (x_prompt, x_sample):
        batch, seq_len, _ = x.shape
        assert batch % SEQ_TILE == 0 and seq_len % TOK_TILE == 0
        n_tok = batch * seq_len
        up, zs, q, kd, vd, za = _proj_call(x, norm_w, w_cat)
        yp = _ssm_call(up, ops)
        ao = _attn_call(q.reshape(n_tok, D_ATTN), kd.reshape(n_tok, 2 * LANES), vd.reshape(n_tok, 2 * LANES),
                        bias, sink_f, batch, seq_len)
        outs.append(_post_call(x, yp, zs, ao.reshape(batch, seq_len, D_ATTN), za, w_glu_b, b_glu, ssm_norm_w,
                               attn_norm_w, w_out_b, final_norm_w.reshape(1, D_MODEL)))
    return tuple(outs)
```

```python
import functools
import math

import numpy as np
import jax
import jax.numpy as jnp
from jax import lax
from jax.experimental import pallas as pl
from jax.experimental.pallas import tpu as pltpu

F32 = jnp.float32
BF16 = jnp.bfloat16

D_MODEL = 1024
D_SSM = 512
D_ATTN = 512
SSM_GROUP = 16
N_GROUPS = D_SSM // SSM_GROUP
SSM_STATE = 64
HEAD_DIM = 64
N_HEADS = D_ATTN // HEAD_DIM
N_KV_HEADS = 2
WINDOW = 128
BLOCK = 128
N_BUCKETS = 32
MAX_DISTANCE = 128
RMS_EPS = 1e-6
NEG_INF = -1e30
LOG2_E = math.log2(math.e)

LANES = 128
SUBLANES = 8
CHUNK = 16
CHUNK_W = CHUNK * SSM_GROUP
STATE_W = 4 * SSM_STATE
SEQ_TILE = SUBLANES
TILE_CHUNKS = LANES // SEQ_TILE
TOK_TILE = TILE_CHUNKS * CHUNK
TILE_ROWS = SEQ_TILE * TOK_TILE
SUB_SEQS = 2
SUB = SUB_SEQS * TOK_TILE
N_SUB = SEQ_TILE // SUB_SEQS
LANE_TILES = D_SSM // LANES
GROUPS_PER_LANE_TILE = LANES // SSM_GROUP
KEYS = 3 * BLOCK
Q_ROWS = 512
MM_ROWS = 512
VMEM_LIMIT = 56 * 1024 * 1024

_C_U, _C_ZS, _C_Q, _C_K, _C_V, _C_ZA, _C_END = 0, 512, 1024, 1536, 1792, 2048, 2560


def _rms(x, w):
    return (x * lax.rsqrt(jnp.mean(x * x, axis=-1, keepdims=True) + RMS_EPS)) * w


def _tile_specs(width):
    return pl.BlockSpec((SUB_SEQS, TOK_TILE, width), lambda sb, cb, s: (sb * N_SUB + s, cb, 0))


_CHUNK_TILE_SPEC = pl.BlockSpec((N_GROUPS, None, None, LANES, CHUNK_W), lambda sb, cb, s: (0, sb, cb, 0, 0))


def _const_spec(shape):
    return pl.BlockSpec(shape, lambda sb, cb, s: (0,) * len(shape))


def _seq_time_perm(seq_major_out):
    out_row = lax.broadcasted_iota(jnp.int32, (LANES, LANES), 0)
    in_row = lax.broadcasted_iota(jnp.int32, (LANES, LANES), 1)
    ts, st = (in_row, out_row) if seq_major_out else (out_row, in_row)
    hit = (ts // SEQ_TILE == st % CHUNK) & (ts % SEQ_TILE == st // CHUNK)
    return jnp.where(hit, 1.0, 0.0).astype(BF16)


def _proj_kernel(x_ref, nw_ref, w_ref, up_ref, zs_ref, q_ref, kd_ref, vd_ref, za_ref, u_scr, b_scr, t_scr):
    s = pl.program_id(2)
    hb = _rms(x_ref[...].reshape(SUB, D_MODEL), nw_ref[...]).astype(BF16)

    def mm(lo, hi):
        return jnp.dot(hb, w_ref[:, lo:hi], preferred_element_type=F32)

    def put(ref, val):
        ref[...] = val.astype(BF16).reshape(ref.shape)

    u_scr[pl.ds(pl.multiple_of(s * SUB, SUB), SUB), :] = mm(_C_U, _C_ZS).astype(BF16)
    put(zs_ref, mm(_C_ZS, _C_Q))
    put(q_ref, mm(_C_Q, _C_K) * (HEAD_DIM ** -0.5 * LOG2_E))
    put(kd_ref, mm(_C_K, _C_V))
    put(vd_ref, mm(_C_V, _C_ZA))
    put(za_ref, mm(_C_ZA, _C_END))

    @pl.when(s == N_SUB - 1)
    def _():
        perm = _seq_time_perm(seq_major_out=False)
        for c in range(TILE_CHUNKS):
            a = jnp.concatenate(
                [u_scr[pl.ds(sq * TOK_TILE + c * CHUNK, CHUNK), :] for sq in range(SEQ_TILE)], axis=0)
            b_scr[c] = jnp.dot(perm, a, preferred_element_type=F32)
        for t in range(CHUNK):
            for j in range(LANE_TILES):
                blk = jnp.concatenate(
                    [b_scr[c, pl.ds(t * SEQ_TILE, SEQ_TILE), pl.ds(j * LANES, LANES)] for c in range(TILE_CHUNKS)],
                    axis=0)
                t_scr[t, j] = blk.T
        for g in range(N_GROUPS):
            j, g3 = divmod(g, GROUPS_PER_LANE_TILE)
            for th in range(CHUNK_W // LANES):
                rows = jnp.concatenate(
                    [t_scr[th * 8 + t3, j, pl.ds(g3 * SSM_GROUP, SSM_GROUP), :] for t3 in range(8)], axis=0)
                up_ref[g, :, pl.ds(th * LANES, LANES)] = rows.T.astype(BF16)


def _proj_call(x, norm_w, w_cat):
    batch, seq_len, _ = x.shape
    nsb, ncb = batch // SEQ_TILE, seq_len // TOK_TILE
    tok = lambda width: jax.ShapeDtypeStruct((batch, seq_len, width), BF16)
    out_shapes = (
        jax.ShapeDtypeStruct((N_GROUPS, nsb, ncb, LANES, CHUNK_W), BF16),
        tok(D_SSM),
        tok(D_ATTN),
        tok(2 * LANES),
        tok(2 * LANES),
        tok(D_ATTN),
    )
    return pl.pallas_call(
        _proj_kernel,
        grid=(nsb, ncb, N_SUB),
        in_specs=[_tile_specs(D_MODEL), _const_spec((1, D_MODEL)), _const_spec((D_MODEL, _C_END))],
        out_specs=(_CHUNK_TILE_SPEC, _tile_specs(D_SSM), _tile_specs(D_ATTN), _tile_specs(2 * LANES),
                   _tile_specs(2 * LANES), _tile_specs(D_ATTN)),
        out_shape=out_shapes,
        scratch_shapes=[
            pltpu.VMEM((TILE_ROWS, D_SSM), BF16),
            pltpu.VMEM((TILE_CHUNKS, LANES, D_SSM), F32),
            pltpu.VMEM((CHUNK, LANE_TILES, LANES, LANES), F32),
        ],
        compiler_params=pltpu.CompilerParams(
            dimension_semantics=("arbitrary", "arbitrary", "arbitrary"), vmem_limit_bytes=VMEM_LIMIT),
        name="proj",
    )(x, norm_w, w_cat)


def _ssm_kernel(u_ref, toep_ref, bs_ref, cs_ref, are_ref, aim_ref, y_ref, s_scr, xf_scr, xb_scr,
                *, nsb, ncb):
    nch = ncb * TILE_CHUNKS
    tiles_per_mm = MM_ROWS // LANES
    seq_rows = nsb * SEQ_TILE

    def mm_blocks():
        for sb in range(nsb):
            for cq in range(ncb // tiles_per_mm):
                yield sb, cq * tiles_per_mm, pl.ds((sb * ncb + cq * tiles_per_mm) * LANES, MM_ROWS)

    def load_u(sb, cb0):
        return jnp.concatenate([u_ref[sb, cb0 + i] for i in range(tiles_per_mm)], axis=0)

    for sb, cb0, rows in mm_blocks():
        st = jnp.dot(load_u(sb, cb0), bs_ref[...], preferred_element_type=F32)
        s_scr[0, rows, :] = st[:, :LANES]
        s_scr[1, rows, :] = st[:, LANES:]

    def chunk_rows(sb, ch):
        return pl.ds(pl.multiple_of(sb * nch * SEQ_TILE + ch * SEQ_TILE, SEQ_TILE), SEQ_TILE)

    def load(ref, part, ch):
        return jnp.concatenate([ref[part, chunk_rows(sb, ch), :] for sb in range(nsb)], axis=0)

    def store(ref, part, ch, val):
        for sb in range(nsb):
            ref[part, chunk_rows(sb, ch), :] = val[sb * SEQ_TILE:(sb + 1) * SEQ_TILE, :]

    are = jnp.broadcast_to(are_ref[...], (seq_rows, LANES))
    aim = jnp.broadcast_to(aim_ref[...], (seq_rows, LANES))
    is_fwd = lax.broadcasted_iota(jnp.int32, (seq_rows, LANES), 1) < SSM_STATE

    def step(i, carry):
        xr, xi = carry
        j = nch - 1 - i
        store(xf_scr, 0, i, xr)
        store(xf_scr, 1, i, xi)
        store(xb_scr, 0, j, xr)
        store(xb_scr, 1, j, xi)
        s_re = jnp.where(is_fwd, load(s_scr, 0, i), load(s_scr, 0, j))
        s_im = jnp.where(is_fwd, load(s_scr, 1, i), load(s_scr, 1, j))
        return (are * xr - aim * xi + s_re, are * xi + aim * xr + s_im)

    zero = jnp.zeros((seq_rows, LANES), F32)
    lax.fori_loop(0, nch, step, (zero, zero), unroll=4)

    fwd_lane = lax.broadcasted_iota(jnp.int32, (MM_ROWS, LANES), 1) < SSM_STATE
    for sb, cb0, rows in mm_blocks():
        xin = jnp.concatenate(
            [jnp.where(fwd_lane, xf_scr[c, rows, :], xb_scr[c, rows, :]) for c in range(2)], axis=1).astype(BF16)
        y = (jnp.dot(load_u(sb, cb0), toep_ref[...], preferred_element_type=F32)
             + jnp.dot(xin, cs_ref[...], preferred_element_type=F32))
        for i in range(tiles_per_mm):
            y_ref[sb, cb0 + i] = y[i * LANES:(i + 1) * LANES, :]


def _ssm_call(up, ops):
    toep, bs, cs, are, aim = ops
    _, nsb, ncb, _, _ = up.shape
    n_rows = nsb * ncb * LANES
    assert ncb % (MM_ROWS // LANES) == 0
    data_spec = pl.BlockSpec((None, nsb, ncb, LANES, CHUNK_W), lambda g: (g, 0, 0, 0, 0))
    grp = lambda *shape: pl.BlockSpec((None,) + shape, lambda g: (g, 0, 0))
    return pl.pallas_call(
        functools.partial(_ssm_kernel, nsb=nsb, ncb=ncb),
        grid=(N_GROUPS,),
        in_specs=[data_spec, grp(CHUNK_W, CHUNK_W), grp(CHUNK_W, STATE_W), grp(STATE_W, CHUNK_W),
                  grp(1, LANES), grp(1, LANES)],
        out_specs=data_spec,
        out_shape=jax.ShapeDtypeStruct(up.shape, F32),
        scratch_shapes=[pltpu.VMEM((2, n_rows, LANES), F32)] * 3,
        compiler_params=pltpu.CompilerParams(
            dimension_semantics=("arbitrary",), vmem_limit_bytes=VMEM_LIMIT),
        name="ssm",
    )(up, toep, bs, cs, are, aim)


def _attn_kernel(sink_ref, q_ref, kd_ref, vd_ref, bias_ref, o_ref, ka_scr, kb_scr, wa_scr, wb_scr, *, seq_len):
    jb = pl.program_id(1)
    nb = seq_len // BLOCK

    @pl.when(jb == 0)
    def _():
        low = lax.broadcasted_iota(jnp.int32, (BLOCK, LANES), 1) < HEAD_DIM
        top = lax.broadcasted_iota(jnp.int32, (BLOCK, LANES), 0) < HEAD_DIM
        one_a = jnp.where(top, 1.0, 0.0)
        one_b = 1.0 - one_a

        def per_block(r, carry):
            rows = pl.ds(pl.multiple_of(r * BLOCK, BLOCK), BLOCK)
            for kh in range(N_KV_HEADS):
                kd = kd_ref[rows, pl.ds(kh * LANES, LANES)].astype(F32)
                ka_scr[kh, rows, :] = jnp.where(low, kd, 0.0).astype(BF16)
                kb_scr[kh, rows, :] = jnp.where(low, 0.0, kd).astype(BF16)
                vt = vd_ref[rows, pl.ds(kh * LANES, LANES)].astype(F32).T
                wa_scr[kh, r] = jnp.concatenate([jnp.where(top, vt, 0.0), one_a], axis=0).astype(BF16)
                wb_scr[kh, r] = jnp.concatenate([jnp.where(top, 0.0, vt), one_b], axis=0).astype(BF16)
            return carry

        lax.fori_loop(0, nb, per_block, 0)

    col_tile0 = lax.broadcasted_iota(jnp.int32, (1, 2 * LANES), 1) < LANES
    row_slot0 = lax.broadcasted_iota(jnp.int32, (BLOCK, 2 * LANES), 0) < HEAD_DIM
    for sb in range(Q_ROWS // BLOCK):
        n = jb * (Q_ROWS // BLOCK) + sb
        kb0 = jnp.clip(n - 1, 0, nb - KEYS // BLOCK)
        ks = pl.multiple_of(kb0 * BLOCK, BLOCK)
        var = jnp.where(n == 0, 0, jnp.where(n == nb - 1, 2, 1))
        qrows = pl.ds(sb * BLOCK, BLOCK)
        for kh in range(N_KV_HEADS):
            kcat = jnp.concatenate([ka_scr[kh, pl.ds(ks, KEYS), :], kb_scr[kh, pl.ds(ks, KEYS), :]], axis=0)
            q2 = jnp.concatenate([q_ref[qrows, pl.ds((2 * kh + jj) * LANES, LANES)] for jj in range(2)], axis=0)
            t = lax.dot_general(kcat, q2, (((1,), (1,)), ((), ())), preferred_element_type=F32)
            t = t + bias_ref[var, kh]
            ps, sink_terms = [], []
            for hh in range(2):
                th = t[hh * KEYS:(hh + 1) * KEYS]
                sink = jnp.where(col_tile0, sink_ref[4 * kh + hh], sink_ref[4 * kh + 2 + hh])
                m = jnp.maximum(jnp.max(th, axis=0, keepdims=True), sink)
                ps.append(jnp.exp2(th - m).astype(BF16))
                sink_terms.append(jnp.exp2(sink - m))
            w = jnp.concatenate([wa_scr[kh, kb0 + i] for i in range(KEYS // BLOCK)]
                                + [wb_scr[kh, kb0 + i] for i in range(KEYS // BLOCK)], axis=1)
            o2 = jnp.dot(w, jnp.concatenate(ps, axis=0), preferred_element_type=F32)
            denom = o2[LANES:] + jnp.where(row_slot0, sink_terms[0], sink_terms[1])
            on = o2[:LANES] / denom
            for jj in range(2):
                o_ref[qrows, pl.ds((2 * kh + jj) * LANES, LANES)] = (
                    on[:, jj * LANES:(jj + 1) * LANES].T.astype(BF16))


def _attn_call(q, kd, vd, bias, sink, batch, seq_len):
    assert seq_len % Q_ROWS == 0 and seq_len >= KEYS
    nq = seq_len // Q_ROWS
    nb = seq_len // BLOCK
    return pl.pallas_call(
        functools.partial(_attn_kernel, seq_len=seq_len),
        grid_spec=pltpu.PrefetchScalarGridSpec(
            num_scalar_prefetch=1,
            grid=(batch, nq),
            in_specs=[
                pl.BlockSpec((Q_ROWS, D_ATTN), lambda b, j, s: (b * nq + j, 0)),
                pl.BlockSpec((seq_len, 2 * LANES), lambda b, j, s: (b, 0)),
                pl.BlockSpec((seq_len, 2 * LANES), lambda b, j, s: (b, 0)),
                pl.BlockSpec((3, N_KV_HEADS, 2 * KEYS, 2 * LANES), lambda b, j, s: (0, 0, 0, 0)),
            ],
            out_specs=pl.BlockSpec((Q_ROWS, D_ATTN), lambda b, j, s: (b * nq + j, 0)),
            scratch_shapes=[
                pltpu.VMEM((N_KV_HEADS, seq_len, LANES), BF16),
                pltpu.VMEM((N_KV_HEADS, seq_len, LANES), BF16),
                pltpu.VMEM((N_KV_HEADS, nb, 2 * LANES, BLOCK), BF16),
                pltpu.VMEM((N_KV_HEADS, nb, 2 * LANES, BLOCK), BF16),
            ],
        ),
        out_shape=jax.ShapeDtypeStruct((batch * seq_len, D_ATTN), BF16),
        compiler_params=pltpu.CompilerParams(
            dimension_semantics=("arbitrary", "arbitrary"), vmem_limit_bytes=VMEM_LIMIT),
        name="attn",
    )(sink, q, kd, vd, bias)


def _gelu_tanh(x):
    c = math.sqrt(2.0 / math.pi)
    return x * (0.5 * (1.0 + jnp.tanh(c * (x + 0.044715 * (x * x * x)))))


def _silu(x):
    return x * jax.nn.sigmoid(x)


def _post_kernel(x_ref, yp_ref, zs_ref, ao_ref, za_ref, wglu_ref, bglu_ref, snw_ref, anw_ref,
                 wout_ref, fnw_ref, o_ref, y_scr, b_scr, t_scr):
    s = pl.program_id(2)

    @pl.when(s == 0)
    def _():
        for g in range(N_GROUPS):
            j, g3 = divmod(g, GROUPS_PER_LANE_TILE)
            for th in range(CHUNK_W // LANES):
                tt = yp_ref[g, :, pl.ds(th * LANES, LANES)].T
                for t3 in range(8):
                    t_scr[th * 8 + t3, j, pl.ds(g3 * SSM_GROUP, SSM_GROUP), :] = (
                        tt[t3 * SSM_GROUP:(t3 + 1) * SSM_GROUP, :])
        for t in range(CHUNK):
            for j in range(LANE_TILES):
                b_scr[t, :, pl.ds(j * LANES, LANES)] = t_scr[t, j].T
        perm = _seq_time_perm(seq_major_out=True)
        for c in range(TILE_CHUNKS):
            a = jnp.concatenate([b_scr[t, pl.ds(c * SEQ_TILE, SEQ_TILE), :] for t in range(CHUNK)], axis=0)
            hi = a.astype(BF16)
            lo = (a - hi.astype(F32)).astype(BF16)
            ys = (jnp.dot(perm, hi, preferred_element_type=F32)
                  + jnp.dot(perm, lo, preferred_element_type=F32))
            for sq in range(SEQ_TILE):
                y_scr[pl.ds(sq * TOK_TILE + c * CHUNK, CHUNK), :] = ys[sq * CHUNK:(sq + 1) * CHUNK, :]

    def get(ref):
        return ref[...].reshape(SUB, ref.shape[-1]).astype(F32)

    y = y_scr[pl.ds(pl.multiple_of(s * SUB, SUB), SUB), :]
    g = _gelu_tanh(y)
    gate = jax.nn.sigmoid(
        jnp.dot(g.astype(BF16), wglu_ref[...], preferred_element_type=F32) + bglu_ref[...])
    n_ssm = _rms(g * gate, snw_ref[...]) * _silu(get(zs_ref))
    n_attn = _rms(get(ao_ref), anw_ref[...]) * _silu(get(za_ref))
    mixed = jnp.concatenate([n_ssm, n_attn], axis=1).astype(BF16)
    res = get(x_ref) + jnp.dot(mixed, wout_ref[...], preferred_element_type=F32)
    o_ref[...] = _rms(res, fnw_ref[...]).reshape(o_ref.shape)


def _post_call(x, yp, zs, ao, za, w_glu, b_glu, ssm_norm_w, attn_norm_w, w_out, final_norm_w):
    batch, seq_len, _ = x.shape
    nsb, ncb = batch // SEQ_TILE, seq_len // TOK_TILE
    return pl.pallas_call(
        _post_kernel,
        grid=(nsb, ncb, N_SUB),
        in_specs=[
            _tile_specs(D_MODEL), _CHUNK_TILE_SPEC, _tile_specs(D_SSM), _tile_specs(D_ATTN), _tile_specs(D_ATTN),
            _const_spec((D_SSM, D_SSM)), _const_spec((1, D_SSM)), _const_spec((1, D_SSM)),
            _const_spec((1, D_ATTN)), _const_spec((D_MODEL, D_MODEL)), _const_spec((1, D_MODEL)),
        ],
        out_specs=_tile_specs(D_MODEL),
        out_shape=jax.ShapeDtypeStruct((batch, seq_len, D_MODEL), F32),
        scratch_shapes=[
            pltpu.VMEM((TILE_ROWS, D_SSM), F32),
            pltpu.VMEM((CHUNK, LANES, D_SSM), F32),
            pltpu.VMEM((CHUNK, LANE_TILES, LANES, LANES), F32),
        ],
        compiler_params=pltpu.CompilerParams(
            dimension_semantics=("arbitrary", "arbitrary", "arbitrary"), vmem_limit_bytes=VMEM_LIMIT),
        name="post",
    )(x, yp, zs, ao, za, w_glu, b_glu, ssm_norm_w, attn_norm_w, w_out, final_norm_w)


def _ssm_operators(lam_re, lam_im, log_step, b_re, b_im, c_re, c_im, d_skip):
    hp = lax.Precision.HIGHEST
    lr, li = lam_re.astype(F32), lam_im.astype(F32)
    dt = jnp.exp(log_step.astype(F32))[..., None]
    k = jnp.arange(CHUNK + 1, dtype=F32)[:, None, None, None]
    mag = jnp.exp(k * (lr * dt))
    ph = k * (li * dt)
    pw_re, pw_im = mag * jnp.cos(ph), mag * jnp.sin(ph)
    lb_re, lb_im = pw_re[1], pw_im[1]
    den = lr * lr + li * li
    nr = lb_re - 1.0
    coef_re = ((nr * lr + lb_im * li) / den)[..., None]
    coef_im = ((lb_im * lr - nr * li) / den)[..., None]
    b_re, b_im = b_re.astype(F32), b_im.astype(F32)
    bb_re = coef_re * b_re - coef_im * b_im
    bb_im = coef_re * b_im + coef_im * b_re
    wb_re = pw_re[..., None] * bb_re - pw_im[..., None] * bb_im
    wb_im = pw_re[..., None] * bb_im + pw_im[..., None] * bb_re
    c_re, c_im = c_re.astype(F32), c_im.astype(F32)
    cp_re = c_re * pw_re[:, :, :, None, :] - c_im * pw_im[:, :, :, None, :]
    cp_im = c_re * pw_im[:, :, :, None, :] + c_im * pw_re[:, :, :, None, :]

    kmat = (jnp.einsum('dgop,kdgpi->kdgoi', c_re, wb_re, precision=hp)
            - jnp.einsum('dgop,kdgpi->kdgoi', c_im, wb_im, precision=hp))
    kf, kb = kmat[:CHUNK, 0], kmat[:CHUNK, 1]
    k0 = kf[0] + kb[0] + d_skip.astype(F32)[:, :, None] * jnp.eye(SSM_GROUP, dtype=F32)
    kall = jnp.concatenate([kb[:0:-1], k0[None], kf[1:]], axis=0)
    toep = jnp.stack([kall[CHUNK - 1 - t:2 * CHUNK - 1 - t] for t in range(CHUNK)])
    toep = jnp.transpose(toep, (2, 0, 4, 1, 3)).reshape(N_GROUPS, CHUNK_W, CHUNK_W)

    def in_rows(w):
        return jnp.transpose(w, (1, 0, 3, 2))

    bs = jnp.concatenate([
        in_rows(wb_re[CHUNK - 1::-1, 0]), in_rows(wb_re[:CHUNK, 1]),
        in_rows(wb_im[CHUNK - 1::-1, 0]), in_rows(wb_im[:CHUNK, 1])], axis=-1)
    bs = bs.reshape(N_GROUPS, CHUNK_W, STATE_W)

    def out_cols(w):
        return jnp.transpose(w, (1, 3, 0, 2))

    cs = jnp.concatenate([
        out_cols(cp_re[1:, 0]), out_cols(cp_re[:0:-1, 1]),
        out_cols(-cp_im[1:, 0]), out_cols(-cp_im[:0:-1, 1])], axis=1)
    cs = cs.reshape(N_GROUPS, STATE_W, CHUNK_W)

    are = jnp.concatenate([pw_re[CHUNK, 0], pw_re[CHUNK, 1]], axis=-1)[:, None, :]
    aim = jnp.concatenate([pw_im[CHUNK, 0], pw_im[CHUNK, 1]], axis=-1)[:, None, :]
    return toep.astype(BF16), bs.astype(BF16), cs.astype(BF16), are, aim


def _t5_buckets_np(rel):
    half = N_BUCKETS // 2
    max_exact = half // 2
    ret = np.where(rel > 0, half, 0)
    n = np.abs(rel)
    nf = np.maximum(n, 1).astype(np.float64)
    large = max_exact + (np.log(nf / max_exact) / math.log(MAX_DISTANCE / max_exact)
                         * (half - max_exact)).astype(np.int32)
    large = np.minimum(large, half - 1)
    return ret + np.where(n < max_exact, n, large)


def _attn_tables(rel_bias):
    offsets = (0, -BLOCK, -2 * BLOCK)
    period = 4 * BLOCK
    dist = np.arange(period)
    dist = np.where(dist < KEYS, dist, dist - period)
    rel = np.stack([dist + off for off in offsets])
    per_dist = rel_bias.astype(F32)[_t5_buckets_np(rel)] * LOG2_E
    per_dist = jnp.where((np.abs(rel) <= WINDOW)[:, :, None], per_dist, NEG_INF)
    per_dist = jnp.transpose(per_dist, (0, 2, 1))
    neg = jnp.concatenate([per_dist[:, :, :1], per_dist[:, :, :0:-1]], axis=-1)
    skew = jnp.tile(neg, (1, 1, KEYS))[:, :, :KEYS * (period - 1)]
    bias = skew.reshape(3, N_HEADS, KEYS, period - 1)[:, :, :, :BLOCK]
    bias = bias.reshape(3, N_KV_HEADS, 2, 2, KEYS, BLOCK)
    bias = jnp.concatenate([bias[:, :, 0], bias[:, :, 1]], axis=-1)
    return bias.reshape(3, N_KV_HEADS, 2 * KEYS, 2 * LANES)


def _pack_w_in(w):
    u, zs, q, k, v, za = jnp.split(w, (512, 1024, 1536, 1664, 1792), axis=1)
    dup = lambda t: jnp.concatenate([t[:, :HEAD_DIM], t[:, :HEAD_DIM], t[:, HEAD_DIM:], t[:, HEAD_DIM:]], axis=1)
    return jnp.concatenate([u, zs, q, dup(k), dup(v), za], axis=1).astype(BF16)


def kernel(x_prompt, x_sample, norm_w, w_in, lam_re, lam_im, log_step, b_re, b_im, c_re, c_im, d_skip,
           w_glu, b_glu, ssm_norm_w, sink, attn_norm_w, w_out, rel_bias, final_norm_w):
    assert norm_w.shape[0] == 1, "single-layer encoder"
    ops = _ssm_operators(lam_re[0], lam_im[0], log_step[0], b_re[0], b_im[0], c_re[0], c_im[0], d_skip[0])
    bias = _attn_tables(rel_bias)
    w_cat = _pack_w_in(w_in[0])
    w_glu_b = w_glu[0].astype(BF16)
    w_out_b = w_out[0].astype(BF16)
    sink_f = sink[0].astype(F32) * LOG2_E

    outs = []
    for x in (x_prompt, x_sample):
        batch, seq_len, _ = x.shape
        assert batch % SEQ_TILE == 0 and seq_len % TOK_TILE == 0
        n_tok = batch * seq_len
        up, zs, q, kd, vd, za = _proj_call(x, norm_w, w_cat)
        yp = _ssm_call(up, ops)
        ao = _attn_call(q.reshape(n_tok, D_ATTN), kd.reshape(n_tok, 2 * LANES), vd.reshape(n_tok, 2 * LANES),
                        bias, sink_f, batch, seq_len)
        outs.append(_post_call(x, yp, zs, ao.reshape(batch, seq_len, D_ATTN), za, w_glu_b, b_glu, ssm_norm_w,
                               attn_norm_w, w_out_b, final_norm_w.reshape(1, D_MODEL)))
    return tuple(outs)
```

```python
import functools
import math

import numpy as np
import jax
import jax.numpy as jnp
from jax import lax
from jax.experimental import pallas as pl
from jax.experimental.pallas import tpu as pltpu

F32 = jnp.float32
BF16 = jnp.bfloat16

D_MODEL = 1024
D_SSM = 512
D_ATTN = 512
SSM_GROUP = 16
N_GROUPS = D_SSM // SSM_GROUP
SSM_STATE = 64
HEAD_DIM = 64
N_HEADS = D_ATTN // HEAD_DIM
N_KV_HEADS = 2
WINDOW = 128
BLOCK = 128
N_BUCKETS = 32
MAX_DISTANCE = 128
RMS_EPS = 1e-6
NEG_INF = -1e30
LOG2_E = math.log2(math.e)

LANES = 128
SUBLANES = 8
CHUNK = 16
CHUNK_W = CHUNK * SSM_GROUP
STATE_W = 4 * SSM_STATE
SEQ_TILE = SUBLANES
TILE_CHUNKS = LANES // SEQ_TILE
TOK_TILE = TILE_CHUNKS * CHUNK
TILE_ROWS = SEQ_TILE * TOK_TILE
SUB_SEQS = 2
SUB = SUB_SEQS * TOK_TILE
N_SUB = SEQ_TILE // SUB_SEQS
LANE_TILES = D_SSM // LANES
GROUPS_PER_LANE_TILE = LANES // SSM_GROUP
KEYS = 3 * BLOCK
Q_ROWS = 512
MM_ROWS = 512
VMEM_LIMIT = 56 * 1024 * 1024

_C_U, _C_ZS, _C_Q, _C_K, _C_V, _C_ZA, _C_END = 0, 512, 1024, 1536, 1792, 2048, 2560


def _rms(x, w):
    return (x * lax.rsqrt(jnp.mean(x * x, axis=-1, keepdims=True) + RMS_EPS)) * w


def _tile_specs(width):
    return pl.BlockSpec((SUB_SEQS, TOK_TILE, width), lambda sb, cb, s: (sb * N_SUB + s, cb, 0))


_CHUNK_TILE_SPEC = pl.BlockSpec((N_GROUPS, None, None, LANES, CHUNK_W), lambda sb, cb, s: (0, sb, cb, 0, 0))


def _const_spec(shape):
    return pl.BlockSpec(shape, lambda sb, cb, s: (0,) * len(shape))


PAIR_ROWS = 2 * SEQ_TILE * CHUNK


def _pair_perm(t_major_out):
    out_row = lax.broadcasted_iota(jnp.int32, (PAIR_ROWS, PAIR_ROWS), 0)
    in_row = lax.broadcasted_iota(jnp.int32, (PAIR_ROWS, PAIR_ROWS), 1)
    tcs, cst = (out_row, in_row) if t_major_out else (in_row, out_row)
    hit = (tcs // (2 * SEQ_TILE) == cst % CHUNK) & (tcs % (2 * SEQ_TILE) == cst // CHUNK)
    return jnp.where(hit, 1.0, 0.0).astype(BF16)


def _proj_kernel(x_ref, nw_ref, w_ref, up_ref, zs_ref, q_ref, kd_ref, vd_ref, za_ref, u_scr, b_scr, t_scr):
    s = pl.program_id(2)
    hb = _rms(x_ref[...].reshape(SUB, D_MODEL), nw_ref[...]).astype(BF16)

    def mm(lo, hi):
        return jnp.dot(hb, w_ref[:, lo:hi], preferred_element_type=F32)

    def put(ref, val):
        ref[...] = val.astype(BF16).reshape(ref.shape)

    u_scr[pl.ds(pl.multiple_of(s * SUB, SUB), SUB), :] = mm(_C_U, _C_ZS).astype(BF16)
    put(zs_ref, mm(_C_ZS, _C_Q))
    put(q_ref, mm(_C_Q, _C_K) * (HEAD_DIM ** -0.5 * LOG2_E))
    put(kd_ref, mm(_C_K, _C_V))
    put(vd_ref, mm(_C_V, _C_ZA))
    put(za_ref, mm(_C_ZA, _C_END))

    @pl.when(s == N_SUB - 1)
    def _():
        perm = _pair_perm(t_major_out=True)
        for cp in range(TILE_CHUNKS // 2):
            a = jnp.concatenate(
                [u_scr[pl.ds(sq * TOK_TILE + (2 * cp + c2) * CHUNK, CHUNK), :]
                 for c2 in range(2) for sq in range(SEQ_TILE)], axis=0)
            b_scr[cp] = jnp.dot(perm, a, preferred_element_type=F32).astype(BF16)
        for t in range(CHUNK):
            for j in range(LANE_TILES):
                blk = jnp.concatenate(
                    [b_scr[cp, pl.ds(t * 2 * SEQ_TILE, 2 * SEQ_TILE), pl.ds(j * LANES, LANES)]
                     for cp in range(TILE_CHUNKS // 2)], axis=0)
                t_scr[t, j] = blk.T
        for g in range(N_GROUPS):
            j, g3 = divmod(g, GROUPS_PER_LANE_TILE)
            for th in range(CHUNK_W // LANES):
                rows = jnp.concatenate(
                    [t_scr[th * 8 + t3, j, pl.ds(g3 * SSM_GROUP, SSM_GROUP), :] for t3 in range(8)], axis=0)
                up_ref[g, :, pl.ds(th * LANES, LANES)] = rows.T


def _proj_call(x, norm_w, w_cat):
    batch, seq_len, _ = x.shape
    nsb, ncb = batch // SEQ_TILE, seq_len // TOK_TILE
    tok = lambda width: jax.ShapeDtypeStruct((batch, seq_len, width), BF16)
    out_shapes = (
        jax.ShapeDtypeStruct((N_GROUPS, nsb, ncb, LANES, CHUNK_W), BF16),
        tok(D_SSM),
        tok(D_ATTN),
        tok(2 * LANES),
        tok(2 * LANES),
        tok(D_ATTN),
    )
    return pl.pallas_call(
        _proj_kernel,
        grid=(nsb, ncb, N_SUB),
        in_specs=[_tile_specs(D_MODEL), _const_spec((1, D_MODEL)), _const_spec((D_MODEL, _C_END))],
        out_specs=(_CHUNK_TILE_SPEC, _tile_specs(D_SSM), _tile_specs(D_ATTN), _tile_specs(2 * LANES),
                   _tile_specs(2 * LANES), _tile_specs(D_ATTN)),
        out_shape=out_shapes,
        scratch_shapes=[
            pltpu.VMEM((TILE_ROWS, D_SSM), BF16),
            pltpu.VMEM((TILE_CHUNKS // 2, PAIR_ROWS, D_SSM), BF16),
            pltpu.VMEM((CHUNK, LANE_TILES, LANES, LANES), BF16),
        ],
        compiler_params=pltpu.CompilerParams(
            dimension_semantics=("arbitrary", "arbitrary", "arbitrary"), vmem_limit_bytes=VMEM_LIMIT),
        name="proj",
    )(x, norm_w, w_cat)


def _ssm_kernel(u_ref, toep_ref, bs_ref, cs_ref, are_ref, aim_ref, y_ref, s_scr, xf_scr, xb_scr,
                *, nsb, ncb):
    nch = ncb * TILE_CHUNKS
    tiles_per_mm = MM_ROWS // LANES
    seq_rows = nsb * SEQ_TILE

    def mm_blocks():
        for sb in range(nsb):
            for cq in range(ncb // tiles_per_mm):
                yield sb, cq * tiles_per_mm, pl.ds((sb * ncb + cq * tiles_per_mm) * LANES, MM_ROWS)

    def load_u(sb, cb0):
        return jnp.concatenate([u_ref[sb, cb0 + i] for i in range(tiles_per_mm)], axis=0)

    for sb, cb0, rows in mm_blocks():
        st = jnp.dot(load_u(sb, cb0), bs_ref[...], preferred_element_type=F32)
        s_scr[0, rows, :] = st[:, :LANES]
        s_scr[1, rows, :] = st[:, LANES:]

    def chunk_rows(sb, ch):
        return pl.ds(pl.multiple_of(sb * nch * SEQ_TILE + ch * SEQ_TILE, SEQ_TILE), SEQ_TILE)

    def load(ref, part, ch):
        return jnp.concatenate([ref[part, chunk_rows(sb, ch), :] for sb in range(nsb)], axis=0)

    def store(ref, part, ch, val):
        for sb in range(nsb):
            ref[part, chunk_rows(sb, ch), :] = val[sb * SEQ_TILE:(sb + 1) * SEQ_TILE, :]

    are = jnp.broadcast_to(are_ref[...], (seq_rows, LANES))
    aim = jnp.broadcast_to(aim_ref[...], (seq_rows, LANES))
    is_fwd = lax.broadcasted_iota(jnp.int32, (seq_rows, LANES), 1) < SSM_STATE

    def step(i, carry):
        xr, xi = carry
        j = nch - 1 - i
        store(xf_scr, 0, i, xr)
        store(xf_scr, 1, i, xi)
        store(xb_scr, 0, j, xr)
        store(xb_scr, 1, j, xi)
        s_re = jnp.where(is_fwd, load(s_scr, 0, i), load(s_scr, 0, j))
        s_im = jnp.where(is_fwd, load(s_scr, 1, i), load(s_scr, 1, j))
        return (are * xr - aim * xi + s_re, are * xi + aim * xr + s_im)

    zero = jnp.zeros((seq_rows, LANES), F32)
    lax.fori_loop(0, nch, step, (zero, zero), unroll=4)

    fwd_lane = lax.broadcasted_iota(jnp.int32, (MM_ROWS, LANES), 1) < SSM_STATE
    for sb, cb0, rows in mm_blocks():
        xin = jnp.concatenate(
            [jnp.where(fwd_lane, xf_scr[c, rows, :], xb_scr[c, rows, :]) for c in range(2)], axis=1).astype(BF16)
        y = (jnp.dot(load_u(sb, cb0), toep_ref[...], preferred_element_type=F32)
             + jnp.dot(xin, cs_ref[...], preferred_element_type=F32))
        for i in range(tiles_per_mm):
            y_ref[sb, cb0 + i] = y[i * LANES:(i + 1) * LANES, :].astype(BF16)


def _ssm_call(up, ops):
    toep, bs, cs, are, aim = ops
    _, nsb, ncb, _, _ = up.shape
    n_rows = nsb * ncb * LANES
    assert ncb % (MM_ROWS // LANES) == 0
    data_spec = pl.BlockSpec((None, nsb, ncb, LANES, CHUNK_W), lambda g: (g, 0, 0, 0, 0))
    grp = lambda *shape: pl.BlockSpec((None,) + shape, lambda g: (g, 0, 0))
    return pl.pallas_call(
        functools.partial(_ssm_kernel, nsb=nsb, ncb=ncb),
        grid=(N_GROUPS,),
        in_specs=[data_spec, grp(CHUNK_W, CHUNK_W), grp(CHUNK_W, STATE_W), grp(STATE_W, CHUNK_W),
                  grp(1, LANES), grp(1, LANES)],
        out_specs=data_spec,
        out_shape=jax.ShapeDtypeStruct(up.shape, BF16),
        scratch_shapes=[pltpu.VMEM((2, n_rows, LANES), F32)] * 3,
        compiler_params=pltpu.CompilerParams(
            dimension_semantics=("arbitrary",), vmem_limit_bytes=VMEM_LIMIT),
        name="ssm",
    )(up, toep, bs, cs, are, aim)


def _attn_kernel(sink_ref, q_ref, kd_ref, vd_ref, bias_ref, o_ref, ka_scr, kb_scr, wa_scr, wb_scr, *, seq_len):
    jb = pl.program_id(1)
    nb = seq_len // BLOCK

    @pl.when(jb == 0)
    def _():
        low = lax.broadcasted_iota(jnp.int32, (BLOCK, LANES), 1) < HEAD_DIM
        zeros = jnp.zeros((HEAD_DIM, BLOCK), BF16)
        ones = jnp.ones((HEAD_DIM, BLOCK), BF16)

        def per_block(r, carry):
            rows = pl.ds(pl.multiple_of(r * BLOCK, BLOCK), BLOCK)
            for kh in range(N_KV_HEADS):
                kd = kd_ref[rows, pl.ds(kh * LANES, LANES)]
                ka_scr[kh, rows, :] = jnp.where(low, kd, jnp.zeros_like(kd))
                kb_scr[kh, rows, :] = jnp.where(low, jnp.zeros_like(kd), kd)
                vt = vd_ref[rows, pl.ds(kh * LANES, LANES)].T
                wa_scr[kh, r] = jnp.concatenate([vt[:HEAD_DIM], zeros, ones, zeros], axis=0)
                wb_scr[kh, r] = jnp.concatenate([zeros, vt[HEAD_DIM:], zeros, ones], axis=0)
            return carry

        lax.fori_loop(0, nb, per_block, 0)

    col_tile0 = lax.broadcasted_iota(jnp.int32, (1, 2 * LANES), 1) < LANES
    row_slot0 = lax.broadcasted_iota(jnp.int32, (BLOCK, 2 * LANES), 0) < HEAD_DIM
    for sb in range(Q_ROWS // BLOCK):
        n = jb * (Q_ROWS // BLOCK) + sb
        kb0 = jnp.clip(n - 1, 0, nb - KEYS // BLOCK)
        ks = pl.multiple_of(kb0 * BLOCK, BLOCK)
        var = jnp.where(n == 0, 0, jnp.where(n == nb - 1, 2, 1))
        qrows = pl.ds(sb * BLOCK, BLOCK)
        for kh in range(N_KV_HEADS):
            kcat = jnp.concatenate([ka_scr[kh, pl.ds(ks, KEYS), :], kb_scr[kh, pl.ds(ks, KEYS), :]], axis=0)
            q2 = jnp.concatenate([q_ref[qrows, pl.ds((2 * kh + jj) * LANES, LANES)] for jj in range(2)], axis=0)
            t = lax.dot_general(kcat, q2, (((1,), (1,)), ((), ())), preferred_element_type=F32)
            t = t + bias_ref[var, kh]
            ps, sink_terms = [], []
            for hh in range(2):
                th = t[hh * KEYS:(hh + 1) * KEYS]
                sink = jnp.where(col_tile0, sink_ref[4 * kh + hh], sink_ref[4 * kh + 2 + hh])
                m = jnp.maximum(jnp.max(th, axis=0, keepdims=True), sink)
                ps.append(jnp.exp2(th - m).astype(BF16))
                sink_terms.append(jnp.exp2(sink - m))
            w = jnp.concatenate([wa_scr[kh, kb0 + i] for i in range(KEYS // BLOCK)]
                                + [wb_scr[kh, kb0 + i] for i in range(KEYS // BLOCK)], axis=1)
            o2 = jnp.dot(w, jnp.concatenate(ps, axis=0), preferred_element_type=F32)
            denom = o2[LANES:] + jnp.where(row_slot0, sink_terms[0], sink_terms[1])
            on = o2[:LANES] / denom
            for jj in range(2):
                o_ref[qrows, pl.ds((2 * kh + jj) * LANES, LANES)] = (
                    on[:, jj * LANES:(jj + 1) * LANES].T.astype(BF16))


def _attn_call(q, kd, vd, bias, sink, batch, seq_len):
    assert seq_len % Q_ROWS == 0 and seq_len >= KEYS
    nq = seq_len // Q_ROWS
    nb = seq_len // BLOCK
    return pl.pallas_call(
        functools.partial(_attn_kernel, seq_len=seq_len),
        grid_spec=pltpu.PrefetchScalarGridSpec(
            num_scalar_prefetch=1,
            grid=(batch, nq),
            in_specs=[
                pl.BlockSpec((Q_ROWS, D_ATTN), lambda b, j, s: (b * nq + j, 0)),
                pl.BlockSpec((seq_len, 2 * LANES), lambda b, j, s: (b, 0)),
                pl.BlockSpec((seq_len, 2 * LANES), lambda b, j, s: (b, 0)),
                pl.BlockSpec((3, N_KV_HEADS, 2 * KEYS, 2 * LANES), lambda b, j, s: (0, 0, 0, 0)),
            ],
            out_specs=pl.BlockSpec((Q_ROWS, D_ATTN), lambda b, j, s: (b * nq + j, 0)),
            scratch_shapes=[
                pltpu.VMEM((N_KV_HEADS, seq_len, LANES), BF16),
                pltpu.VMEM((N_KV_HEADS, seq_len, LANES), BF16),
                pltpu.VMEM((N_KV_HEADS, nb, 2 * LANES, BLOCK), BF16),
                pltpu.VMEM((N_KV_HEADS, nb, 2 * LANES, BLOCK), BF16),
            ],
        ),
        out_shape=jax.ShapeDtypeStruct((batch * seq_len, D_ATTN), BF16),
        compiler_params=pltpu.CompilerParams(
            dimension_semantics=("arbitrary", "arbitrary"), vmem_limit_bytes=VMEM_LIMIT),
        name="attn",
    )(sink, q, kd, vd, bias)


def _gelu_tanh(x):
    c = math.sqrt(2.0 / math.pi)
    return x * (0.5 * (1.0 + jnp.tanh(c * (x + 0.044715 * (x * x * x)))))


def _silu(x):
    return x * jax.nn.sigmoid(x)


def _post_kernel(x_ref, yp_ref, zs_ref, ao_ref, za_ref, wglu_ref, bglu_ref, snw_ref, anw_ref,
                 wout_ref, fnw_ref, o_ref, y_scr, b_scr, t_scr):
    s = pl.program_id(2)

    @pl.when(s == 0)
    def _():
        for g in range(N_GROUPS):
            j, g3 = divmod(g, GROUPS_PER_LANE_TILE)
            for th in range(CHUNK_W // LANES):
                tt = yp_ref[g, :, pl.ds(th * LANES, LANES)].T
                for t3 in range(8):
                    t_scr[th * 8 + t3, j, pl.ds(g3 * SSM_GROUP, SSM_GROUP), :] = (
                        tt[t3 * SSM_GROUP:(t3 + 1) * SSM_GROUP, :])
        for t in range(CHUNK):
            for j in range(LANE_TILES):
                b_scr[t, :, pl.ds(j * LANES, LANES)] = t_scr[t, j].T
        perm = _pair_perm(t_major_out=False)
        for cp in range(TILE_CHUNKS // 2):
            a = jnp.concatenate(
                [b_scr[t, pl.ds(cp * 2 * SEQ_TILE, 2 * SEQ_TILE), :] for t in range(CHUNK)], axis=0)
            ys = jnp.dot(perm, a, preferred_element_type=F32)
            for c2 in range(2):
                for sq in range(SEQ_TILE):
                    r0 = (c2 * SEQ_TILE + sq) * CHUNK
                    y_scr[pl.ds(sq * TOK_TILE + (2 * cp + c2) * CHUNK, CHUNK), :] = ys[r0:r0 + CHUNK, :]

    def get(ref):
        return ref[...].reshape(SUB, ref.shape[-1]).astype(F32)

    y = y_scr[pl.ds(pl.multiple_of(s * SUB, SUB), SUB), :]
    g = _gelu_tanh(y)
    gate = jax.nn.sigmoid(
        jnp.dot(g.astype(BF16), wglu_ref[...], preferred_element_type=F32) + bglu_ref[...])
    n_ssm = _rms(g * gate, snw_ref[...]) * _silu(get(zs_ref))
    n_attn = _rms(get(ao_ref), anw_ref[...]) * _silu(get(za_ref))
    mixed = jnp.concatenate([n_ssm, n_attn], axis=1).astype(BF16)
    res = get(x_ref) + jnp.dot(mixed, wout_ref[...], preferred_element_type=F32)
    o_ref[...] = _rms(res, fnw_ref[...]).reshape(o_ref.shape)


def _post_call(x, yp, zs, ao, za, w_glu, b_glu, ssm_norm_w, attn_norm_w, w_out, final_norm_w):
    batch, seq_len, _ = x.shape
    nsb, ncb = batch // SEQ_TILE, seq_len // TOK_TILE
    return pl.pallas_call(
        _post_kernel,
        grid=(nsb, ncb, N_SUB),
        in_specs=[
            _tile_specs(D_MODEL), _CHUNK_TILE_SPEC, _tile_specs(D_SSM), _tile_specs(D_ATTN), _tile_specs(D_ATTN),
            _const_spec((D_SSM, D_SSM)), _const_spec((1, D_SSM)), _const_spec((1, D_SSM)),
            _const_spec((1, D_ATTN)), _const_spec((D_MODEL, D_MODEL)), _const_spec((1, D_MODEL)),
        ],
        out_specs=_tile_specs(D_MODEL),
        out_shape=jax.ShapeDtypeStruct((batch, seq_len, D_MODEL), F32),
        scratch_shapes=[
            pltpu.VMEM((TILE_ROWS, D_SSM), F32),
            pltpu.VMEM((CHUNK, LANES, D_SSM), BF16),
            pltpu.VMEM((CHUNK, LANE_TILES, LANES, LANES), BF16),
        ],
        compiler_params=pltpu.CompilerParams(
            dimension_semantics=("arbitrary", "arbitrary", "arbitrary"), vmem_limit_bytes=VMEM_LIMIT),
        name="post",
    )(x, yp, zs, ao, za, w_glu, b_glu, ssm_norm_w, attn_norm_w, w_out, final_norm_w)


def _cmul(ar, ai, br, bi):
    return ar * br - ai * bi, ar * bi + ai * br


def _split_bf16(x):
    hi = x.astype(BF16)
    return hi, (x - hi.astype(F32)).astype(BF16)


def _dot_split(a, b):
    a_hi, a_lo = _split_bf16(a)
    b_hi, b_lo = _split_bf16(b)
    dot = functools.partial(jnp.dot, preferred_element_type=F32)
    return dot(a_hi, b_hi) + (dot(a_hi, b_lo) + dot(a_lo, b_hi))


def _ops_kernel(row_ref, col_ref, bt_ref, ct_ref, dsk_ref, toep_ref, bs_ref, cs_ref, a_ref):
    def discretise(lr, li, ls):
        dt = jnp.exp(ls)
        mag = jnp.exp(lr * dt)
        return mag * jnp.cos(li * dt), mag * jnp.sin(li * dt)

    lr, li = row_ref[0], row_ref[1]
    lb_re, lb_im = discretise(lr, li, row_ref[2])
    den = lr * lr + li * li
    nr = lb_re - 1.0
    coef_re = (nr * lr + lb_im * li) / den
    coef_im = (lb_im * lr - nr * li) / den
    bb_re, bb_im = _cmul(coef_re, coef_im, bt_ref[0], bt_ref[1])
    bb_re = jnp.concatenate([bb_re] * CHUNK, axis=0)
    bb_im = jnp.concatenate([bb_im] * CHUNK, axis=0)

    row_t = lax.broadcasted_iota(jnp.int32, (CHUNK_W, LANES), 0) // SSM_GROUP
    fwd_lane = lax.broadcasted_iota(jnp.int32, (CHUNK_W, LANES), 1) < SSM_STATE
    col_t = lax.broadcasted_iota(jnp.int32, (LANES, CHUNK_W), 1) // SSM_GROUP
    fwd_row = lax.broadcasted_iota(jnp.int32, (LANES, CHUNK_W), 0) < SSM_STATE
    ct_re, ct_im = ct_ref[0], ct_ref[1]
    w_re = jnp.zeros((CHUNK_W, LANES), F32)
    w_im = jnp.zeros((CHUNK_W, LANES), F32)
    toep = jnp.zeros((CHUNK_W, CHUNK_W), F32)
    for i in range(CHUNK):
        t_now = jnp.where(fwd_lane, i, CHUNK - 1 - i)
        w_re, w_im = _cmul(w_re, w_im, lb_re, lb_im)
        w_re = w_re + jnp.where(row_t == t_now, bb_re, 0.0)
        w_im = w_im + jnp.where(row_t == t_now, bb_im, 0.0)
        out_now = col_t == jnp.where(fwd_row, i, CHUNK - 1 - i)
        rhs = jnp.concatenate([jnp.where(out_now, ct_re, 0.0), jnp.where(out_now, -ct_im, 0.0)], axis=0)
        toep = toep + _dot_split(jnp.concatenate([w_re, w_im], axis=1), rhs)
    r = lax.broadcasted_iota(jnp.int32, (CHUNK_W, CHUNK_W), 0)
    c = lax.broadcasted_iota(jnp.int32, (CHUNK_W, CHUNK_W), 1)
    toep_ref[...] = (toep + jnp.where(r == c, dsk_ref[...], 0.0)).astype(BF16)
    bs_ref[...] = jnp.concatenate([w_re, w_im], axis=1).astype(BF16)

    pr, pi = discretise(col_ref[0], col_ref[1], col_ref[2])
    k = jnp.where(fwd_row, col_t + 1, CHUNK - col_t)
    acc_re = jnp.ones((LANES, CHUNK_W), F32)
    acc_im = jnp.zeros((LANES, CHUNK_W), F32)
    for bit in range(CHUNK.bit_length()):
        nre, nim = _cmul(acc_re, acc_im, pr, pi)
        take = (k & (1 << bit)) != 0
        acc_re = jnp.where(take, nre, acc_re)
        acc_im = jnp.where(take, nim, acc_im)
        pr, pi = _cmul(pr, pi, pr, pi)
    cp_re, cp_im = _cmul(ct_re, ct_im, acc_re, acc_im)
    cs_ref[...] = jnp.concatenate([cp_re, -cp_im], axis=0).astype(BF16)

    a_re, a_im = lb_re, lb_im
    for _ in range(CHUNK.bit_length() - 1):
        a_re, a_im = _cmul(a_re, a_im, a_re, a_im)
    a_ref[0] = a_re
    a_ref[1] = a_im


def _ssm_operators(lam_re, lam_im, log_step, b_re, b_im, c_re, c_im, d_skip):
    g_first = lambda a: jnp.transpose(a.astype(F32), (1, 0, 2)).reshape(N_GROUPS, 2 * SSM_STATE)
    params = jnp.stack([g_first(lam_re), g_first(lam_im),
                        jnp.repeat(log_step.astype(F32).T, SSM_STATE, axis=1)], axis=1)
    b_t = lambda b: jnp.transpose(b.astype(F32), (1, 3, 0, 2)).reshape(N_GROUPS, SSM_GROUP, 2 * SSM_STATE)
    c_t = lambda c: jnp.tile(
        jnp.transpose(c.astype(F32), (1, 0, 3, 2)).reshape(N_GROUPS, 2 * SSM_STATE, SSM_GROUP), (1, 1, CHUNK))
    bt = jnp.stack([b_t(b_re), b_t(b_im)], axis=1)
    ct = jnp.stack([c_t(c_re), c_t(c_im)], axis=1)
    dsk = jnp.tile(d_skip.astype(F32), (1, CHUNK))[:, None, :]
    grp = lambda *shape: pl.BlockSpec((None,) + shape, lambda g: (g,) + (0,) * len(shape))
    mat = jax.ShapeDtypeStruct((N_GROUPS, CHUNK_W, CHUNK_W), BF16)
    toep, bs, cs, a = pl.pallas_call(
        _ops_kernel,
        grid=(N_GROUPS,),
        in_specs=[grp(3, 1, LANES), grp(3, LANES, 1), grp(2, SSM_GROUP, LANES), grp(2, LANES, CHUNK_W),
                  grp(1, CHUNK_W)],
        out_specs=(grp(CHUNK_W, CHUNK_W), grp(CHUNK_W, STATE_W), grp(STATE_W, CHUNK_W), grp(2, 1, LANES)),
        out_shape=(mat, mat, mat, jax.ShapeDtypeStruct((N_GROUPS, 2, 1, LANES), F32)),
        compiler_params=pltpu.CompilerParams(dimension_semantics=("arbitrary",)),
        name="s5_ops",
    )(params[:, :, None, :], params[:, :, :, None], bt, ct, dsk)
    return toep, bs, cs, a[:, 0], a[:, 1]


def _t5_buckets_np(rel):
    half = N_BUCKETS // 2
    max_exact = half // 2
    ret = np.where(rel > 0, half, 0)
    n = np.abs(rel)
    nf = np.maximum(n, 1).astype(np.float64)
    large = max_exact + (np.log(nf / max_exact) / math.log(MAX_DISTANCE / max_exact)
                         * (half - max_exact)).astype(np.int32)
    large = np.minimum(large, half - 1)
    return ret + np.where(n < max_exact, n, large)


def _attn_tables(rel_bias):
    offsets = (0, -BLOCK, -2 * BLOCK)
    period = 4 * BLOCK
    dist = np.arange(period)
    dist = np.where(dist < KEYS, dist, dist - period)
    rel = np.stack([dist + off for off in offsets])
    per_dist = rel_bias.astype(F32)[_t5_buckets_np(rel)] * LOG2_E
    per_dist = jnp.where((np.abs(rel) <= WINDOW)[:, :, None], per_dist, NEG_INF)
    per_dist = jnp.transpose(per_dist, (0, 2, 1))
    neg = jnp.concatenate([per_dist[:, :, :1], per_dist[:, :, :0:-1]], axis=-1)
    skew = jnp.tile(neg, (1, 1, KEYS))[:, :, :KEYS * (period - 1)]
    bias = skew.reshape(3, N_HEADS, KEYS, period - 1)[:, :, :, :BLOCK]
    bias = bias.reshape(3, N_KV_HEADS, 2, 2, KEYS, BLOCK)
    bias = jnp.concatenate([bias[:, :, 0], bias[:, :, 1]], axis=-1)
    return bias.reshape(3, N_KV_HEADS, 2 * KEYS, 2 * LANES)


def _pack_w_in(w):
    u, zs, q, k, v, za = jnp.split(w, (512, 1024, 1536, 1664, 1792), axis=1)
    dup = lambda t: jnp.concatenate([t[:, :HEAD_DIM], t[:, :HEAD_DIM], t[:, HEAD_DIM:], t[:, HEAD_DIM:]], axis=1)
    return jnp.concatenate([u, zs, q, dup(k), dup(v), za], axis=1).astype(BF16)


def kernel(x_prompt, x_sample, norm_w, w_in, lam_re, lam_im, log_step, b_re, b_im, c_re, c_im, d_skip,
           w_glu, b_glu, ssm_norm_w, sink, attn_norm_w, w_out, rel_bias, final_norm_w):
    assert norm_w.shape[0] == 1, "single-layer encoder"
    ops = _ssm_operators(lam_re[0], lam_im[0], log_step[0], b_re[0], b_im[0], c_re[0], c_im[0], d_skip[0])
    bias = _attn_tables(rel_bias)
    w_cat = _pack_w_in(w_in[0])
    w_glu_b = w_glu[0].astype(BF16)
    w_out_b = w_out[0].astype(BF16)
    sink_f = sink[0].astype(F32) * LOG2_E

    outs = []
    for x in (x_prompt, x_sample):
        batch, seq_len, _ = x.shape
        assert batch % SEQ_TILE == 0 and seq_len % TOK_TILE == 0
        n_tok = batch * seq_len
        up, zs, q, kd, vd, za = _proj_call(x, norm_w, w_cat)
        yp = _ssm_call(up, ops)
        ao = _attn_call(q.reshape(n_tok, D_ATTN), kd.reshape(n_tok, 2 * LANES), vd.reshape(n_tok, 2 * LANES),
                        bias, sink_f, batch, seq_len)
        outs.append(_post_call(x, yp, zs, ao.reshape(batch, seq_len, D_ATTN), za, w_glu_b, b_glu, ssm_norm_w,
                               attn_norm_w, w_out_b, final_norm_w.reshape(1, D_MODEL)))
    return tuple(outs)
```

```python
import functools
import math

import numpy as np
import jax
import jax.numpy as jnp
from jax import lax
from jax.experimental import pallas as pl
from jax.experimental.pallas import tpu as pltpu

F32 = jnp.float32
BF16 = jnp.bfloat16

D_MODEL = 1024
D_SSM = 512
D_ATTN = 512
SSM_GROUP = 16
N_GROUPS = D_SSM // SSM_GROUP
SSM_STATE = 64
HEAD_DIM = 64
N_HEADS = D_ATTN // HEAD_DIM
N_KV_HEADS = 2
WINDOW = 128
BLOCK = 128
N_BUCKETS = 32
MAX_DISTANCE = 128
RMS_EPS = 1e-6
NEG_INF = -1e30
LOG2_E = math.log2(math.e)

LANES = 128
SUBLANES = 8
CHUNK = 16
CHUNK_W = CHUNK * SSM_GROUP
STATE_W = 4 * SSM_STATE
SEQ_TILE = SUBLANES
TILE_CHUNKS = LANES // SEQ_TILE
TOK_TILE = TILE_CHUNKS * CHUNK
TILE_ROWS = SEQ_TILE * TOK_TILE
SUB_SEQS = 4
SUB = SUB_SEQS * TOK_TILE
N_SUB = SEQ_TILE // SUB_SEQS
LANE_TILES = D_SSM // LANES
GROUPS_PER_LANE_TILE = LANES // SSM_GROUP
KEYS = 3 * BLOCK
Q_ROWS = 1024
SUM_ROWS = 16
MM_ROWS = 512
VMEM_LIMIT = 56 * 1024 * 1024

_C_U, _C_ZS, _C_Q, _C_K, _C_V, _C_ZA, _C_END = 0, 512, 1024, 1536, 1792, 2048, 2560


def _rms(x, w):
    return (x * lax.rsqrt(jnp.mean(x * x, axis=-1, keepdims=True) + RMS_EPS)) * w


def _tile_specs(width):
    return pl.BlockSpec((SUB_SEQS, TOK_TILE, width), lambda sb, cb, s: (sb * N_SUB + s, cb, 0))


_CHUNK_TILE_SPEC = pl.BlockSpec((N_GROUPS, None, None, LANES, CHUNK_W), lambda sb, cb, s: (0, sb, cb, 0, 0))


def _const_spec(shape):
    return pl.BlockSpec(shape, lambda sb, cb, s: (0,) * len(shape))


PAIR_ROWS = 2 * SEQ_TILE * CHUNK


def _pair_perm(t_major_out):
    out_row = lax.broadcasted_iota(jnp.int32, (PAIR_ROWS, PAIR_ROWS), 0)
    in_row = lax.broadcasted_iota(jnp.int32, (PAIR_ROWS, PAIR_ROWS), 1)
    tcs, cst = (out_row, in_row) if t_major_out else (in_row, out_row)
    hit = (tcs // (2 * SEQ_TILE) == cst % CHUNK) & (tcs % (2 * SEQ_TILE) == cst // CHUNK)
    return jnp.where(hit, 1.0, 0.0).astype(BF16)


def _proj_kernel(x_ref, nw_ref, w_ref, up_ref, zs_ref, q_ref, kd_ref, vd_ref, za_ref, u_scr, b_scr, t_scr):
    s = pl.program_id(2)
    hb = _rms(x_ref[...].reshape(SUB, D_MODEL), nw_ref[...]).astype(BF16)

    def mm(lo, hi):
        return jnp.dot(hb, w_ref[:, lo:hi], preferred_element_type=F32)

    def put(ref, val):
        ref[...] = val.astype(BF16).reshape(ref.shape)

    u_scr[pl.ds(pl.multiple_of(s * SUB, SUB), SUB), :] = mm(_C_U, _C_ZS).astype(BF16)
    put(zs_ref, mm(_C_ZS, _C_Q))
    put(q_ref, mm(_C_Q, _C_K) * (HEAD_DIM ** -0.5 * LOG2_E))
    put(kd_ref, mm(_C_K, _C_V))
    put(vd_ref, mm(_C_V, _C_ZA))
    put(za_ref, mm(_C_ZA, _C_END))

    @pl.when(s == N_SUB - 1)
    def _():
        perm = _pair_perm(t_major_out=True)
        for cp in range(TILE_CHUNKS // 2):
            a = jnp.concatenate(
                [u_scr[pl.ds(sq * TOK_TILE + (2 * cp + c2) * CHUNK, CHUNK), :]
                 for c2 in range(2) for sq in range(SEQ_TILE)], axis=0)
            b_scr[cp] = jnp.dot(perm, a, preferred_element_type=F32).astype(BF16)
        for t in range(CHUNK):
            for j in range(LANE_TILES):
                blk = jnp.concatenate(
                    [b_scr[cp, pl.ds(t * 2 * SEQ_TILE, 2 * SEQ_TILE), pl.ds(j * LANES, LANES)]
                     for cp in range(TILE_CHUNKS // 2)], axis=0)
                t_scr[t, j] = blk.T
        for g in range(N_GROUPS):
            j, g3 = divmod(g, GROUPS_PER_LANE_TILE)
            for th in range(CHUNK_W // LANES):
                rows = jnp.concatenate(
                    [t_scr[th * 8 + t3, j, pl.ds(g3 * SSM_GROUP, SSM_GROUP), :] for t3 in range(8)], axis=0)
                up_ref[g, :, pl.ds(th * LANES, LANES)] = rows.T


def _proj_call(x, norm_w, w_cat):
    batch, seq_len, _ = x.shape
    nsb, ncb = batch // SEQ_TILE, seq_len // TOK_TILE
    tok = lambda width: jax.ShapeDtypeStruct((batch, seq_len, width), BF16)
    out_shapes = (
        jax.ShapeDtypeStruct((N_GROUPS, nsb, ncb, LANES, CHUNK_W), BF16),
        tok(D_SSM),
        tok(D_ATTN),
        tok(2 * LANES),
        tok(2 * LANES),
        tok(D_ATTN),
    )
    return pl.pallas_call(
        _proj_kernel,
        grid=(nsb, ncb, N_SUB),
        in_specs=[_tile_specs(D_MODEL), _const_spec((1, D_MODEL)), _const_spec((D_MODEL, _C_END))],
        out_specs=(_CHUNK_TILE_SPEC, _tile_specs(D_SSM), _tile_specs(D_ATTN), _tile_specs(2 * LANES),
                   _tile_specs(2 * LANES), _tile_specs(D_ATTN)),
        out_shape=out_shapes,
        scratch_shapes=[
            pltpu.VMEM((TILE_ROWS, D_SSM), BF16),
            pltpu.VMEM((TILE_CHUNKS // 2, PAIR_ROWS, D_SSM), BF16),
            pltpu.VMEM((CHUNK, LANE_TILES, LANES, LANES), BF16),
        ],
        compiler_params=pltpu.CompilerParams(
            dimension_semantics=("arbitrary", "arbitrary", "arbitrary"), vmem_limit_bytes=VMEM_LIMIT),
        name="proj",
    )(x, norm_w, w_cat)


def _ssm_kernel(u_ref, toep_ref, bs_ref, cs_ref, are_ref, aim_ref, y_ref, s_scr, xf_scr, xb_scr,
                *, nsb, ncb):
    nch = ncb * TILE_CHUNKS
    tiles_per_mm = MM_ROWS // LANES
    seq_rows = nsb * SEQ_TILE

    def mm_blocks():
        for sb in range(nsb):
            for cq in range(ncb // tiles_per_mm):
                yield sb, cq * tiles_per_mm, pl.ds((sb * ncb + cq * tiles_per_mm) * LANES, MM_ROWS)

    def load_u(sb, cb0):
        return jnp.concatenate([u_ref[sb, cb0 + i] for i in range(tiles_per_mm)], axis=0)

    for sb, cb0, rows in mm_blocks():
        st = jnp.dot(load_u(sb, cb0), bs_ref[...], preferred_element_type=F32)
        s_scr[0, rows, :] = st[:, :LANES]
        s_scr[1, rows, :] = st[:, LANES:]

    def chunk_rows(sb, ch):
        return pl.ds(pl.multiple_of(sb * nch * SEQ_TILE + ch * SEQ_TILE, SEQ_TILE), SEQ_TILE)

    def load(ref, part, ch):
        return jnp.concatenate([ref[part, chunk_rows(sb, ch), :] for sb in range(nsb)], axis=0)

    def store(ref, part, ch, val):
        for sb in range(nsb):
            ref[part, chunk_rows(sb, ch), :] = val[sb * SEQ_TILE:(sb + 1) * SEQ_TILE, :]

    are = jnp.broadcast_to(are_ref[...], (seq_rows, LANES))
    aim = jnp.broadcast_to(aim_ref[...], (seq_rows, LANES))
    is_fwd = lax.broadcasted_iota(jnp.int32, (seq_rows, LANES), 1) < SSM_STATE

    def step(i, carry):
        xr, xi = carry
        j = nch - 1 - i
        store(xf_scr, 0, i, xr)
        store(xf_scr, 1, i, xi)
        store(xb_scr, 0, j, xr)
        store(xb_scr, 1, j, xi)
        s_re = jnp.where(is_fwd, load(s_scr, 0, i), load(s_scr, 0, j))
        s_im = jnp.where(is_fwd, load(s_scr, 1, i), load(s_scr, 1, j))
        return (are * xr - aim * xi + s_re, are * xi + aim * xr + s_im)

    zero = jnp.zeros((seq_rows, LANES), F32)
    lax.fori_loop(0, nch, step, (zero, zero), unroll=4)

    fwd_lane = lax.broadcasted_iota(jnp.int32, (MM_ROWS, LANES), 1) < SSM_STATE
    for sb, cb0, rows in mm_blocks():
        xin = jnp.concatenate(
            [jnp.where(fwd_lane, xf_scr[c, rows, :], xb_scr[c, rows, :]) for c in range(2)], axis=1).astype(BF16)
        y = (jnp.dot(load_u(sb, cb0), toep_ref[...], preferred_element_type=F32)
             + jnp.dot(xin, cs_ref[...], preferred_element_type=F32))
        for i in range(tiles_per_mm):
            y_ref[sb, cb0 + i] = y[i * LANES:(i + 1) * LANES, :].astype(BF16)


def _ssm_call(up, ops):
    toep, bs, cs, are, aim = ops
    _, nsb, ncb, _, _ = up.shape
    n_rows = nsb * ncb * LANES
    assert ncb % (MM_ROWS // LANES) == 0
    data_spec = pl.BlockSpec((None, nsb, ncb, LANES, CHUNK_W), lambda g: (g, 0, 0, 0, 0))
    grp = lambda *shape: pl.BlockSpec((None,) + shape, lambda g: (g, 0, 0))
    return pl.pallas_call(
        functools.partial(_ssm_kernel, nsb=nsb, ncb=ncb),
        grid=(N_GROUPS,),
        in_specs=[data_spec, grp(CHUNK_W, CHUNK_W), grp(CHUNK_W, STATE_W), grp(STATE_W, CHUNK_W),
                  grp(1, LANES), grp(1, LANES)],
        out_specs=data_spec,
        out_shape=jax.ShapeDtypeStruct(up.shape, BF16),
        scratch_shapes=[pltpu.VMEM((2, n_rows, LANES), F32)] * 3,
        compiler_params=pltpu.CompilerParams(
            dimension_semantics=("arbitrary",), vmem_limit_bytes=VMEM_LIMIT),
        name="ssm",
    )(up, toep, bs, cs, are, aim)


def _attn_kernel(sink_ref, q_ref, kd_ref, vd_ref, bias_ref, o_ref, ka_scr, kb_scr, wa_scr, wb_scr, *, seq_len):
    jb = pl.program_id(1)
    nb = seq_len // BLOCK

    @pl.when(jb == 0)
    def _():
        low = lax.broadcasted_iota(jnp.int32, (BLOCK, LANES), 1) < HEAD_DIM
        zeros = jnp.zeros((HEAD_DIM, BLOCK), BF16)
        ind_a = jnp.where(lax.broadcasted_iota(jnp.int32, (SUM_ROWS, BLOCK), 0) < SUM_ROWS // 2, 1.0, 0.0)
        ind_b = (1.0 - ind_a).astype(BF16)
        ind_a = ind_a.astype(BF16)

        def per_block(r, carry):
            rows = pl.ds(pl.multiple_of(r * BLOCK, BLOCK), BLOCK)
            for kh in range(N_KV_HEADS):
                kd = kd_ref[rows, pl.ds(kh * LANES, LANES)]
                ka_scr[kh, rows, :] = jnp.where(low, kd, jnp.zeros_like(kd))
                kb_scr[kh, rows, :] = jnp.where(low, jnp.zeros_like(kd), kd)
                vt = vd_ref[rows, pl.ds(kh * LANES, LANES)].T
                wa_scr[kh, r] = jnp.concatenate([vt[:HEAD_DIM], zeros, ind_a], axis=0)
                wb_scr[kh, r] = jnp.concatenate([zeros, vt[HEAD_DIM:], ind_b], axis=0)
            return carry

        lax.fori_loop(0, nb, per_block, 0)

    col_tile0 = lax.broadcasted_iota(jnp.int32, (1, 2 * LANES), 1) < LANES
    row_slot0 = lax.broadcasted_iota(jnp.int32, (BLOCK, 2 * LANES), 0) < HEAD_DIM
    for sb in range(Q_ROWS // BLOCK):
        n = jb * (Q_ROWS // BLOCK) + sb
        kb0 = jnp.clip(n - 1, 0, nb - KEYS // BLOCK)
        ks = pl.multiple_of(kb0 * BLOCK, BLOCK)
        var = jnp.where(n == 0, 0, jnp.where(n == nb - 1, 2, 1))
        qrows = pl.ds(sb * BLOCK, BLOCK)
        for kh in range(N_KV_HEADS):
            kcat = jnp.concatenate([ka_scr[kh, pl.ds(ks, KEYS), :], kb_scr[kh, pl.ds(ks, KEYS), :]], axis=0)
            q2 = jnp.concatenate([q_ref[qrows, pl.ds((2 * kh + jj) * LANES, LANES)] for jj in range(2)], axis=0)
            t = lax.dot_general(kcat, q2, (((1,), (1,)), ((), ())), preferred_element_type=F32)
            t = t + bias_ref[var, kh]
            ps, sink_terms = [], []
            for hh in range(2):
                th = t[hh * KEYS:(hh + 1) * KEYS]
                sink = jnp.where(col_tile0, sink_ref[4 * kh + hh], sink_ref[4 * kh + 2 + hh])
                m = jnp.maximum(jnp.max(th, axis=0, keepdims=True), sink)
                ps.append(jnp.exp2(th - m).astype(BF16))
                sink_terms.append(jnp.exp2(sink - m))
            w = jnp.concatenate([wa_scr[kh, kb0 + i] for i in range(KEYS // BLOCK)]
                                + [wb_scr[kh, kb0 + i] for i in range(KEYS // BLOCK)], axis=1)
            o2 = jnp.dot(w, jnp.concatenate(ps, axis=0), preferred_element_type=F32)
            sums = [o2[LANES + hh * (SUM_ROWS // 2):LANES + hh * (SUM_ROWS // 2) + 1] for hh in range(2)]
            denom = jnp.where(row_slot0, sums[0] + sink_terms[0], sums[1] + sink_terms[1])
            on = o2[:LANES] / denom
            for jj in range(2):
                o_ref[qrows, pl.ds((2 * kh + jj) * LANES, LANES)] = (
                    on[:, jj * LANES:(jj + 1) * LANES].T.astype(BF16))


def _attn_call(q, kd, vd, bias, sink, batch, seq_len):
    assert seq_len % Q_ROWS == 0 and seq_len >= KEYS
    nq = seq_len // Q_ROWS
    nb = seq_len // BLOCK
    return pl.pallas_call(
        functools.partial(_attn_kernel, seq_len=seq_len),
        grid_spec=pltpu.PrefetchScalarGridSpec(
            num_scalar_prefetch=1,
            grid=(batch, nq),
            in_specs=[
                pl.BlockSpec((Q_ROWS, D_ATTN), lambda b, j, s: (b * nq + j, 0)),
                pl.BlockSpec((seq_len, 2 * LANES), lambda b, j, s: (b, 0)),
                pl.BlockSpec((seq_len, 2 * LANES), lambda b, j, s: (b, 0)),
                pl.BlockSpec((3, N_KV_HEADS, 2 * KEYS, 2 * LANES), lambda b, j, s: (0, 0, 0, 0)),
            ],
            out_specs=pl.BlockSpec((Q_ROWS, D_ATTN), lambda b, j, s: (b * nq + j, 0)),
            scratch_shapes=[
                pltpu.VMEM((N_KV_HEADS, seq_len, LANES), BF16),
                pltpu.VMEM((N_KV_HEADS, seq_len, LANES), BF16),
                pltpu.VMEM((N_KV_HEADS, nb, LANES + SUM_ROWS, BLOCK), BF16),
                pltpu.VMEM((N_KV_HEADS, nb, LANES + SUM_ROWS, BLOCK), BF16),
            ],
        ),
        out_shape=jax.ShapeDtypeStruct((batch * seq_len, D_ATTN), BF16),
        compiler_params=pltpu.CompilerParams(
            dimension_semantics=("arbitrary", "arbitrary"), vmem_limit_bytes=VMEM_LIMIT),
        name="attn",
    )(sink, q, kd, vd, bias)


def _gelu_tanh(x):
    c = math.sqrt(2.0 / math.pi)
    return x * (0.5 * (1.0 + jnp.tanh(c * (x + 0.044715 * (x * x * x)))))


def _silu(x):
    return x * jax.nn.sigmoid(x)


def _post_kernel(x_ref, yp_ref, zs_ref, ao_ref, za_ref, wglu_ref, bglu_ref, snw_ref, anw_ref,
                 wout_ref, fnw_ref, o_ref, y_scr, b_scr, t_scr):
    s = pl.program_id(2)

    @pl.when(s == 0)
    def _():
        for g in range(N_GROUPS):
            j, g3 = divmod(g, GROUPS_PER_LANE_TILE)
            for th in range(CHUNK_W // LANES):
                tt = yp_ref[g, :, pl.ds(th * LANES, LANES)].T
                for t3 in range(8):
                    t_scr[th * 8 + t3, j, pl.ds(g3 * SSM_GROUP, SSM_GROUP), :] = (
                        tt[t3 * SSM_GROUP:(t3 + 1) * SSM_GROUP, :])
        for t in range(CHUNK):
            for j in range(LANE_TILES):
                b_scr[t, :, pl.ds(j * LANES, LANES)] = t_scr[t, j].T
        perm = _pair_perm(t_major_out=False)
        for cp in range(TILE_CHUNKS // 2):
            a = jnp.concatenate(
                [b_scr[t, pl.ds(cp * 2 * SEQ_TILE, 2 * SEQ_TILE), :] for t in range(CHUNK)], axis=0)
            ys = jnp.dot(perm, a, preferred_element_type=F32)
            for c2 in range(2):
                for sq in range(SEQ_TILE):
                    r0 = (c2 * SEQ_TILE + sq) * CHUNK
                    y_scr[pl.ds(sq * TOK_TILE + (2 * cp + c2) * CHUNK, CHUNK), :] = ys[r0:r0 + CHUNK, :]

    def get(ref):
        return ref[...].reshape(SUB, ref.shape[-1]).astype(F32)

    y = y_scr[pl.ds(pl.multiple_of(s * SUB, SUB), SUB), :]
    g = _gelu_tanh(y)
    gate = jax.nn.sigmoid(
        jnp.dot(g.astype(BF16), wglu_ref[...], preferred_element_type=F32) + bglu_ref[...])
    n_ssm = _rms(g * gate, snw_ref[...]) * _silu(get(zs_ref))
    n_attn = _rms(get(ao_ref), anw_ref[...]) * _silu(get(za_ref))
    mixed = jnp.concatenate([n_ssm, n_attn], axis=1).astype(BF16)
    res = get(x_ref) + jnp.dot(mixed, wout_ref[...], preferred_element_type=F32)
    o_ref[...] = _rms(res, fnw_ref[...]).reshape(o_ref.shape)


def _post_call(x, yp, zs, ao, za, w_glu, b_glu, ssm_norm_w, attn_norm_w, w_out, final_norm_w):
    batch, seq_len, _ = x.shape
    nsb, ncb = batch // SEQ_TILE, seq_len // TOK_TILE
    return pl.pallas_call(
        _post_kernel,
        grid=(nsb, ncb, N_SUB),
        in_specs=[
            _tile_specs(D_MODEL), _CHUNK_TILE_SPEC, _tile_specs(D_SSM), _tile_specs(D_ATTN), _tile_specs(D_ATTN),
            _const_spec((D_SSM, D_SSM)), _const_spec((1, D_SSM)), _const_spec((1, D_SSM)),
            _const_spec((1, D_ATTN)), _const_spec((D_MODEL, D_MODEL)), _const_spec((1, D_MODEL)),
        ],
        out_specs=_tile_specs(D_MODEL),
        out_shape=jax.ShapeDtypeStruct((batch, seq_len, D_MODEL), F32),
        scratch_shapes=[
            pltpu.VMEM((TILE_ROWS, D_SSM), F32),
            pltpu.VMEM((CHUNK, LANES, D_SSM), BF16),
            pltpu.VMEM((CHUNK, LANE_TILES, LANES, LANES), BF16),
        ],
        compiler_params=pltpu.CompilerParams(
            dimension_semantics=("arbitrary", "arbitrary", "arbitrary"), vmem_limit_bytes=VMEM_LIMIT),
        name="post",
    )(x, yp, zs, ao, za, w_glu, b_glu, ssm_norm_w, attn_norm_w, w_out, final_norm_w)


def _cmul(ar, ai, br, bi):
    return ar * br - ai * bi, ar * bi + ai * br


def _split_bf16(x):
    hi = x.astype(BF16)
    return hi, (x - hi.astype(F32)).astype(BF16)


def _dot_split(a, b):
    a_hi, a_lo = _split_bf16(a)
    b_hi, b_lo = _split_bf16(b)
    dot = functools.partial(jnp.dot, preferred_element_type=F32)
    return dot(a_hi, b_hi) + (dot(a_hi, b_lo) + dot(a_lo, b_hi))


def _ops_kernel(row_ref, col_ref, bt_ref, ct_ref, dsk_ref, toep_ref, bs_ref, cs_ref, a_ref):
    def discretise(lr, li, ls):
        dt = jnp.exp(ls)
        mag = jnp.exp(lr * dt)
        return mag * jnp.cos(li * dt), mag * jnp.sin(li * dt)

    lr, li = row_ref[0], row_ref[1]
    lb_re, lb_im = discretise(lr, li, row_ref[2])
    den = lr * lr + li * li
    nr = lb_re - 1.0
    coef_re = (nr * lr + lb_im * li) / den
    coef_im = (lb_im * lr - nr * li) / den
    bb_re, bb_im = _cmul(coef_re, coef_im, bt_ref[0], bt_ref[1])
    bb_re = jnp.concatenate([bb_re] * CHUNK, axis=0)
    bb_im = jnp.concatenate([bb_im] * CHUNK, axis=0)

    row_t = lax.broadcasted_iota(jnp.int32, (CHUNK_W, LANES), 0) // SSM_GROUP
    fwd_lane = lax.broadcasted_iota(jnp.int32, (CHUNK_W, LANES), 1) < SSM_STATE
    col_t = lax.broadcasted_iota(jnp.int32, (LANES, CHUNK_W), 1) // SSM_GROUP
    fwd_row = lax.broadcasted_iota(jnp.int32, (LANES, CHUNK_W), 0) < SSM_STATE
    ct_re, ct_im = ct_ref[0], ct_ref[1]
    w_re = jnp.zeros((CHUNK_W, LANES), F32)
    w_im = jnp.zeros((CHUNK_W, LANES), F32)
    toep = jnp.zeros((CHUNK_W, CHUNK_W), F32)
    for i in range(CHUNK):
        t_now = jnp.where(fwd_lane, i, CHUNK - 1 - i)
        w_re, w_im = _cmul(w_re, w_im, lb_re, lb_im)
        w_re = w_re + jnp.where(row_t == t_now, bb_re, 0.0)
        w_im = w_im + jnp.where(row_t == t_now, bb_im, 0.0)
        out_now = col_t == jnp.where(fwd_row, i, CHUNK - 1 - i)
        rhs = jnp.concatenate([jnp.where(out_now, ct_re, 0.0), jnp.where(out_now, -ct_im, 0.0)], axis=0)
        toep = toep + _dot_split(jnp.concatenate([w_re, w_im], axis=1), rhs)
    r = lax.broadcasted_iota(jnp.int32, (CHUNK_W, CHUNK_W), 0)
    c = lax.broadcasted_iota(jnp.int32, (CHUNK_W, CHUNK_W), 1)
    toep_ref[...] = (toep + jnp.where(r == c, dsk_ref[...], 0.0)).astype(BF16)
    bs_ref[...] = jnp.concatenate([w_re, w_im], axis=1).astype(BF16)

    pr, pi = discretise(col_ref[0], col_ref[1], col_ref[2])
    k = jnp.where(fwd_row, col_t + 1, CHUNK - col_t)
    acc_re = jnp.ones((LANES, CHUNK_W), F32)
    acc_im = jnp.zeros((LANES, CHUNK_W), F32)
    for bit in range(CHUNK.bit_length()):
        nre, nim = _cmul(acc_re, acc_im, pr, pi)
        take = (k & (1 << bit)) != 0
        acc_re = jnp.where(take, nre, acc_re)
        acc_im = jnp.where(take, nim, acc_im)
        pr, pi = _cmul(pr, pi, pr, pi)
    cp_re, cp_im = _cmul(ct_re, ct_im, acc_re, acc_im)
    cs_ref[...] = jnp.concatenate([cp_re, -cp_im], axis=0).astype(BF16)

    a_re, a_im = lb_re, lb_im
    for _ in range(CHUNK.bit_length() - 1):
        a_re, a_im = _cmul(a_re, a_im, a_re, a_im)
    a_ref[0] = a_re
    a_ref[1] = a_im


def _ssm_operators(lam_re, lam_im, log_step, b_re, b_im, c_re, c_im, d_skip):
    g_first = lambda a: jnp.transpose(a.astype(F32), (1, 0, 2)).reshape(N_GROUPS, 2 * SSM_STATE)
    params = jnp.stack([g_first(lam_re), g_first(lam_im),
                        jnp.repeat(log_step.astype(F32).T, SSM_STATE, axis=1)], axis=1)
    b_t = lambda b: jnp.transpose(b.astype(F32), (1, 3, 0, 2)).reshape(N_GROUPS, SSM_GROUP, 2 * SSM_STATE)
    c_t = lambda c: jnp.tile(
        jnp.transpose(c.astype(F32), (1, 0, 3, 2)).reshape(N_GROUPS, 2 * SSM_STATE, SSM_GROUP), (1, 1, CHUNK))
    bt = jnp.stack([b_t(b_re), b_t(b_im)], axis=1)
    ct = jnp.stack([c_t(c_re), c_t(c_im)], axis=1)
    dsk = jnp.tile(d_skip.astype(F32), (1, CHUNK))[:, None, :]
    grp = lambda *shape: pl.BlockSpec((None,) + shape, lambda g: (g,) + (0,) * len(shape))
    mat = jax.ShapeDtypeStruct((N_GROUPS, CHUNK_W, CHUNK_W), BF16)
    toep, bs, cs, a = pl.pallas_call(
        _ops_kernel,
        grid=(N_GROUPS,),
        in_specs=[grp(3, 1, LANES), grp(3, LANES, 1), grp(2, SSM_GROUP, LANES), grp(2, LANES, CHUNK_W),
                  grp(1, CHUNK_W)],
        out_specs=(grp(CHUNK_W, CHUNK_W), grp(CHUNK_W, STATE_W), grp(STATE_W, CHUNK_W), grp(2, 1, LANES)),
        out_shape=(mat, mat, mat, jax.ShapeDtypeStruct((N_GROUPS, 2, 1, LANES), F32)),
        compiler_params=pltpu.CompilerParams(dimension_semantics=("arbitrary",)),
        name="s5_ops",
    )(params[:, :, None, :], params[:, :, :, None], bt, ct, dsk)
    return toep, bs, cs, a[:, 0], a[:, 1]


def _t5_buckets_np(rel):
    half = N_BUCKETS // 2
    max_exact = half // 2
    ret = np.where(rel > 0, half, 0)
    n = np.abs(rel)
    nf = np.maximum(n, 1).astype(np.float64)
    large = max_exact + (np.log(nf / max_exact) / math.log(MAX_DISTANCE / max_exact)
                         * (half - max_exact)).astype(np.int32)
    large = np.minimum(large, half - 1)
    return ret + np.where(n < max_exact, n, large)


def _attn_tables(rel_bias):
    offsets = (0, -BLOCK, -2 * BLOCK)
    period = 4 * BLOCK
    dist = np.arange(period)
    dist = np.where(dist < KEYS, dist, dist - period)
    rel = np.stack([dist + off for off in offsets])
    per_dist = rel_bias.astype(F32)[_t5_buckets_np(rel)] * LOG2_E
    per_dist = jnp.where((np.abs(rel) <= WINDOW)[:, :, None], per_dist, NEG_INF)
    per_dist = jnp.transpose(per_dist, (0, 2, 1))
    neg = jnp.concatenate([per_dist[:, :, :1], per_dist[:, :, :0:-1]], axis=-1)
    skew = jnp.tile(neg, (1, 1, KEYS))[:, :, :KEYS * (period - 1)]
    bias = skew.reshape(3, N_HEADS, KEYS, period - 1)[:, :, :, :BLOCK]
    bias = bias.reshape(3, N_KV_HEADS, 2, 2, KEYS, BLOCK)
    bias = jnp.concatenate([bias[:, :, 0], bias[:, :, 1]], axis=-1)
    return bias.reshape(3, N_KV_HEADS, 2 * KEYS, 2 * LANES)


def _pack_w_in(w):
    u, zs, q, k, v, za = jnp.split(w, (512, 1024, 1536, 1664, 1792), axis=1)
    dup = lambda t: jnp.concatenate([t[:, :HEAD_DIM], t[:, :HEAD_DIM], t[:, HEAD_DIM:], t[:, HEAD_DIM:]], axis=1)
    return jnp.concatenate([u, zs, q, dup(k), dup(v), za], axis=1).astype(BF16)


def kernel(x_prompt, x_sample, norm_w, w_in, lam_re, lam_im, log_step, b_re, b_im, c_re, c_im, d_skip,
           w_glu, b_glu, ssm_norm_w, sink, attn_norm_w, w_out, rel_bias, final_norm_w):
    assert norm_w.shape[0] == 1, "single-layer encoder"
    ops = _ssm_operators(lam_re[0], lam_im[0], log_step[0], b_re[0], b_im[0], c_re[0], c_im[0], d_skip[0])
    bias = _attn_tables(rel_bias)
    w_cat = _pack_w_in(w_in[0])
    w_glu_b = w_glu[0].astype(BF16)
    w_out_b = w_out[0].astype(BF16)
    sink_f = sink[0].astype(F32) * LOG2_E

    outs = []
    for x in (x_prompt, x_sample):
        batch, seq_len, _ = x.shape
        assert batch % SEQ_TILE == 0 and seq_len % TOK_TILE == 0
        n_tok = batch * seq_len
        up, zs, q, kd, vd, za = _proj_call(x, norm_w, w_cat)
        yp = _ssm_call(up, ops)
        ao = _attn_call(q.reshape(n_tok, D_ATTN), kd.reshape(n_tok, 2 * LANES), vd.reshape(n_tok, 2 * LANES),
                        bias, sink_f, batch, seq_len)
        outs.append(_post_call(x, yp, zs, ao.reshape(batch, seq_len, D_ATTN), za, w_glu_b, b_glu, ssm_norm_w,
                               attn_norm_w, w_out_b, final_norm_w.reshape(1, D_MODEL)))
    return tuple(outs)
```

```python
import functools
import math

import numpy as np
import jax
import jax.numpy as jnp
from jax import lax
from jax.experimental import pallas as pl
from jax.experimental.pallas import tpu as pltpu

F32 = jnp.float32
BF16 = jnp.bfloat16

D_MODEL = 1024
D_SSM = 512
D_ATTN = 512
SSM_GROUP = 16
N_GROUPS = D_SSM // SSM_GROUP
SSM_STATE = 64
HEAD_DIM = 64
N_HEADS = D_ATTN // HEAD_DIM
N_KV_HEADS = 2
WINDOW = 128
BLOCK = 128
N_BUCKETS = 32
MAX_DISTANCE = 128
RMS_EPS = 1e-6
NEG_INF = -1e30
LOG2_E = math.log2(math.e)

LANES = 128
SUBLANES = 8
CHUNK = 16
CHUNK_W = CHUNK * SSM_GROUP
STATE_W = 4 * SSM_STATE
SEQ_TILE = SUBLANES
TILE_CHUNKS = LANES // SEQ_TILE
TOK_TILE = TILE_CHUNKS * CHUNK
TILE_ROWS = SEQ_TILE * TOK_TILE
SUB_SEQS = 4
SUB = SUB_SEQS * TOK_TILE
N_SUB = SEQ_TILE // SUB_SEQS
LANE_TILES = D_SSM // LANES
GROUPS_PER_LANE_TILE = LANES // SSM_GROUP
KEYS = 3 * BLOCK
Q_ROWS = 1024
BLOCKS_PER_ITER = 4
SUM_ROWS = 16
MM_ROWS = 512
VMEM_LIMIT = 56 * 1024 * 1024

_C_U, _C_ZS, _C_Q, _C_K, _C_V, _C_ZA, _C_END = 0, 512, 1024, 1536, 1792, 2048, 2560


def _rms(x, w):
    return (x * lax.rsqrt(jnp.mean(x * x, axis=-1, keepdims=True) + RMS_EPS)) * w


def _tile_specs(width):
    return pl.BlockSpec((SUB_SEQS, TOK_TILE, width), lambda sb, cb, s: (sb * N_SUB + s, cb, 0))


_CHUNK_TILE_SPEC = pl.BlockSpec((N_GROUPS, None, None, LANES, CHUNK_W), lambda sb, cb, s: (0, sb, cb, 0, 0))


def _const_spec(shape):
    return pl.BlockSpec(shape, lambda sb, cb, s: (0,) * len(shape))


PAIR_ROWS = 2 * SEQ_TILE * CHUNK


def _pair_perm(t_major_out):
    out_row = lax.broadcasted_iota(jnp.int32, (PAIR_ROWS, PAIR_ROWS), 0)
    in_row = lax.broadcasted_iota(jnp.int32, (PAIR_ROWS, PAIR_ROWS), 1)
    tcs, cst = (out_row, in_row) if t_major_out else (in_row, out_row)
    hit = (tcs // (2 * SEQ_TILE) == cst % CHUNK) & (tcs % (2 * SEQ_TILE) == cst // CHUNK)
    return jnp.where(hit, 1.0, 0.0).astype(BF16)


def _proj_kernel(x_ref, nw_ref, w_ref, up_ref, zs_ref, q_ref, kd_ref, vd_ref, za_ref, u_scr, b_scr, t_scr):
    s = pl.program_id(2)
    hb = _rms(x_ref[...].reshape(SUB, D_MODEL), nw_ref[...]).astype(BF16)

    def mm(lo, hi):
        return jnp.dot(hb, w_ref[:, lo:hi], preferred_element_type=F32)

    def put(ref, val):
        ref[...] = val.astype(BF16).reshape(ref.shape)

    u_scr[pl.ds(pl.multiple_of(s * SUB, SUB), SUB), :] = mm(_C_U, _C_ZS).astype(BF16)
    put(zs_ref, mm(_C_ZS, _C_Q))
    put(q_ref, mm(_C_Q, _C_K) * (HEAD_DIM ** -0.5 * LOG2_E))
    put(kd_ref, mm(_C_K, _C_V))
    put(vd_ref, mm(_C_V, _C_ZA))
    put(za_ref, mm(_C_ZA, _C_END))

    @pl.when(s == N_SUB - 1)
    def _():
        perm = _pair_perm(t_major_out=True)
        for cp in range(TILE_CHUNKS // 2):
            a = jnp.concatenate(
                [u_scr[pl.ds(sq * TOK_TILE + (2 * cp + c2) * CHUNK, CHUNK), :]
                 for c2 in range(2) for sq in range(SEQ_TILE)], axis=0)
            b_scr[cp] = jnp.dot(perm, a, preferred_element_type=F32).astype(BF16)
        for t in range(CHUNK):
            for j in range(LANE_TILES):
                blk = jnp.concatenate(
                    [b_scr[cp, pl.ds(t * 2 * SEQ_TILE, 2 * SEQ_TILE), pl.ds(j * LANES, LANES)]
                     for cp in range(TILE_CHUNKS // 2)], axis=0)
                t_scr[t, j] = blk.T
        for g in range(N_GROUPS):
            j, g3 = divmod(g, GROUPS_PER_LANE_TILE)
            for th in range(CHUNK_W // LANES):
                rows = jnp.concatenate(
                    [t_scr[th * 8 + t3, j, pl.ds(g3 * SSM_GROUP, SSM_GROUP), :] for t3 in range(8)], axis=0)
                up_ref[g, :, pl.ds(th * LANES, LANES)] = rows.T


def _proj_call(x, norm_w, w_cat):
    batch, seq_len, _ = x.shape
    nsb, ncb = batch // SEQ_TILE, seq_len // TOK_TILE
    tok = lambda width: jax.ShapeDtypeStruct((batch, seq_len, width), BF16)
    out_shapes = (
        jax.ShapeDtypeStruct((N_GROUPS, nsb, ncb, LANES, CHUNK_W), BF16),
        tok(D_SSM),
        tok(D_ATTN),
        tok(2 * LANES),
        tok(2 * LANES),
        tok(D_ATTN),
    )
    return pl.pallas_call(
        _proj_kernel,
        grid=(nsb, ncb, N_SUB),
        in_specs=[_tile_specs(D_MODEL), _const_spec((1, D_MODEL)), _const_spec((D_MODEL, _C_END))],
        out_specs=(_CHUNK_TILE_SPEC, _tile_specs(D_SSM), _tile_specs(D_ATTN), _tile_specs(2 * LANES),
                   _tile_specs(2 * LANES), _tile_specs(D_ATTN)),
        out_shape=out_shapes,
        scratch_shapes=[
            pltpu.VMEM((TILE_ROWS, D_SSM), BF16),
            pltpu.VMEM((TILE_CHUNKS // 2, PAIR_ROWS, D_SSM), BF16),
            pltpu.VMEM((CHUNK, LANE_TILES, LANES, LANES), BF16),
        ],
        compiler_params=pltpu.CompilerParams(
            dimension_semantics=("arbitrary", "arbitrary", "arbitrary"), vmem_limit_bytes=VMEM_LIMIT),
        name="proj",
    )(x, norm_w, w_cat)


def _ssm_kernel(u_ref, toep_ref, bs_ref, cs_ref, are_ref, aim_ref, y_ref, s_scr, xf_scr, xb_scr,
                *, nsb, ncb):
    nch = ncb * TILE_CHUNKS
    tiles_per_mm = MM_ROWS // LANES
    seq_rows = nsb * SEQ_TILE

    def mm_blocks():
        for sb in range(nsb):
            for cq in range(ncb // tiles_per_mm):
                yield sb, cq * tiles_per_mm, pl.ds((sb * ncb + cq * tiles_per_mm) * LANES, MM_ROWS)

    def load_u(sb, cb0):
        return jnp.concatenate([u_ref[sb, cb0 + i] for i in range(tiles_per_mm)], axis=0)

    for sb, cb0, rows in mm_blocks():
        st = jnp.dot(load_u(sb, cb0), bs_ref[...], preferred_element_type=F32)
        s_scr[0, rows, :] = st[:, :LANES]
        s_scr[1, rows, :] = st[:, LANES:]

    def chunk_rows(sb, ch):
        return pl.ds(pl.multiple_of(sb * nch * SEQ_TILE + ch * SEQ_TILE, SEQ_TILE), SEQ_TILE)

    def load(ref, part, ch):
        return jnp.concatenate([ref[part, chunk_rows(sb, ch), :] for sb in range(nsb)], axis=0)

    def store(ref, part, ch, val):
        for sb in range(nsb):
            ref[part, chunk_rows(sb, ch), :] = val[sb * SEQ_TILE:(sb + 1) * SEQ_TILE, :]

    are = jnp.broadcast_to(are_ref[...], (seq_rows, LANES))
    aim = jnp.broadcast_to(aim_ref[...], (seq_rows, LANES))
    is_fwd = lax.broadcasted_iota(jnp.int32, (seq_rows, LANES), 1) < SSM_STATE

    def step(i, carry):
        xr, xi = carry
        j = nch - 1 - i
        store(xf_scr, 0, i, xr)
        store(xf_scr, 1, i, xi)
        store(xb_scr, 0, j, xr)
        store(xb_scr, 1, j, xi)
        s_re = jnp.where(is_fwd, load(s_scr, 0, i), load(s_scr, 0, j))
        s_im = jnp.where(is_fwd, load(s_scr, 1, i), load(s_scr, 1, j))
        return (are * xr - aim * xi + s_re, are * xi + aim * xr + s_im)

    zero = jnp.zeros((seq_rows, LANES), F32)
    lax.fori_loop(0, nch, step, (zero, zero), unroll=4)

    fwd_lane = lax.broadcasted_iota(jnp.int32, (MM_ROWS, LANES), 1) < SSM_STATE
    for sb, cb0, rows in mm_blocks():
        xin = jnp.concatenate(
            [jnp.where(fwd_lane, xf_scr[c, rows, :], xb_scr[c, rows, :]) for c in range(2)], axis=1).astype(BF16)
        y = (jnp.dot(load_u(sb, cb0), toep_ref[...], preferred_element_type=F32)
             + jnp.dot(xin, cs_ref[...], preferred_element_type=F32))
        for i in range(tiles_per_mm):
            y_ref[sb, cb0 + i] = y[i * LANES:(i + 1) * LANES, :].astype(BF16)


def _ssm_call(up, ops):
    toep, bs, cs, are, aim = ops
    _, nsb, ncb, _, _ = up.shape
    n_rows = nsb * ncb * LANES
    assert ncb % (MM_ROWS // LANES) == 0
    data_spec = pl.BlockSpec((None, nsb, ncb, LANES, CHUNK_W), lambda g: (g, 0, 0, 0, 0))
    grp = lambda *shape: pl.BlockSpec((None,) + shape, lambda g: (g, 0, 0))
    return pl.pallas_call(
        functools.partial(_ssm_kernel, nsb=nsb, ncb=ncb),
        grid=(N_GROUPS,),
        in_specs=[data_spec, grp(CHUNK_W, CHUNK_W), grp(CHUNK_W, STATE_W), grp(STATE_W, CHUNK_W),
                  grp(1, LANES), grp(1, LANES)],
        out_specs=data_spec,
        out_shape=jax.ShapeDtypeStruct(up.shape, BF16),
        scratch_shapes=[pltpu.VMEM((2, n_rows, LANES), F32)] * 3,
        compiler_params=pltpu.CompilerParams(
            dimension_semantics=("arbitrary",), vmem_limit_bytes=VMEM_LIMIT),
        name="ssm",
    )(up, toep, bs, cs, are, aim)


def _attn_kernel(sink_ref, q_ref, kd_ref, vd_ref, bias_ref, o_ref, ka_scr, kb_scr, wa_scr, wb_scr, *, seq_len):
    jb = pl.program_id(1)
    nb = seq_len // BLOCK

    @pl.when(jb == 0)
    def _():
        low = lax.broadcasted_iota(jnp.int32, (BLOCK, LANES), 1) < HEAD_DIM
        zeros = jnp.zeros((HEAD_DIM, BLOCK), BF16)
        ind_a = jnp.where(lax.broadcasted_iota(jnp.int32, (SUM_ROWS, BLOCK), 0) < SUM_ROWS // 2, 1.0, 0.0)
        ind_b = (1.0 - ind_a).astype(BF16)
        ind_a = ind_a.astype(BF16)

        def per_block(r, carry):
            rows = pl.ds(pl.multiple_of(r * BLOCK, BLOCK), BLOCK)
            for kh in range(N_KV_HEADS):
                kd = kd_ref[rows, pl.ds(kh * LANES, LANES)]
                ka_scr[kh, rows, :] = jnp.where(low, kd, jnp.zeros_like(kd))
                kb_scr[kh, rows, :] = jnp.where(low, jnp.zeros_like(kd), kd)
                vt = vd_ref[rows, pl.ds(kh * LANES, LANES)].T
                wa_scr[kh, r] = jnp.concatenate([vt[:HEAD_DIM], zeros, ind_a], axis=0)
                wb_scr[kh, r] = jnp.concatenate([zeros, vt[HEAD_DIM:], ind_b], axis=0)
            return carry

        lax.fori_loop(0, nb, per_block, 0)

    col_tile0 = lax.broadcasted_iota(jnp.int32, (1, 2 * LANES), 1) < LANES
    row_slot0 = lax.broadcasted_iota(jnp.int32, (BLOCK, 2 * LANES), 0) < HEAD_DIM
    def query_block(sb):
        n = jb * (Q_ROWS // BLOCK) + sb
        kb0 = jnp.clip(n - 1, 0, nb - KEYS // BLOCK)
        ks = pl.multiple_of(kb0 * BLOCK, BLOCK)
        var = jnp.where(n == 0, 0, jnp.where(n == nb - 1, 2, 1))
        qrows = pl.ds(pl.multiple_of(sb * BLOCK, BLOCK), BLOCK)
        for kh in range(N_KV_HEADS):
            kcat = jnp.concatenate([ka_scr[kh, pl.ds(ks, KEYS), :], kb_scr[kh, pl.ds(ks, KEYS), :]], axis=0)
            q2 = jnp.concatenate([q_ref[qrows, pl.ds((2 * kh + jj) * LANES, LANES)] for jj in range(2)], axis=0)
            t = lax.dot_general(kcat, q2, (((1,), (1,)), ((), ())), preferred_element_type=F32)
            t = t + bias_ref[var, kh]
            ps, sink_terms = [], []
            for hh in range(2):
                th = t[hh * KEYS:(hh + 1) * KEYS]
                sink = jnp.where(col_tile0, sink_ref[4 * kh + hh], sink_ref[4 * kh + 2 + hh])
                m = jnp.maximum(jnp.max(th, axis=0, keepdims=True), sink)
                ps.append(jnp.exp2(th - m).astype(BF16))
                sink_terms.append(jnp.exp2(sink - m))
            w = jnp.concatenate([wa_scr[kh, kb0 + i] for i in range(KEYS // BLOCK)]
                                + [wb_scr[kh, kb0 + i] for i in range(KEYS // BLOCK)], axis=1)
            o2 = jnp.dot(w, jnp.concatenate(ps, axis=0), preferred_element_type=F32)
            sums = [o2[LANES + hh * (SUM_ROWS // 2):LANES + hh * (SUM_ROWS // 2) + 1] for hh in range(2)]
            denom = jnp.where(row_slot0, sums[0] + sink_terms[0], sums[1] + sink_terms[1])
            on = o2[:LANES] / denom
            for jj in range(2):
                o_ref[qrows, pl.ds((2 * kh + jj) * LANES, LANES)] = (
                    on[:, jj * LANES:(jj + 1) * LANES].T.astype(BF16))

    def query_blocks(i, carry):
        for sb in range(BLOCKS_PER_ITER):
            query_block(i * BLOCKS_PER_ITER + sb)
        return carry

    lax.fori_loop(0, Q_ROWS // (BLOCKS_PER_ITER * BLOCK), query_blocks, 0)


def _attn_call(q, kd, vd, bias, sink, batch, seq_len):
    assert seq_len % Q_ROWS == 0 and seq_len >= KEYS
    nq = seq_len // Q_ROWS
    nb = seq_len // BLOCK
    return pl.pallas_call(
        functools.partial(_attn_kernel, seq_len=seq_len),
        grid_spec=pltpu.PrefetchScalarGridSpec(
            num_scalar_prefetch=1,
            grid=(batch, nq),
            in_specs=[
                pl.BlockSpec((Q_ROWS, D_ATTN), lambda b, j, s: (b * nq + j, 0)),
                pl.BlockSpec((seq_len, 2 * LANES), lambda b, j, s: (b, 0)),
                pl.BlockSpec((seq_len, 2 * LANES), lambda b, j, s: (b, 0)),
                pl.BlockSpec((3, N_KV_HEADS, 2 * KEYS, 2 * LANES), lambda b, j, s: (0, 0, 0, 0)),
            ],
            out_specs=pl.BlockSpec((Q_ROWS, D_ATTN), lambda b, j, s: (b * nq + j, 0)),
            scratch_shapes=[
                pltpu.VMEM((N_KV_HEADS, seq_len, LANES), BF16),
                pltpu.VMEM((N_KV_HEADS, seq_len, LANES), BF16),
                pltpu.VMEM((N_KV_HEADS, nb, LANES + SUM_ROWS, BLOCK), BF16),
                pltpu.VMEM((N_KV_HEADS, nb, LANES + SUM_ROWS, BLOCK), BF16),
            ],
        ),
        out_shape=jax.ShapeDtypeStruct((batch * seq_len, D_ATTN), BF16),
        compiler_params=pltpu.CompilerParams(
            dimension_semantics=("arbitrary", "arbitrary"), vmem_limit_bytes=VMEM_LIMIT),
        name="attn",
    )(sink, q, kd, vd, bias)


def _gelu_tanh(x):
    c = math.sqrt(2.0 / math.pi)
    return x * (0.5 * (1.0 + jnp.tanh(c * (x + 0.044715 * (x * x * x)))))


def _silu(x):
    return x * jax.nn.sigmoid(x)


def _post_kernel(x_ref, yp_ref, zs_ref, ao_ref, za_ref, wglu_ref, bglu_ref, snw_ref, anw_ref,
                 wout_ref, fnw_ref, o_ref, y_scr, b_scr, t_scr):
    s = pl.program_id(2)

    @pl.when(s == 0)
    def _():
        for g in range(N_GROUPS):
            j, g3 = divmod(g, GROUPS_PER_LANE_TILE)
            for th in range(CHUNK_W // LANES):
                tt = yp_ref[g, :, pl.ds(th * LANES, LANES)].T
                for t3 in range(8):
                    t_scr[th * 8 + t3, j, pl.ds(g3 * SSM_GROUP, SSM_GROUP), :] = (
                        tt[t3 * SSM_GROUP:(t3 + 1) * SSM_GROUP, :])
        for t in range(CHUNK):
            for j in range(LANE_TILES):
                b_scr[t, :, pl.ds(j * LANES, LANES)] = t_scr[t, j].T
        perm = _pair_perm(t_major_out=False)
        for cp in range(TILE_CHUNKS // 2):
            a = jnp.concatenate(
                [b_scr[t, pl.ds(cp * 2 * SEQ_TILE, 2 * SEQ_TILE), :] for t in range(CHUNK)], axis=0)
            ys = jnp.dot(perm, a, preferred_element_type=F32)
            for c2 in range(2):
                for sq in range(SEQ_TILE):
                    r0 = (c2 * SEQ_TILE + sq) * CHUNK
                    y_scr[pl.ds(sq * TOK_TILE + (2 * cp + c2) * CHUNK, CHUNK), :] = ys[r0:r0 + CHUNK, :]

    def get(ref):
        return ref[...].reshape(SUB, ref.shape[-1]).astype(F32)

    y = y_scr[pl.ds(pl.multiple_of(s * SUB, SUB), SUB), :]
    g = _gelu_tanh(y)
    gate = jax.nn.sigmoid(
        jnp.dot(g.astype(BF16), wglu_ref[...], preferred_element_type=F32) + bglu_ref[...])
    n_ssm = _rms(g * gate, snw_ref[...]) * _silu(get(zs_ref))
    n_attn = _rms(get(ao_ref), anw_ref[...]) * _silu(get(za_ref))
    mixed = jnp.concatenate([n_ssm, n_attn], axis=1).astype(BF16)
    res = get(x_ref) + jnp.dot(mixed, wout_ref[...], preferred_element_type=F32)
    o_ref[...] = _rms(res, fnw_ref[...]).reshape(o_ref.shape)


def _post_call(x, yp, zs, ao, za, w_glu, b_glu, ssm_norm_w, attn_norm_w, w_out, final_norm_w):
    batch, seq_len, _ = x.shape
    nsb, ncb = batch // SEQ_TILE, seq_len // TOK_TILE
    return pl.pallas_call(
        _post_kernel,
        grid=(nsb, ncb, N_SUB),
        in_specs=[
            _tile_specs(D_MODEL), _CHUNK_TILE_SPEC, _tile_specs(D_SSM), _tile_specs(D_ATTN), _tile_specs(D_ATTN),
            _const_spec((D_SSM, D_SSM)), _const_spec((1, D_SSM)), _const_spec((1, D_SSM)),
            _const_spec((1, D_ATTN)), _const_spec((D_MODEL, D_MODEL)), _const_spec((1, D_MODEL)),
        ],
        out_specs=_tile_specs(D_MODEL),
        out_shape=jax.ShapeDtypeStruct((batch, seq_len, D_MODEL), F32),
        scratch_shapes=[
            pltpu.VMEM((TILE_ROWS, D_SSM), F32),
            pltpu.VMEM((CHUNK, LANES, D_SSM), BF16),
            pltpu.VMEM((CHUNK, LANE_TILES, LANES, LANES), BF16),
        ],
        compiler_params=pltpu.CompilerParams(
            dimension_semantics=("arbitrary", "arbitrary", "arbitrary"), vmem_limit_bytes=VMEM_LIMIT),
        name="post",
    )(x, yp, zs, ao, za, w_glu, b_glu, ssm_norm_w, attn_norm_w, w_out, final_norm_w)


def _cmul(ar, ai, br, bi):
    return ar * br - ai * bi, ar * bi + ai * br


def _split_bf16(x):
    hi = x.astype(BF16)
    return hi, (x - hi.astype(F32)).astype(BF16)


def _dot_split(a, b):
    a_hi, a_lo = _split_bf16(a)
    b_hi, b_lo = _split_bf16(b)
    dot = functools.partial(jnp.dot, preferred_element_type=F32)
    return dot(a_hi, b_hi) + (dot(a_hi, b_lo) + dot(a_lo, b_hi))


def _ops_kernel(row_ref, col_ref, bt_ref, ct_ref, dsk_ref, toep_ref, bs_ref, cs_ref, a_ref):
    def discretise(lr, li, ls):
        dt = jnp.exp(ls)
        mag = jnp.exp(lr * dt)
        return mag * jnp.cos(li * dt), mag * jnp.sin(li * dt)

    lr, li = row_ref[0], row_ref[1]
    lb_re, lb_im = discretise(lr, li, row_ref[2])
    den = lr * lr + li * li
    nr = lb_re - 1.0
    coef_re = (nr * lr + lb_im * li) / den
    coef_im = (lb_im * lr - nr * li) / den
    bb_re, bb_im = _cmul(coef_re, coef_im, bt_ref[0], bt_ref[1])
    bb_re = jnp.concatenate([bb_re] * CHUNK, axis=0)
    bb_im = jnp.concatenate([bb_im] * CHUNK, axis=0)

    row_t = lax.broadcasted_iota(jnp.int32, (CHUNK_W, LANES), 0) // SSM_GROUP
    fwd_lane = lax.broadcasted_iota(jnp.int32, (CHUNK_W, LANES), 1) < SSM_STATE
    col_t = lax.broadcasted_iota(jnp.int32, (LANES, CHUNK_W), 1) // SSM_GROUP
    fwd_row = lax.broadcasted_iota(jnp.int32, (LANES, CHUNK_W), 0) < SSM_STATE
    ct_re, ct_im = ct_ref[0], ct_ref[1]
    w_re = jnp.zeros((CHUNK_W, LANES), F32)
    w_im = jnp.zeros((CHUNK_W, LANES), F32)
    toep = jnp.zeros((CHUNK_W, CHUNK_W), F32)
    for i in range(CHUNK):
        t_now = jnp.where(fwd_lane, i, CHUNK - 1 - i)
        w_re, w_im = _cmul(w_re, w_im, lb_re, lb_im)
        w_re = w_re + jnp.where(row_t == t_now, bb_re, 0.0)
        w_im = w_im + jnp.where(row_t == t_now, bb_im, 0.0)
        out_now = col_t == jnp.where(fwd_row, i, CHUNK - 1 - i)
        rhs = jnp.concatenate([jnp.where(out_now, ct_re, 0.0), jnp.where(out_now, -ct_im, 0.0)], axis=0)
        toep = toep + _dot_split(jnp.concatenate([w_re, w_im], axis=1), rhs)
    r = lax.broadcasted_iota(jnp.int32, (CHUNK_W, CHUNK_W), 0)
    c = lax.broadcasted_iota(jnp.int32, (CHUNK_W, CHUNK_W), 1)
    toep_ref[...] = (toep + jnp.where(r == c, dsk_ref[...], 0.0)).astype(BF16)
    bs_ref[...] = jnp.concatenate([w_re, w_im], axis=1).astype(BF16)

    pr, pi = discretise(col_ref[0], col_ref[1], col_ref[2])
    k = jnp.where(fwd_row, col_t + 1, CHUNK - col_t)
    acc_re = jnp.ones((LANES, CHUNK_W), F32)
    acc_im = jnp.zeros((LANES, CHUNK_W), F32)
    for bit in range(CHUNK.bit_length()):
        nre, nim = _cmul(acc_re, acc_im, pr, pi)
        take = (k & (1 << bit)) != 0
        acc_re = jnp.where(take, nre, acc_re)
        acc_im = jnp.where(take, nim, acc_im)
        pr, pi = _cmul(pr, pi, pr, pi)
    cp_re, cp_im = _cmul(ct_re, ct_im, acc_re, acc_im)
    cs_ref[...] = jnp.concatenate([cp_re, -cp_im], axis=0).astype(BF16)

    a_re, a_im = lb_re, lb_im
    for _ in range(CHUNK.bit_length() - 1):
        a_re, a_im = _cmul(a_re, a_im, a_re, a_im)
    a_ref[0] = a_re
    a_ref[1] = a_im


def _ssm_operators(lam_re, lam_im, log_step, b_re, b_im, c_re, c_im, d_skip):
    g_first = lambda a: jnp.transpose(a.astype(F32), (1, 0, 2)).reshape(N_GROUPS, 2 * SSM_STATE)
    params = jnp.stack([g_first(lam_re), g_first(lam_im),
                        jnp.repeat(log_step.astype(F32).T, SSM_STATE, axis=1)], axis=1)
    b_t = lambda b: jnp.transpose(b.astype(F32), (1, 3, 0, 2)).reshape(N_GROUPS, SSM_GROUP, 2 * SSM_STATE)
    c_t = lambda c: jnp.tile(
        jnp.transpose(c.astype(F32), (1, 0, 3, 2)).reshape(N_GROUPS, 2 * SSM_STATE, SSM_GROUP), (1, 1, CHUNK))
    bt = jnp.stack([b_t(b_re), b_t(b_im)], axis=1)
    ct = jnp.stack([c_t(c_re), c_t(c_im)], axis=1)
    dsk = jnp.tile(d_skip.astype(F32), (1, CHUNK))[:, None, :]
    grp = lambda *shape: pl.BlockSpec((None,) + shape, lambda g: (g,) + (0,) * len(shape))
    mat = jax.ShapeDtypeStruct((N_GROUPS, CHUNK_W, CHUNK_W), BF16)
    toep, bs, cs, a = pl.pallas_call(
        _ops_kernel,
        grid=(N_GROUPS,),
        in_specs=[grp(3, 1, LANES), grp(3, LANES, 1), grp(2, SSM_GROUP, LANES), grp(2, LANES, CHUNK_W),
                  grp(1, CHUNK_W)],
        out_specs=(grp(CHUNK_W, CHUNK_W), grp(CHUNK_W, STATE_W), grp(STATE_W, CHUNK_W), grp(2, 1, LANES)),
        out_shape=(mat, mat, mat, jax.ShapeDtypeStruct((N_GROUPS, 2, 1, LANES), F32)),
        compiler_params=pltpu.CompilerParams(dimension_semantics=("arbitrary",)),
        name="s5_ops",
    )(params[:, :, None, :], params[:, :, :, None], bt, ct, dsk)
    return toep, bs, cs, a[:, 0], a[:, 1]


def _t5_buckets_np(rel):
    half = N_BUCKETS // 2
    max_exact = half // 2
    ret = np.where(rel > 0, half, 0)
    n = np.abs(rel)
    nf = np.maximum(n, 1).astype(np.float64)
    large = max_exact + (np.log(nf / max_exact) / math.log(MAX_DISTANCE / max_exact)
                         * (half - max_exact)).astype(np.int32)
    large = np.minimum(large, half - 1)
    return ret + np.where(n < max_exact, n, large)


def _attn_tables(rel_bias):
    offsets = (0, -BLOCK, -2 * BLOCK)
    period = 4 * BLOCK
    dist = np.arange(period)
    dist = np.where(dist < KEYS, dist, dist - period)
    rel = np.stack([dist + off for off in offsets])
    per_dist = rel_bias.astype(F32)[_t5_buckets_np(rel)] * LOG2_E
    per_dist = jnp.where((np.abs(rel) <= WINDOW)[:, :, None], per_dist, NEG_INF)
    per_dist = jnp.transpose(per_dist, (0, 2, 1))
    neg = jnp.concatenate([per_dist[:, :, :1], per_dist[:, :, :0:-1]], axis=-1)
    skew = jnp.tile(neg, (1, 1, KEYS))[:, :, :KEYS * (period - 1)]
    bias = skew.reshape(3, N_HEADS, KEYS, period - 1)[:, :, :, :BLOCK]
    bias = bias.reshape(3, N_KV_HEADS, 2, 2, KEYS, BLOCK)
    bias = jnp.concatenate([bias[:, :, 0], bias[:, :, 1]], axis=-1)
    return bias.reshape(3, N_KV_HEADS, 2 * KEYS, 2 * LANES)


def _pack_w_in(w):
    u, zs, q, k, v, za = jnp.split(w, (512, 1024, 1536, 1664, 1792), axis=1)
    dup = lambda t: jnp.concatenate([t[:, :HEAD_DIM], t[:, :HEAD_DIM], t[:, HEAD_DIM:], t[:, HEAD_DIM:]], axis=1)
    return jnp.concatenate([u, zs, q, dup(k), dup(v), za], axis=1).astype(BF16)


def kernel(x_prompt, x_sample, norm_w, w_in, lam_re, lam_im, log_step, b_re, b_im, c_re, c_im, d_skip,
           w_glu, b_glu, ssm_norm_w, sink, attn_norm_w, w_out, rel_bias, final_norm_w):
    assert norm_w.shape[0] == 1, "single-layer encoder"
    ops = _ssm_operators(lam_re[0], lam_im[0], log_step[0], b_re[0], b_im[0], c_re[0], c_im[0], d_skip[0])
    bias = _attn_tables(rel_bias)
    w_cat = _pack_w_in(w_in[0])
    w_glu_b = w_glu[0].astype(BF16)
    w_out_b = w_out[0].astype(BF16)
    sink_f = sink[0].astype(F32) * LOG2_E

    outs = []
    for x in (x_prompt, x_sample):
        batch, seq_len, _ = x.shape
        assert batch % SEQ_TILE == 0 and seq_len % TOK_TILE == 0
        n_tok = batch * seq_len
        up, zs, q, kd, vd, za = _proj_call(x, norm_w, w_cat)
        yp = _ssm_call(up, ops)
        ao = _attn_call(q.reshape(n_tok, D_ATTN), kd.reshape(n_tok, 2 * LANES), vd.reshape(n_tok, 2 * LANES),
                        bias, sink_f, batch, seq_len)
        outs.append(_post_call(x, yp, zs, ao.reshape(batch, seq_len, D_ATTN), za, w_glu_b, b_glu, ssm_norm_w,
                               attn_norm_w, w_out_b, final_norm_w.reshape(1, D_MODEL)))
    return tuple(outs)
```

```python
import functools
import math

import numpy as np
import jax
import jax.numpy as jnp
from jax import lax
from jax.experimental import pallas as pl
from jax.experimental.pallas import tpu as pltpu

F32 = jnp.float32
BF16 = jnp.bfloat16

D_MODEL = 1024
D_SSM = 512
D_ATTN = 512
SSM_GROUP = 16
N_GROUPS = D_SSM // SSM_GROUP
SSM_STATE = 64
HEAD_DIM = 64
N_HEADS = D_ATTN // HEAD_DIM
N_KV_HEADS = 2
WINDOW = 128
BLOCK = 128
N_BUCKETS = 32
MAX_DISTANCE = 128
RMS_EPS = 1e-6
NEG_INF = -1e30
LOG2_E = math.log2(math.e)

LANES = 128
SUBLANES = 8
CHUNK = 16
CHUNK_W = CHUNK * SSM_GROUP
STATE_W = 4 * SSM_STATE
SEQ_TILE = SUBLANES
TILE_CHUNKS = LANES // SEQ_TILE
TOK_TILE = TILE_CHUNKS * CHUNK
TILE_ROWS = SEQ_TILE * TOK_TILE
SUB_SEQS = 4
SUB = SUB_SEQS * TOK_TILE
N_SUB = SEQ_TILE // SUB_SEQS
LANE_TILES = D_SSM // LANES
GROUPS_PER_LANE_TILE = LANES // SSM_GROUP
KEYS = 3 * BLOCK
Q_ROWS = 1024
BLOCKS_PER_ITER = 4
SUM_ROWS = LANES
MM_ROWS = 512
VMEM_LIMIT = 56 * 1024 * 1024

_C_U, _C_ZS, _C_Q, _C_K, _C_V, _C_ZA, _C_END = 0, 512, 1024, 1536, 1792, 2048, 2560


def _rms(x, w):
    return (x * lax.rsqrt(jnp.mean(x * x, axis=-1, keepdims=True) + RMS_EPS)) * w


def _tile_specs(width):
    return pl.BlockSpec((SUB_SEQS, TOK_TILE, width), lambda sb, cb, s: (sb * N_SUB + s, cb, 0))


_CHUNK_TILE_SPEC = pl.BlockSpec((N_GROUPS, None, None, LANES, CHUNK_W), lambda sb, cb, s: (0, sb, cb, 0, 0))


def _const_spec(shape):
    return pl.BlockSpec(shape, lambda sb, cb, s: (0,) * len(shape))


PAIR_ROWS = 2 * SEQ_TILE * CHUNK


def _pair_perm(t_major_out):
    out_row = lax.broadcasted_iota(jnp.int32, (PAIR_ROWS, PAIR_ROWS), 0)
    in_row = lax.broadcasted_iota(jnp.int32, (PAIR_ROWS, PAIR_ROWS), 1)
    tcs, cst = (out_row, in_row) if t_major_out else (in_row, out_row)
    hit = (tcs // (2 * SEQ_TILE) == cst % CHUNK) & (tcs % (2 * SEQ_TILE) == cst // CHUNK)
    return jnp.where(hit, 1.0, 0.0).astype(BF16)


def _proj_kernel(x_ref, nw_ref, w_ref, up_ref, zs_ref, q_ref, kd_ref, vd_ref, za_ref, u_scr, b_scr, t_scr):
    s = pl.program_id(2)
    hb = _rms(x_ref[...].reshape(SUB, D_MODEL), nw_ref[...]).astype(BF16)

    def mm(lo, hi):
        return jnp.dot(hb, w_ref[:, lo:hi], preferred_element_type=F32)

    def put(ref, val):
        ref[...] = val.astype(BF16).reshape(ref.shape)

    u_scr[pl.ds(pl.multiple_of(s * SUB, SUB), SUB), :] = mm(_C_U, _C_ZS).astype(BF16)
    put(zs_ref, mm(_C_ZS, _C_Q))
    put(q_ref, mm(_C_Q, _C_K) * (HEAD_DIM ** -0.5 * LOG2_E))
    put(kd_ref, mm(_C_K, _C_V))
    put(vd_ref, mm(_C_V, _C_ZA))
    put(za_ref, mm(_C_ZA, _C_END))

    @pl.when(s == N_SUB - 1)
    def _():
        perm = _pair_perm(t_major_out=True)
        for cp in range(TILE_CHUNKS // 2):
            a = jnp.concatenate(
                [u_scr[pl.ds(sq * TOK_TILE + (2 * cp + c2) * CHUNK, CHUNK), :]
                 for c2 in range(2) for sq in range(SEQ_TILE)], axis=0)
            b_scr[cp] = jnp.dot(perm, a, preferred_element_type=F32).astype(BF16)
        for t in range(CHUNK):
            for j in range(LANE_TILES):
                blk = jnp.concatenate(
                    [b_scr[cp, pl.ds(t * 2 * SEQ_TILE, 2 * SEQ_TILE), pl.ds(j * LANES, LANES)]
                     for cp in range(TILE_CHUNKS // 2)], axis=0)
                t_scr[t, j] = blk.T
        for g in range(N_GROUPS):
            j, g3 = divmod(g, GROUPS_PER_LANE_TILE)
            for th in range(CHUNK_W // LANES):
                rows = jnp.concatenate(
                    [t_scr[th * 8 + t3, j, pl.ds(g3 * SSM_GROUP, SSM_GROUP), :] for t3 in range(8)], axis=0)
                up_ref[g, :, pl.ds(th * LANES, LANES)] = rows.T


def _proj_call(x, norm_w, w_cat):
    batch, seq_len, _ = x.shape
    nsb, ncb = batch // SEQ_TILE, seq_len // TOK_TILE
    tok = lambda width: jax.ShapeDtypeStruct((batch, seq_len, width), BF16)
    out_shapes = (
        jax.ShapeDtypeStruct((N_GROUPS, nsb, ncb, LANES, CHUNK_W), BF16),
        tok(D_SSM),
        tok(D_ATTN),
        tok(2 * LANES),
        tok(2 * LANES),
        tok(D_ATTN),
    )
    return pl.pallas_call(
        _proj_kernel,
        grid=(nsb, ncb, N_SUB),
        in_specs=[_tile_specs(D_MODEL), _const_spec((1, D_MODEL)), _const_spec((D_MODEL, _C_END))],
        out_specs=(_CHUNK_TILE_SPEC, _tile_specs(D_SSM), _tile_specs(D_ATTN), _tile_specs(2 * LANES),
                   _tile_specs(2 * LANES), _tile_specs(D_ATTN)),
        out_shape=out_shapes,
        scratch_shapes=[
            pltpu.VMEM((TILE_ROWS, D_SSM), BF16),
            pltpu.VMEM((TILE_CHUNKS // 2, PAIR_ROWS, D_SSM), BF16),
            pltpu.VMEM((CHUNK, LANE_TILES, LANES, LANES), BF16),
        ],
        compiler_params=pltpu.CompilerParams(
            dimension_semantics=("arbitrary", "arbitrary", "arbitrary"), vmem_limit_bytes=VMEM_LIMIT),
        name="proj",
    )(x, norm_w, w_cat)


def _ssm_kernel(u_ref, toep_ref, bs_ref, cs_ref, are_ref, aim_ref, y_ref, s_scr, xf_scr, xb_scr,
                *, nsb, ncb):
    nch = ncb * TILE_CHUNKS
    tiles_per_mm = MM_ROWS // LANES
    seq_rows = nsb * SEQ_TILE

    def mm_blocks():
        for sb in range(nsb):
            for cq in range(ncb // tiles_per_mm):
                yield sb, cq * tiles_per_mm, pl.ds((sb * ncb + cq * tiles_per_mm) * LANES, MM_ROWS)

    def load_u(sb, cb0):
        return jnp.concatenate([u_ref[sb, cb0 + i] for i in range(tiles_per_mm)], axis=0)

    for sb, cb0, rows in mm_blocks():
        st = jnp.dot(load_u(sb, cb0), bs_ref[...], preferred_element_type=F32)
        s_scr[0, rows, :] = st[:, :LANES]
        s_scr[1, rows, :] = st[:, LANES:]

    def chunk_rows(sb, ch):
        return pl.ds(pl.multiple_of(sb * nch * SEQ_TILE + ch * SEQ_TILE, SEQ_TILE), SEQ_TILE)

    def load(ref, part, ch):
        return jnp.concatenate([ref[part, chunk_rows(sb, ch), :] for sb in range(nsb)], axis=0)

    def store(ref, part, ch, val):
        for sb in range(nsb):
            ref[part, chunk_rows(sb, ch), :] = val[sb * SEQ_TILE:(sb + 1) * SEQ_TILE, :]

    are = jnp.broadcast_to(are_ref[...], (seq_rows, LANES))
    aim = jnp.broadcast_to(aim_ref[...], (seq_rows, LANES))
    is_fwd = lax.broadcasted_iota(jnp.int32, (seq_rows, LANES), 1) < SSM_STATE

    def step(i, carry):
        xr, xi = carry
        j = nch - 1 - i
        store(xf_scr, 0, i, xr)
        store(xf_scr, 1, i, xi)
        store(xb_scr, 0, j, xr)
        store(xb_scr, 1, j, xi)
        s_re = jnp.where(is_fwd, load(s_scr, 0, i), load(s_scr, 0, j))
        s_im = jnp.where(is_fwd, load(s_scr, 1, i), load(s_scr, 1, j))
        return (are * xr - aim * xi + s_re, are * xi + aim * xr + s_im)

    zero = jnp.zeros((seq_rows, LANES), F32)
    lax.fori_loop(0, nch, step, (zero, zero), unroll=4)

    fwd_lane = lax.broadcasted_iota(jnp.int32, (MM_ROWS, LANES), 1) < SSM_STATE
    for sb, cb0, rows in mm_blocks():
        xin = jnp.concatenate(
            [jnp.where(fwd_lane, xf_scr[c, rows, :], xb_scr[c, rows, :]) for c in range(2)], axis=1).astype(BF16)
        y = (jnp.dot(load_u(sb, cb0), toep_ref[...], preferred_element_type=F32)
             + jnp.dot(xin, cs_ref[...], preferred_element_type=F32))
        for i in range(tiles_per_mm):
            y_ref[sb, cb0 + i] = y[i * LANES:(i + 1) * LANES, :].astype(BF16)


def _ssm_call(up, ops):
    toep, bs, cs, are, aim = ops
    _, nsb, ncb, _, _ = up.shape
    n_rows = nsb * ncb * LANES
    assert ncb % (MM_ROWS // LANES) == 0
    data_spec = pl.BlockSpec((None, nsb, ncb, LANES, CHUNK_W), lambda g: (g, 0, 0, 0, 0))
    grp = lambda *shape: pl.BlockSpec((None,) + shape, lambda g: (g, 0, 0))
    return pl.pallas_call(
        functools.partial(_ssm_kernel, nsb=nsb, ncb=ncb),
        grid=(N_GROUPS,),
        in_specs=[data_spec, grp(CHUNK_W, CHUNK_W), grp(CHUNK_W, STATE_W), grp(STATE_W, CHUNK_W),
                  grp(1, LANES), grp(1, LANES)],
        out_specs=data_spec,
        out_shape=jax.ShapeDtypeStruct(up.shape, BF16),
        scratch_shapes=[pltpu.VMEM((2, n_rows, LANES), F32)] * 3,
        compiler_params=pltpu.CompilerParams(
            dimension_semantics=("arbitrary",), vmem_limit_bytes=VMEM_LIMIT),
        name="ssm",
    )(up, toep, bs, cs, are, aim)


def _attn_kernel(sink_ref, q_ref, kd_ref, vd_ref, bias_ref, o_ref, ka_scr, kb_scr, wa_scr, wb_scr, *, seq_len):
    jb = pl.program_id(1)
    nb = seq_len // BLOCK

    @pl.when(jb == 0)
    def _():
        low = lax.broadcasted_iota(jnp.int32, (BLOCK, LANES), 1) < HEAD_DIM
        zeros = jnp.zeros((HEAD_DIM, BLOCK), BF16)
        ind_a = jnp.where(lax.broadcasted_iota(jnp.int32, (SUM_ROWS, BLOCK), 0) < SUM_ROWS // 2, 1.0, 0.0)
        ind_b = (1.0 - ind_a).astype(BF16)
        ind_a = ind_a.astype(BF16)

        def per_block(r, carry):
            rows = pl.ds(pl.multiple_of(r * BLOCK, BLOCK), BLOCK)
            for kh in range(N_KV_HEADS):
                kd = kd_ref[rows, pl.ds(kh * LANES, LANES)]
                ka_scr[kh, rows, :] = jnp.where(low, kd, jnp.zeros_like(kd))
                kb_scr[kh, rows, :] = jnp.where(low, jnp.zeros_like(kd), kd)
                vt = vd_ref[rows, pl.ds(kh * LANES, LANES)].T
                wa_scr[kh, r] = jnp.concatenate([vt[:HEAD_DIM], zeros, ind_a], axis=0)
                wb_scr[kh, r] = jnp.concatenate([zeros, vt[HEAD_DIM:], ind_b], axis=0)
            return carry

        lax.fori_loop(0, nb, per_block, 0)

    col_tile0 = lax.broadcasted_iota(jnp.int32, (1, 2 * LANES), 1) < LANES
    row_slot0 = lax.broadcasted_iota(jnp.int32, (BLOCK, 2 * LANES), 0) < HEAD_DIM
    def query_block(sb):
        n = jb * (Q_ROWS // BLOCK) + sb
        kb0 = jnp.clip(n - 1, 0, nb - KEYS // BLOCK)
        ks = pl.multiple_of(kb0 * BLOCK, BLOCK)
        var = jnp.where(n == 0, 0, jnp.where(n == nb - 1, 2, 1))
        qrows = pl.ds(pl.multiple_of(sb * BLOCK, BLOCK), BLOCK)
        for kh in range(N_KV_HEADS):
            kcat = jnp.concatenate([ka_scr[kh, pl.ds(ks, KEYS), :], kb_scr[kh, pl.ds(ks, KEYS), :]], axis=0)
            q2 = jnp.concatenate([q_ref[qrows, pl.ds((2 * kh + jj) * LANES, LANES)] for jj in range(2)], axis=0)
            t = lax.dot_general(kcat, q2, (((1,), (1,)), ((), ())), preferred_element_type=F32)
            t = t + bias_ref[var, kh]
            ps, sink_terms = [], []
            for hh in range(2):
                th = t[hh * KEYS:(hh + 1) * KEYS]
                sink = jnp.where(col_tile0, sink_ref[4 * kh + hh], sink_ref[4 * kh + 2 + hh])
                m = jnp.maximum(jnp.max(th, axis=0, keepdims=True), sink)
                ps.append(jnp.exp2(th - m).astype(BF16))
                sink_terms.append(jnp.exp2(sink - m))
            w = jnp.concatenate([wa_scr[kh, kb0 + i] for i in range(KEYS // BLOCK)]
                                + [wb_scr[kh, kb0 + i] for i in range(KEYS // BLOCK)], axis=1)
            o2 = jnp.dot(w, jnp.concatenate(ps, axis=0), preferred_element_type=F32)
            denom = o2[LANES:] + jnp.where(row_slot0, sink_terms[0], sink_terms[1])
            on = o2[:LANES] / denom
            for jj in range(2):
                o_ref[qrows, pl.ds((2 * kh + jj) * LANES, LANES)] = (
                    on[:, jj * LANES:(jj + 1) * LANES].T.astype(BF16))

    def query_blocks(i, carry):
        for sb in range(BLOCKS_PER_ITER):
            query_block(i * BLOCKS_PER_ITER + sb)
        return carry

    lax.fori_loop(0, Q_ROWS // (BLOCKS_PER_ITER * BLOCK), query_blocks, 0)


def _attn_call(q, kd, vd, bias, sink, batch, seq_len):
    assert seq_len % Q_ROWS == 0 and seq_len >= KEYS
    nq = seq_len // Q_ROWS
    nb = seq_len // BLOCK
    return pl.pallas_call(
        functools.partial(_attn_kernel, seq_len=seq_len),
        grid_spec=pltpu.PrefetchScalarGridSpec(
            num_scalar_prefetch=1,
            grid=(batch, nq),
            in_specs=[
                pl.BlockSpec((Q_ROWS, D_ATTN), lambda b, j, s: (b * nq + j, 0)),
                pl.BlockSpec((seq_len, 2 * LANES), lambda b, j, s: (b, 0)),
                pl.BlockSpec((seq_len, 2 * LANES), lambda b, j, s: (b, 0)),
                pl.BlockSpec((3, N_KV_HEADS, 2 * KEYS, 2 * LANES), lambda b, j, s: (0, 0, 0, 0)),
            ],
            out_specs=pl.BlockSpec((Q_ROWS, D_ATTN), lambda b, j, s: (b * nq + j, 0)),
            scratch_shapes=[
                pltpu.VMEM((N_KV_HEADS, seq_len, LANES), BF16),
                pltpu.VMEM((N_KV_HEADS, seq_len, LANES), BF16),
                pltpu.VMEM((N_KV_HEADS, nb, LANES + SUM_ROWS, BLOCK), BF16),
                pltpu.VMEM((N_KV_HEADS, nb, LANES + SUM_ROWS, BLOCK), BF16),
            ],
        ),
        out_shape=jax.ShapeDtypeStruct((batch * seq_len, D_ATTN), BF16),
        compiler_params=pltpu.CompilerParams(
            dimension_semantics=("arbitrary", "arbitrary"), vmem_limit_bytes=VMEM_LIMIT),
        name="attn",
    )(sink, q, kd, vd, bias)


def _gelu_tanh(x):
    c = math.sqrt(2.0 / math.pi)
    half = 0.5 * x
    return half + half * jnp.tanh(x * (c + (c * 0.044715) * (x * x)))


def _sigmoid(x):
    return 0.5 * jnp.tanh(0.5 * x) + 0.5


def _silu(x):
    return x * _sigmoid(x)


def _post_kernel(x_ref, yp_ref, zs_ref, ao_ref, za_ref, wglu_ref, bglu_ref, snw_ref, anw_ref,
                 wout_ref, fnw_ref, o_ref, y_scr, b_scr, t_scr):
    s = pl.program_id(2)

    @pl.when(s == 0)
    def _():
        for g in range(N_GROUPS):
            j, g3 = divmod(g, GROUPS_PER_LANE_TILE)
            for th in range(CHUNK_W // LANES):
                tt = yp_ref[g, :, pl.ds(th * LANES, LANES)].T
                for t3 in range(8):
                    t_scr[th * 8 + t3, j, pl.ds(g3 * SSM_GROUP, SSM_GROUP), :] = (
                        tt[t3 * SSM_GROUP:(t3 + 1) * SSM_GROUP, :])
        for t in range(CHUNK):
            for j in range(LANE_TILES):
                b_scr[t, :, pl.ds(j * LANES, LANES)] = t_scr[t, j].T
        perm = _pair_perm(t_major_out=False)
        for cp in range(TILE_CHUNKS // 2):
            a = jnp.concatenate(
                [b_scr[t, pl.ds(cp * 2 * SEQ_TILE, 2 * SEQ_TILE), :] for t in range(CHUNK)], axis=0)
            ys = jnp.dot(perm, a, preferred_element_type=F32)
            for c2 in range(2):
                for sq in range(SEQ_TILE):
                    r0 = (c2 * SEQ_TILE + sq) * CHUNK
                    y_scr[pl.ds(sq * TOK_TILE + (2 * cp + c2) * CHUNK, CHUNK), :] = ys[r0:r0 + CHUNK, :]

    def get(ref):
        return ref[...].reshape(SUB, ref.shape[-1]).astype(F32)

    y = y_scr[pl.ds(pl.multiple_of(s * SUB, SUB), SUB), :]
    g = _gelu_tanh(y)
    gate = _sigmoid(jnp.dot(g.astype(BF16), wglu_ref[...], preferred_element_type=F32) + bglu_ref[...])
    n_ssm = _rms(g * gate, snw_ref[...]) * _silu(get(zs_ref))
    n_attn = _rms(get(ao_ref), anw_ref[...]) * _silu(get(za_ref))
    mixed = jnp.concatenate([n_ssm, n_attn], axis=1).astype(BF16)
    res = get(x_ref) + jnp.dot(mixed, wout_ref[...], preferred_element_type=F32)
    o_ref[...] = _rms(res, fnw_ref[...]).reshape(o_ref.shape)


def _post_call(x, yp, zs, ao, za, w_glu, b_glu, ssm_norm_w, attn_norm_w, w_out, final_norm_w):
    batch, seq_len, _ = x.shape
    nsb, ncb = batch // SEQ_TILE, seq_len // TOK_TILE
    return pl.pallas_call(
        _post_kernel,
        grid=(nsb, ncb, N_SUB),
        in_specs=[
            _tile_specs(D_MODEL), _CHUNK_TILE_SPEC, _tile_specs(D_SSM), _tile_specs(D_ATTN), _tile_specs(D_ATTN),
            _const_spec((D_SSM, D_SSM)), _const_spec((1, D_SSM)), _const_spec((1, D_SSM)),
            _const_spec((1, D_ATTN)), _const_spec((D_MODEL, D_MODEL)), _const_spec((1, D_MODEL)),
        ],
        out_specs=_tile_specs(D_MODEL),
        out_shape=jax.ShapeDtypeStruct((batch, seq_len, D_MODEL), F32),
        scratch_shapes=[
            pltpu.VMEM((TILE_ROWS, D_SSM), F32),
            pltpu.VMEM((CHUNK, LANES, D_SSM), BF16),
            pltpu.VMEM((CHUNK, LANE_TILES, LANES, LANES), BF16),
        ],
        compiler_params=pltpu.CompilerParams(
            dimension_semantics=("arbitrary", "arbitrary", "arbitrary"), vmem_limit_bytes=VMEM_LIMIT),
        name="post",
    )(x, yp, zs, ao, za, w_glu, b_glu, ssm_norm_w, attn_norm_w, w_out, final_norm_w)


def _cmul(ar, ai, br, bi):
    return ar * br - ai * bi, ar * bi + ai * br


def _split_bf16(x):
    hi = x.astype(BF16)
    return hi, (x - hi.astype(F32)).astype(BF16)


def _dot_split(a, b):
    a_hi, a_lo = _split_bf16(a)
    b_hi, b_lo = _split_bf16(b)
    dot = functools.partial(jnp.dot, preferred_element_type=F32)
    return dot(a_hi, b_hi) + (dot(a_hi, b_lo) + dot(a_lo, b_hi))


def _ops_kernel(row_ref, col_ref, bt_ref, ct_ref, dsk_ref, toep_ref, bs_ref, cs_ref, a_ref):
    def discretise(lr, li, ls):
        dt = jnp.exp(ls)
        mag = jnp.exp(lr * dt)
        return mag * jnp.cos(li * dt), mag * jnp.sin(li * dt)

    lr, li = row_ref[0], row_ref[1]
    lb_re, lb_im = discretise(lr, li, row_ref[2])
    den = lr * lr + li * li
    nr = lb_re - 1.0
    coef_re = (nr * lr + lb_im * li) / den
    coef_im = (lb_im * lr - nr * li) / den
    bb_re, bb_im = _cmul(coef_re, coef_im, bt_ref[0], bt_ref[1])
    bb_re = jnp.concatenate([bb_re] * CHUNK, axis=0)
    bb_im = jnp.concatenate([bb_im] * CHUNK, axis=0)

    row_t = lax.broadcasted_iota(jnp.int32, (CHUNK_W, LANES), 0) // SSM_GROUP
    fwd_lane = lax.broadcasted_iota(jnp.int32, (CHUNK_W, LANES), 1) < SSM_STATE
    col_t = lax.broadcasted_iota(jnp.int32, (LANES, CHUNK_W), 1) // SSM_GROUP
    fwd_row = lax.broadcasted_iota(jnp.int32, (LANES, CHUNK_W), 0) < SSM_STATE
    ct_re, ct_im = ct_ref[0], ct_ref[1]
    w_re = jnp.zeros((CHUNK_W, LANES), F32)
    w_im = jnp.zeros((CHUNK_W, LANES), F32)
    toep = jnp.zeros((CHUNK_W, CHUNK_W), F32)
    for i in range(CHUNK):
        t_now = jnp.where(fwd_lane, i, CHUNK - 1 - i)
        w_re, w_im = _cmul(w_re, w_im, lb_re, lb_im)
        w_re = w_re + jnp.where(row_t == t_now, bb_re, 0.0)
        w_im = w_im + jnp.where(row_t == t_now, bb_im, 0.0)
        out_now = col_t == jnp.where(fwd_row, i, CHUNK - 1 - i)
        rhs = jnp.concatenate([jnp.where(out_now, ct_re, 0.0), jnp.where(out_now, -ct_im, 0.0)], axis=0)
        toep = toep + _dot_split(jnp.concatenate([w_re, w_im], axis=1), rhs)
    r = lax.broadcasted_iota(jnp.int32, (CHUNK_W, CHUNK_W), 0)
    c = lax.broadcasted_iota(jnp.int32, (CHUNK_W, CHUNK_W), 1)
    toep_ref[...] = (toep + jnp.where(r == c, dsk_ref[...], 0.0)).astype(BF16)
    bs_ref[...] = jnp.concatenate([w_re, w_im], axis=1).astype(BF16)

    pr, pi = discretise(col_ref[0], col_ref[1], col_ref[2])
    k = jnp.where(fwd_row, col_t + 1, CHUNK - col_t)
    acc_re = jnp.ones((LANES, CHUNK_W), F32)
    acc_im = jnp.zeros((LANES, CHUNK_W), F32)
    for bit in range(CHUNK.bit_length()):
        nre, nim = _cmul(acc_re, acc_im, pr, pi)
        take = (k & (1 << bit)) != 0
        acc_re = jnp.where(take, nre, acc_re)
        acc_im = jnp.where(take, nim, acc_im)
        pr, pi = _cmul(pr, pi, pr, pi)
    cp_re, cp_im = _cmul(ct_re, ct_im, acc_re, acc_im)
    cs_ref[...] = jnp.concatenate([cp_re, -cp_im], axis=0).astype(BF16)

    a_re, a_im = lb_re, lb_im
    for _ in range(CHUNK.bit_length() - 1):
        a_re, a_im = _cmul(a_re, a_im, a_re, a_im)
    a_ref[0] = a_re
    a_ref[1] = a_im


def _ssm_operators(lam_re, lam_im, log_step, b_re, b_im, c_re, c_im, d_skip):
    g_first = lambda a: jnp.transpose(a.astype(F32), (1, 0, 2)).reshape(N_GROUPS, 2 * SSM_STATE)
    params = jnp.stack([g_first(lam_re), g_first(lam_im),
                        jnp.repeat(log_step.astype(F32).T, SSM_STATE, axis=1)], axis=1)
    b_t = lambda b: jnp.transpose(b.astype(F32), (1, 3, 0, 2)).reshape(N_GROUPS, SSM_GROUP, 2 * SSM_STATE)
    c_t = lambda c: jnp.tile(
        jnp.transpose(c.astype(F32), (1, 0, 3, 2)).reshape(N_GROUPS, 2 * SSM_STATE, SSM_GROUP), (1, 1, CHUNK))
    bt = jnp.stack([b_t(b_re), b_t(b_im)], axis=1)
    ct = jnp.stack([c_t(c_re), c_t(c_im)], axis=1)
    dsk = jnp.tile(d_skip.astype(F32), (1, CHUNK))[:, None, :]
    grp = lambda *shape: pl.BlockSpec((None,) + shape, lambda g: (g,) + (0,) * len(shape))
    mat = jax.ShapeDtypeStruct((N_GROUPS, CHUNK_W, CHUNK_W), BF16)
    toep, bs, cs, a = pl.pallas_call(
        _ops_kernel,
        grid=(N_GROUPS,),
        in_specs=[grp(3, 1, LANES), grp(3, LANES, 1), grp(2, SSM_GROUP, LANES), grp(2, LANES, CHUNK_W),
                  grp(1, CHUNK_W)],
        out_specs=(grp(CHUNK_W, CHUNK_W), grp(CHUNK_W, STATE_W), grp(STATE_W, CHUNK_W), grp(2, 1, LANES)),
        out_shape=(mat, mat, mat, jax.ShapeDtypeStruct((N_GROUPS, 2, 1, LANES), F32)),
        compiler_params=pltpu.CompilerParams(dimension_semantics=("arbitrary",)),
        name="s5_ops",
    )(params[:, :, None, :], params[:, :, :, None], bt, ct, dsk)
    return toep, bs, cs, a[:, 0], a[:, 1]


def _t5_buckets_np(rel):
    half = N_BUCKETS // 2
    max_exact = half // 2
    ret = np.where(rel > 0, half, 0)
    n = np.abs(rel)
    nf = np.maximum(n, 1).astype(np.float64)
    large = max_exact + (np.log(nf / max_exact) / math.log(MAX_DISTANCE / max_exact)
                         * (half - max_exact)).astype(np.int32)
    large = np.minimum(large, half - 1)
    return ret + np.where(n < max_exact, n, large)


def _attn_tables(rel_bias):
    offsets = (0, -BLOCK, -2 * BLOCK)
    period = 4 * BLOCK
    dist = np.arange(period)
    dist = np.where(dist < KEYS, dist, dist - period)
    rel = np.stack([dist + off for off in offsets])
    per_dist = rel_bias.astype(F32)[_t5_buckets_np(rel)] * LOG2_E
    per_dist = jnp.where((np.abs(rel) <= WINDOW)[:, :, None], per_dist, NEG_INF)
    per_dist = jnp.transpose(per_dist, (0, 2, 1))
    neg = jnp.concatenate([per_dist[:, :, :1], per_dist[:, :, :0:-1]], axis=-1)
    skew = jnp.tile(neg, (1, 1, KEYS))[:, :, :KEYS * (period - 1)]
    bias = skew.reshape(3, N_HEADS, KEYS, period - 1)[:, :, :, :BLOCK]
    bias = bias.reshape(3, N_KV_HEADS, 2, 2, KEYS, BLOCK)
    bias = jnp.concatenate([bias[:, :, 0], bias[:, :, 1]], axis=-1)
    return bias.reshape(3, N_KV_HEADS, 2 * KEYS, 2 * LANES)


def _pack_w_in(w):
    u, zs, q, k, v, za = jnp.split(w, (512, 1024, 1536, 1664, 1792), axis=1)
    dup = lambda t: jnp.concatenate([t[:, :HEAD_DIM], t[:, :HEAD_DIM], t[:, HEAD_DIM:], t[:, HEAD_DIM:]], axis=1)
    return jnp.concatenate([u, zs, q, dup(k), dup(v), za], axis=1).astype(BF16)


def kernel(x_prompt, x_sample, norm_w, w_in, lam_re, lam_im, log_step, b_re, b_im, c_re, c_im, d_skip,
           w_glu, b_glu, ssm_norm_w, sink, attn_norm_w, w_out, rel_bias, final_norm_w):
    assert norm_w.shape[0] == 1, "single-layer encoder"
    ops = _ssm_operators(lam_re[0], lam_im[0], log_step[0], b_re[0], b_im[0], c_re[0], c_im[0], d_skip[0])
    bias = _attn_tables(rel_bias)
    w_cat = _pack_w_in(w_in[0])
    w_glu_b = w_glu[0].astype(BF16)
    w_out_b = w_out[0].astype(BF16)
    sink_f = sink[0].astype(F32) * LOG2_E

    outs = []
    for x in (x_prompt, x_sample):
        batch, seq_len, _ = x.shape
        assert batch % SEQ_TILE == 0 and seq_len % TOK_TILE == 0
        n_tok = batch * seq_len
        up, zs, q, kd, vd, za = _proj_call(x, norm_w, w_cat)
        yp = _ssm_call(up, ops)
        ao = _attn_call(q.reshape(n_tok, D_ATTN), kd.reshape(n_tok, 2 * LANES), vd.reshape(n_tok, 2 * LANES),
                        bias, sink_f, batch, seq_len)
        outs.append(_post_call(x, yp, zs, ao.reshape(batch, seq_len, D_ATTN), za, w_glu_b, b_glu, ssm_norm_w,
                               attn_norm_w, w_out_b, final_norm_w.reshape(1, D_MODEL)))
    return tuple(outs)
```

```python
import functools
import math

import numpy as np
import jax
import jax.numpy as jnp
from jax import lax
from jax.experimental import pallas as pl
from jax.experimental.pallas import tpu as pltpu

F32 = jnp.float32
BF16 = jnp.bfloat16

D_MODEL = 1024
D_SSM = 512
D_ATTN = 512
SSM_GROUP = 16
N_GROUPS = D_SSM // SSM_GROUP
SSM_STATE = 64
HEAD_DIM = 64
N_HEADS = D_ATTN // HEAD_DIM
N_KV_HEADS = 2
WINDOW = 128
BLOCK = 128
N_BUCKETS = 32
MAX_DISTANCE = 128
RMS_EPS = 1e-6
NEG_INF = -1e30
LOG2_E = math.log2(math.e)

LANES = 128
SUBLANES = 8
CHUNK = 16
CHUNK_W = CHUNK * SSM_GROUP
STATE_W = 4 * SSM_STATE
SEQ_TILE = SUBLANES
TILE_CHUNKS = LANES // SEQ_TILE
TOK_TILE = TILE_CHUNKS * CHUNK
TILE_ROWS = SEQ_TILE * TOK_TILE
SUB_SEQS = 4
SUB = SUB_SEQS * TOK_TILE
N_SUB = SEQ_TILE // SUB_SEQS
LANE_TILES = D_SSM // LANES
GROUPS_PER_LANE_TILE = LANES // SSM_GROUP
KEYS = 3 * BLOCK
Q_ROWS = 1024
BLOCKS_PER_ITER = 4
KV_W = N_KV_HEADS * HEAD_DIM
MM_ROWS = 512
VMEM_LIMIT = 56 * 1024 * 1024

_C_U, _C_ZS, _C_Q, _C_K, _C_V, _C_ZA, _C_END = 0, 512, 1024, 1536, 1664, 1792, 2304


def _rms(x, w):
    return (x * lax.rsqrt(jnp.mean(x * x, axis=-1, keepdims=True) + RMS_EPS)) * w


def _tile_specs(width):
    return pl.BlockSpec((SUB_SEQS, TOK_TILE, width), lambda sb, cb, s: (sb * N_SUB + s, cb, 0))


_CHUNK_TILE_SPEC = pl.BlockSpec((N_GROUPS, None, None, LANES, CHUNK_W), lambda sb, cb, s: (0, sb, cb, 0, 0))


def _const_spec(shape):
    return pl.BlockSpec(shape, lambda sb, cb, s: (0,) * len(shape))


PAIR_ROWS = 2 * SEQ_TILE * CHUNK


def _pair_perm(t_major_out):
    out_row = lax.broadcasted_iota(jnp.int32, (PAIR_ROWS, PAIR_ROWS), 0)
    in_row = lax.broadcasted_iota(jnp.int32, (PAIR_ROWS, PAIR_ROWS), 1)
    tcs, cst = (out_row, in_row) if t_major_out else (in_row, out_row)
    hit = (tcs // (2 * SEQ_TILE) == cst % CHUNK) & (tcs % (2 * SEQ_TILE) == cst // CHUNK)
    return jnp.where(hit, 1.0, 0.0).astype(BF16)


def _proj_kernel(x_ref, nw_ref, w_ref, up_ref, zs_ref, q_ref, k_ref, v_ref, za_ref, u_scr, b_scr, t_scr):
    s = pl.program_id(2)
    hb = _rms(x_ref[...].reshape(SUB, D_MODEL), nw_ref[...]).astype(BF16)

    def mm(lo, hi):
        return jnp.dot(hb, w_ref[:, lo:hi], preferred_element_type=F32)

    def put(ref, val):
        ref[...] = val.astype(BF16).reshape(ref.shape)

    u_scr[pl.ds(pl.multiple_of(s * SUB, SUB), SUB), :] = mm(_C_U, _C_ZS).astype(BF16)
    put(zs_ref, mm(_C_ZS, _C_Q))
    put(q_ref, mm(_C_Q, _C_K) * (HEAD_DIM ** -0.5 * LOG2_E))
    put(k_ref, mm(_C_K, _C_V))
    put(v_ref, mm(_C_V, _C_ZA))
    put(za_ref, mm(_C_ZA, _C_END))

    @pl.when(s == N_SUB - 1)
    def _():
        perm = _pair_perm(t_major_out=True)
        for cp in range(TILE_CHUNKS // 2):
            a = jnp.concatenate(
                [u_scr[pl.ds(sq * TOK_TILE + (2 * cp + c2) * CHUNK, CHUNK), :]
                 for c2 in range(2) for sq in range(SEQ_TILE)], axis=0)
            b_scr[cp] = jnp.dot(perm, a, preferred_element_type=F32).astype(BF16)
        for t in range(CHUNK):
            for j in range(LANE_TILES):
                blk = jnp.concatenate(
                    [b_scr[cp, pl.ds(t * 2 * SEQ_TILE, 2 * SEQ_TILE), pl.ds(j * LANES, LANES)]
                     for cp in range(TILE_CHUNKS // 2)], axis=0)
                t_scr[t, j] = blk.T
        for g in range(N_GROUPS):
            j, g3 = divmod(g, GROUPS_PER_LANE_TILE)
            for th in range(CHUNK_W // LANES):
                rows = jnp.concatenate(
                    [t_scr[th * 8 + t3, j, pl.ds(g3 * SSM_GROUP, SSM_GROUP), :] for t3 in range(8)], axis=0)
                up_ref[g, :, pl.ds(th * LANES, LANES)] = rows.T


def _proj_call(x, norm_w, w_in):
    batch, seq_len, _ = x.shape
    nsb, ncb = batch // SEQ_TILE, seq_len // TOK_TILE
    tok = lambda width: jax.ShapeDtypeStruct((batch, seq_len, width), BF16)
    out_shapes = (
        jax.ShapeDtypeStruct((N_GROUPS, nsb, ncb, LANES, CHUNK_W), BF16),
        tok(D_SSM),
        tok(D_ATTN),
        tok(KV_W),
        tok(KV_W),
        tok(D_ATTN),
    )
    return pl.pallas_call(
        _proj_kernel,
        grid=(nsb, ncb, N_SUB),
        in_specs=[_tile_specs(D_MODEL), _const_spec((1, D_MODEL)), _const_spec((D_MODEL, _C_END))],
        out_specs=(_CHUNK_TILE_SPEC, _tile_specs(D_SSM), _tile_specs(D_ATTN), _tile_specs(KV_W),
                   _tile_specs(KV_W), _tile_specs(D_ATTN)),
        out_shape=out_shapes,
        scratch_shapes=[
            pltpu.VMEM((TILE_ROWS, D_SSM), BF16),
            pltpu.VMEM((TILE_CHUNKS // 2, PAIR_ROWS, D_SSM), BF16),
            pltpu.VMEM((CHUNK, LANE_TILES, LANES, LANES), BF16),
        ],
        compiler_params=pltpu.CompilerParams(
            dimension_semantics=("arbitrary", "arbitrary", "arbitrary"), vmem_limit_bytes=VMEM_LIMIT),
        name="proj",
    )(x, norm_w, w_in)


def _ssm_kernel(u_ref, toep_ref, bs_ref, cs_ref, are_ref, aim_ref, y_ref, s_scr, xf_scr, xb_scr,
                *, nsb, ncb):
    nch = ncb * TILE_CHUNKS
    tiles_per_mm = MM_ROWS // LANES
    seq_rows = nsb * SEQ_TILE

    def mm_blocks():
        for sb in range(nsb):
            for cq in range(ncb // tiles_per_mm):
                yield sb, cq * tiles_per_mm, pl.ds((sb * ncb + cq * tiles_per_mm) * LANES, MM_ROWS)

    def load_u(sb, cb0):
        return jnp.concatenate([u_ref[sb, cb0 + i] for i in range(tiles_per_mm)], axis=0)

    for sb, cb0, rows in mm_blocks():
        st = jnp.dot(load_u(sb, cb0), bs_ref[...], preferred_element_type=F32)
        s_scr[0, rows, :] = st[:, :LANES]
        s_scr[1, rows, :] = st[:, LANES:]

    def chunk_rows(sb, ch):
        return pl.ds(pl.multiple_of(sb * nch * SEQ_TILE + ch * SEQ_TILE, SEQ_TILE), SEQ_TILE)

    def load(ref, part, ch):
        return jnp.concatenate([ref[part, chunk_rows(sb, ch), :] for sb in range(nsb)], axis=0)

    def store(ref, part, ch, val):
        for sb in range(nsb):
            ref[part, chunk_rows(sb, ch), :] = val[sb * SEQ_TILE:(sb + 1) * SEQ_TILE, :]

    are = jnp.broadcast_to(are_ref[...], (seq_rows, LANES))
    aim = jnp.broadcast_to(aim_ref[...], (seq_rows, LANES))
    is_fwd = lax.broadcasted_iota(jnp.int32, (seq_rows, LANES), 1) < SSM_STATE

    def step(i, carry):
        xr, xi = carry
        j = nch - 1 - i
        store(xf_scr, 0, i, xr)
        store(xf_scr, 1, i, xi)
        store(xb_scr, 0, j, xr)
        store(xb_scr, 1, j, xi)
        s_re = jnp.where(is_fwd, load(s_scr, 0, i), load(s_scr, 0, j))
        s_im = jnp.where(is_fwd, load(s_scr, 1, i), load(s_scr, 1, j))
        return (are * xr - aim * xi + s_re, are * xi + aim * xr + s_im)

    zero = jnp.zeros((seq_rows, LANES), F32)
    lax.fori_loop(0, nch, step, (zero, zero), unroll=4)

    fwd_lane = lax.broadcasted_iota(jnp.int32, (MM_ROWS, LANES), 1) < SSM_STATE
    for sb, cb0, rows in mm_blocks():
        xin = jnp.concatenate(
            [jnp.where(fwd_lane, xf_scr[c, rows, :], xb_scr[c, rows, :]) for c in range(2)], axis=1).astype(BF16)
        y = (jnp.dot(load_u(sb, cb0), toep_ref[...], preferred_element_type=F32)
             + jnp.dot(xin, cs_ref[...], preferred_element_type=F32))
        for i in range(tiles_per_mm):
            y_ref[sb, cb0 + i] = y[i * LANES:(i + 1) * LANES, :].astype(BF16)


def _ssm_call(up, ops):
    toep, bs, cs, are, aim = ops
    _, nsb, ncb, _, _ = up.shape
    n_rows = nsb * ncb * LANES
    assert ncb % (MM_ROWS // LANES) == 0
    data_spec = pl.BlockSpec((None, nsb, ncb, LANES, CHUNK_W), lambda g: (g, 0, 0, 0, 0))
    grp = lambda *shape: pl.BlockSpec((None,) + shape, lambda g: (g, 0, 0))
    return pl.pallas_call(
        functools.partial(_ssm_kernel, nsb=nsb, ncb=ncb),
        grid=(N_GROUPS,),
        in_specs=[data_spec, grp(CHUNK_W, CHUNK_W), grp(CHUNK_W, STATE_W), grp(STATE_W, CHUNK_W),
                  grp(1, LANES), grp(1, LANES)],
        out_specs=data_spec,
        out_shape=jax.ShapeDtypeStruct(up.shape, BF16),
        scratch_shapes=[pltpu.VMEM((2, n_rows, LANES), F32)] * 3,
        compiler_params=pltpu.CompilerParams(
            dimension_semantics=("arbitrary",), vmem_limit_bytes=VMEM_LIMIT),
        name="ssm",
    )(up, toep, bs, cs, are, aim)


def _attn_kernel(sink_ref, q_ref, k_ref, v_ref, bias_ref, o_ref, ka_scr, kb_scr, wa_scr, wb_scr, *, seq_len):
    jb = pl.program_id(1)
    nb = seq_len // BLOCK

    @pl.when(jb == 0)
    def _():
        low = lax.broadcasted_iota(jnp.int32, (BLOCK, LANES), 1) < HEAD_DIM
        zero_rows = jnp.zeros((HEAD_DIM, BLOCK), BF16)

        def per_block(r, carry):
            rows = pl.ds(pl.multiple_of(r * BLOCK, BLOCK), BLOCK)
            k = k_ref[rows, :]
            k_swapped = pltpu.roll(k, HEAD_DIM, 1)
            zero = jnp.zeros_like(k)
            ka_scr[0, rows, :] = jnp.where(low, k, zero)
            kb_scr[0, rows, :] = jnp.where(low, zero, k_swapped)
            ka_scr[1, rows, :] = jnp.where(low, k_swapped, zero)
            kb_scr[1, rows, :] = jnp.where(low, zero, k)
            vt = v_ref[rows, :].T
            for kh in range(N_KV_HEADS):
                vt_kh = vt[kh * HEAD_DIM:(kh + 1) * HEAD_DIM]
                wa_scr[kh, r] = jnp.concatenate([vt_kh, zero_rows], axis=0)
                wb_scr[kh, r] = jnp.concatenate([zero_rows, vt_kh], axis=0)
            return carry

        lax.fori_loop(0, nb, per_block, 0)

    col_tile0 = lax.broadcasted_iota(jnp.int32, (1, 2 * LANES), 1) < LANES
    row_slot0 = lax.broadcasted_iota(jnp.int32, (BLOCK, 2 * LANES), 0) < HEAD_DIM
    def query_block(sb):
        n = jb * (Q_ROWS // BLOCK) + sb
        kb0 = jnp.clip(n - 1, 0, nb - KEYS // BLOCK)
        ks = pl.multiple_of(kb0 * BLOCK, BLOCK)
        var = jnp.where(n == 0, 0, jnp.where(n == nb - 1, 2, 1))
        qrows = pl.ds(pl.multiple_of(sb * BLOCK, BLOCK), BLOCK)
        for kh in range(N_KV_HEADS):
            kcat = jnp.concatenate([ka_scr[kh, pl.ds(ks, KEYS), :], kb_scr[kh, pl.ds(ks, KEYS), :]], axis=0)
            q2 = jnp.concatenate([q_ref[qrows, pl.ds((2 * kh + jj) * LANES, LANES)] for jj in range(2)], axis=0)
            t = lax.dot_general(kcat, q2, (((1,), (1,)), ((), ())), preferred_element_type=F32)
            t = t + bias_ref[var, kh]
            ps, invs = [], []
            for hh in range(2):
                th = t[hh * KEYS:(hh + 1) * KEYS]
                sink = jnp.where(col_tile0, sink_ref[4 * kh + hh], sink_ref[4 * kh + 2 + hh])
                m = jnp.maximum(jnp.max(th, axis=0, keepdims=True), sink)
                p = jnp.exp2(th - m)
                invs.append(1.0 / (jnp.sum(p, axis=0, keepdims=True) + jnp.exp2(sink - m)))
                ps.append(p.astype(BF16))
            w = jnp.concatenate([wa_scr[kh, kb0 + i] for i in range(KEYS // BLOCK)]
                                + [wb_scr[kh, kb0 + i] for i in range(KEYS // BLOCK)], axis=1)
            o = jnp.dot(w, jnp.concatenate(ps, axis=0), preferred_element_type=F32)
            on = o * jnp.where(row_slot0, invs[0], invs[1])
            for jj in range(2):
                o_ref[qrows, pl.ds((2 * kh + jj) * LANES, LANES)] = (
                    on[:, jj * LANES:(jj + 1) * LANES].T.astype(BF16))

    def query_blocks(i, carry):
        for sb in range(BLOCKS_PER_ITER):
            query_block(i * BLOCKS_PER_ITER + sb)
        return carry

    lax.fori_loop(0, Q_ROWS // (BLOCKS_PER_ITER * BLOCK), query_blocks, 0)


def _attn_call(q, k, v, bias, sink, batch, seq_len):
    assert seq_len % Q_ROWS == 0 and seq_len >= KEYS
    nq = seq_len // Q_ROWS
    nb = seq_len // BLOCK
    return pl.pallas_call(
        functools.partial(_attn_kernel, seq_len=seq_len),
        grid_spec=pltpu.PrefetchScalarGridSpec(
            num_scalar_prefetch=1,
            grid=(batch, nq),
            in_specs=[
                pl.BlockSpec((Q_ROWS, D_ATTN), lambda b, j, s: (b * nq + j, 0)),
                pl.BlockSpec((seq_len, KV_W), lambda b, j, s: (b, 0)),
                pl.BlockSpec((seq_len, KV_W), lambda b, j, s: (b, 0)),
                pl.BlockSpec((3, N_KV_HEADS, 2 * KEYS, 2 * LANES), lambda b, j, s: (0, 0, 0, 0)),
            ],
            out_specs=pl.BlockSpec((Q_ROWS, D_ATTN), lambda b, j, s: (b * nq + j, 0)),
            scratch_shapes=[
                pltpu.VMEM((N_KV_HEADS, seq_len, LANES), BF16),
                pltpu.VMEM((N_KV_HEADS, seq_len, LANES), BF16),
                pltpu.VMEM((N_KV_HEADS, nb, LANES, BLOCK), BF16),
                pltpu.VMEM((N_KV_HEADS, nb, LANES, BLOCK), BF16),
            ],
        ),
        out_shape=jax.ShapeDtypeStruct((batch * seq_len, D_ATTN), BF16),
        compiler_params=pltpu.CompilerParams(
            dimension_semantics=("arbitrary", "arbitrary"), vmem_limit_bytes=VMEM_LIMIT),
        name="attn",
    )(sink, q, k, v, bias)


def _gelu_tanh(x):
    c = math.sqrt(2.0 / math.pi)
    half = 0.5 * x
    return half + half * jnp.tanh(x * (c + (c * 0.044715) * (x * x)))


def _sigmoid(x):
    return 0.5 * jnp.tanh(0.5 * x) + 0.5


def _silu(x):
    return x * _sigmoid(x)


def _post_kernel(x_ref, yp_ref, zs_ref, ao_ref, za_ref, wglu_ref, bglu_ref, snw_ref, anw_ref,
                 wout_ref, fnw_ref, o_ref, y_scr, b_scr, t_scr):
    s = pl.program_id(2)

    @pl.when(s == 0)
    def _():
        for g in range(N_GROUPS):
            j, g3 = divmod(g, GROUPS_PER_LANE_TILE)
            for th in range(CHUNK_W // LANES):
                tt = yp_ref[g, :, pl.ds(th * LANES, LANES)].T
                for t3 in range(8):
                    t_scr[th * 8 + t3, j, pl.ds(g3 * SSM_GROUP, SSM_GROUP), :] = (
                        tt[t3 * SSM_GROUP:(t3 + 1) * SSM_GROUP, :])
        for t in range(CHUNK):
            for j in range(LANE_TILES):
                b_scr[t, :, pl.ds(j * LANES, LANES)] = t_scr[t, j].T
        perm = _pair_perm(t_major_out=False)
        for cp in range(TILE_CHUNKS // 2):
            a = jnp.concatenate(
                [b_scr[t, pl.ds(cp * 2 * SEQ_TILE, 2 * SEQ_TILE), :] for t in range(CHUNK)], axis=0)
            ys = jnp.dot(perm, a, preferred_element_type=F32)
            for c2 in range(2):
                for sq in range(SEQ_TILE):
                    r0 = (c2 * SEQ_TILE + sq) * CHUNK
                    y_scr[pl.ds(sq * TOK_TILE + (2 * cp + c2) * CHUNK, CHUNK), :] = ys[r0:r0 + CHUNK, :]

    def get(ref):
        return ref[...].reshape(SUB, ref.shape[-1]).astype(F32)

    y = y_scr[pl.ds(pl.multiple_of(s * SUB, SUB), SUB), :]
    g = _gelu_tanh(y)
    gate = _sigmoid(jnp.dot(g.astype(BF16), wglu_ref[...], preferred_element_type=F32) + bglu_ref[...])
    n_ssm = _rms(g * gate, snw_ref[...]) * _silu(get(zs_ref))
    n_attn = _rms(get(ao_ref), anw_ref[...]) * _silu(get(za_ref))
    mixed = jnp.concatenate([n_ssm, n_attn], axis=1).astype(BF16)
    res = get(x_ref) + jnp.dot(mixed, wout_ref[...], preferred_element_type=F32)
    o_ref[...] = _rms(res, fnw_ref[...]).reshape(o_ref.shape)


def _post_call(x, yp, zs, ao, za, w_glu, b_glu, ssm_norm_w, attn_norm_w, w_out, final_norm_w):
    batch, seq_len, _ = x.shape
    nsb, ncb = batch // SEQ_TILE, seq_len // TOK_TILE
    return pl.pallas_call(
        _post_kernel,
        grid=(nsb, ncb, N_SUB),
        in_specs=[
            _tile_specs(D_MODEL), _CHUNK_TILE_SPEC, _tile_specs(D_SSM), _tile_specs(D_ATTN), _tile_specs(D_ATTN),
            _const_spec((D_SSM, D_SSM)), _const_spec((1, D_SSM)), _const_spec((1, D_SSM)),
            _const_spec((1, D_ATTN)), _const_spec((D_MODEL, D_MODEL)), _const_spec((1, D_MODEL)),
        ],
        out_specs=_tile_specs(D_MODEL),
        out_shape=jax.ShapeDtypeStruct((batch, seq_len, D_MODEL), F32),
        scratch_shapes=[
            pltpu.VMEM((TILE_ROWS, D_SSM), F32),
            pltpu.VMEM((CHUNK, LANES, D_SSM), BF16),
            pltpu.VMEM((CHUNK, LANE_TILES, LANES, LANES), BF16),
        ],
        compiler_params=pltpu.CompilerParams(
            dimension_semantics=("arbitrary", "arbitrary", "arbitrary"), vmem_limit_bytes=VMEM_LIMIT),
        name="post",
    )(x, yp, zs, ao, za, w_glu, b_glu, ssm_norm_w, attn_norm_w, w_out, final_norm_w)


def _cmul(ar, ai, br, bi):
    return ar * br - ai * bi, ar * bi + ai * br


def _split_bf16(x):
    hi = x.astype(BF16)
    return hi, (x - hi.astype(F32)).astype(BF16)


def _dot_split(a, b):
    a_hi, a_lo = _split_bf16(a)
    b_hi, b_lo = _split_bf16(b)
    dot = functools.partial(jnp.dot, preferred_element_type=F32)
    return dot(a_hi, b_hi) + (dot(a_hi, b_lo) + dot(a_lo, b_hi))


def _ops_kernel(row_ref, col_ref, bt_ref, ct_ref, dsk_ref, toep_ref, bs_ref, cs_ref, a_ref):
    def discretise(lr, li, ls):
        dt = jnp.exp(ls)
        mag = jnp.exp(lr * dt)
        return mag * jnp.cos(li * dt), mag * jnp.sin(li * dt)

    lr, li = row_ref[0], row_ref[1]
    lb_re, lb_im = discretise(lr, li, row_ref[2])
    den = lr * lr + li * li
    nr = lb_re - 1.0
    coef_re = (nr * lr + lb_im * li) / den
    coef_im = (lb_im * lr - nr * li) / den
    bb_re, bb_im = _cmul(coef_re, coef_im, bt_ref[0], bt_ref[1])
    bb_re = jnp.concatenate([bb_re] * CHUNK, axis=0)
    bb_im = jnp.concatenate([bb_im] * CHUNK, axis=0)

    row_t = lax.broadcasted_iota(jnp.int32, (CHUNK_W, LANES), 0) // SSM_GROUP
    fwd_lane = lax.broadcasted_iota(jnp.int32, (CHUNK_W, LANES), 1) < SSM_STATE
    col_t = lax.broadcasted_iota(jnp.int32, (LANES, CHUNK_W), 1) // SSM_GROUP
    fwd_row = lax.broadcasted_iota(jnp.int32, (LANES, CHUNK_W), 0) < SSM_STATE
    ct_re, ct_im = ct_ref[0], ct_ref[1]
    w_re = jnp.zeros((CHUNK_W, LANES), F32)
    w_im = jnp.zeros((CHUNK_W, LANES), F32)
    toep = jnp.zeros((CHUNK_W, CHUNK_W), F32)
    for i in range(CHUNK):
        t_now = jnp.where(fwd_lane, i, CHUNK - 1 - i)
        w_re, w_im = _cmul(w_re, w_im, lb_re, lb_im)
        w_re = w_re + jnp.where(row_t == t_now, bb_re, 0.0)
        w_im = w_im + jnp.where(row_t == t_now, bb_im, 0.0)
        out_now = col_t == jnp.where(fwd_row, i, CHUNK - 1 - i)
        rhs = jnp.concatenate([jnp.where(out_now, ct_re, 0.0), jnp.where(out_now, -ct_im, 0.0)], axis=0)
        toep = toep + _dot_split(jnp.concatenate([w_re, w_im], axis=1), rhs)
    r = lax.broadcasted_iota(jnp.int32, (CHUNK_W, CHUNK_W), 0)
    c = lax.broadcasted_iota(jnp.int32, (CHUNK_W, CHUNK_W), 1)
    toep_ref[...] = (toep + jnp.where(r == c, dsk_ref[...], 0.0)).astype(BF16)
    bs_ref[...] = jnp.concatenate([w_re, w_im], axis=1).astype(BF16)

    pr, pi = discretise(col_ref[0], col_ref[1], col_ref[2])
    k = jnp.where(fwd_row, col_t + 1, CHUNK - col_t)
    acc_re = jnp.ones((LANES, CHUNK_W), F32)
    acc_im = jnp.zeros((LANES, CHUNK_W), F32)
    for bit in range(CHUNK.bit_length()):
        nre, nim = _cmul(acc_re, acc_im, pr, pi)
        take = (k & (1 << bit)) != 0
        acc_re = jnp.where(take, nre, acc_re)
        acc_im = jnp.where(take, nim, acc_im)
        pr, pi = _cmul(pr, pi, pr, pi)
    cp_re, cp_im = _cmul(ct_re, ct_im, acc_re, acc_im)
    cs_ref[...] = jnp.concatenate([cp_re, -cp_im], axis=0).astype(BF16)

    a_re, a_im = lb_re, lb_im
    for _ in range(CHUNK.bit_length() - 1):
        a_re, a_im = _cmul(a_re, a_im, a_re, a_im)
    a_ref[0] = a_re
    a_ref[1] = a_im


def _ssm_operators(lam_re, lam_im, log_step, b_re, b_im, c_re, c_im, d_skip):
    g_first = lambda a: jnp.transpose(a.astype(F32), (1, 0, 2)).reshape(N_GROUPS, 2 * SSM_STATE)
    params = jnp.stack([g_first(lam_re), g_first(lam_im),
                        jnp.repeat(log_step.astype(F32).T, SSM_STATE, axis=1)], axis=1)
    b_t = lambda b: jnp.transpose(b.astype(F32), (1, 3, 0, 2)).reshape(N_GROUPS, SSM_GROUP, 2 * SSM_STATE)
    c_t = lambda c: jnp.tile(
        jnp.transpose(c.astype(F32), (1, 0, 3, 2)).reshape(N_GROUPS, 2 * SSM_STATE, SSM_GROUP), (1, 1, CHUNK))
    bt = jnp.stack([b_t(b_re), b_t(b_im)], axis=1)
    ct = jnp.stack([c_t(c_re), c_t(c_im)], axis=1)
    dsk = jnp.tile(d_skip.astype(F32), (1, CHUNK))[:, None, :]
    grp = lambda *shape: pl.BlockSpec((None,) + shape, lambda g: (g,) + (0,) * len(shape))
    mat = jax.ShapeDtypeStruct((N_GROUPS, CHUNK_W, CHUNK_W), BF16)
    toep, bs, cs, a = pl.pallas_call(
        _ops_kernel,
        grid=(N_GROUPS,),
        in_specs=[grp(3, 1, LANES), grp(3, LANES, 1), grp(2, SSM_GROUP, LANES), grp(2, LANES, CHUNK_W),
                  grp(1, CHUNK_W)],
        out_specs=(grp(CHUNK_W, CHUNK_W), grp(CHUNK_W, STATE_W), grp(STATE_W, CHUNK_W), grp(2, 1, LANES)),
        out_shape=(mat, mat, mat, jax.ShapeDtypeStruct((N_GROUPS, 2, 1, LANES), F32)),
        compiler_params=pltpu.CompilerParams(dimension_semantics=("arbitrary",)),
        name="s5_ops",
    )(params[:, :, None, :], params[:, :, :, None], bt, ct, dsk)
    return toep, bs, cs, a[:, 0], a[:, 1]


def _t5_buckets_np(rel):
    half = N_BUCKETS // 2
    max_exact = half // 2
    ret = np.where(rel > 0, half, 0)
    n = np.abs(rel)
    nf = np.maximum(n, 1).astype(np.float64)
    large = max_exact + (np.log(nf / max_exact) / math.log(MAX_DISTANCE / max_exact)
                         * (half - max_exact)).astype(np.int32)
    large = np.minimum(large, half - 1)
    return ret + np.where(n < max_exact, n, large)


def _attn_tables(rel_bias):
    offsets = (0, -BLOCK, -2 * BLOCK)
    period = 4 * BLOCK
    dist = np.arange(period)
    dist = np.where(dist < KEYS, dist, dist - period)
    rel = np.stack([dist + off for off in offsets])
    per_dist = rel_bias.astype(F32)[_t5_buckets_np(rel)] * LOG2_E
    per_dist = jnp.where((np.abs(rel) <= WINDOW)[:, :, None], per_dist, NEG_INF)
    per_dist = jnp.transpose(per_dist, (0, 2, 1))
    neg = jnp.concatenate([per_dist[:, :, :1], per_dist[:, :, :0:-1]], axis=-1)
    skew = jnp.tile(neg, (1, 1, KEYS))[:, :, :KEYS * (period - 1)]
    bias = skew.reshape(3, N_HEADS, KEYS, period - 1)[:, :, :, :BLOCK]
    bias = bias.reshape(3, N_KV_HEADS, 2, 2, KEYS, BLOCK)
    bias = jnp.concatenate([bias[:, :, 0], bias[:, :, 1]], axis=-1)
    return bias.reshape(3, N_KV_HEADS, 2 * KEYS, 2 * LANES)


def kernel(x_prompt, x_sample, norm_w, w_in, lam_re, lam_im, log_step, b_re, b_im, c_re, c_im, d_skip,
           w_glu, b_glu, ssm_norm_w, sink, attn_norm_w, w_out, rel_bias, final_norm_w):
    assert norm_w.shape[0] == 1, "single-layer encoder"
    ops = _ssm_operators(lam_re[0], lam_im[0], log_step[0], b_re[0], b_im[0], c_re[0], c_im[0], d_skip[0])
    bias = _attn_tables(rel_bias)
    w_in_b = w_in[0].astype(BF16)
    w_glu_b = w_glu[0].astype(BF16)
    w_out_b = w_out[0].astype(BF16)
    sink_f = sink[0].astype(F32) * LOG2_E

    outs = []
    for x in (x_prompt, x_sample):
        batch, seq_len, _ = x.shape
        assert batch % SEQ_TILE == 0 and seq_len % TOK_TILE == 0
        n_tok = batch * seq_len
        up, zs, q, k, v, za = _proj_call(x, norm_w, w_in_b)
        yp = _ssm_call(up, ops)
        ao = _attn_call(q.reshape(n_tok, D_ATTN), k.reshape(n_tok, KV_W), v.reshape(n_tok, KV_W),
                        bias, sink_f, batch, seq_len)
        outs.append(_post_call(x, yp, zs, ao.reshape(batch, seq_len, D_ATTN), za, w_glu_b, b_glu, ssm_norm_w,
                               attn_norm_w, w_out_b, final_norm_w.reshape(1, D_MODEL)))
    return tuple(outs)
```

```python
import functools
import math

import numpy as np
import jax
import jax.numpy as jnp
from jax import lax
from jax.experimental import pallas as pl
from jax.experimental.pallas import tpu as pltpu

F32 = jnp.float32
BF16 = jnp.bfloat16

D_MODEL = 1024
D_SSM = 512
D_ATTN = 512
SSM_GROUP = 16
N_GROUPS = D_SSM // SSM_GROUP
SSM_STATE = 64
HEAD_DIM = 64
N_HEADS = D_ATTN // HEAD_DIM
N_KV_HEADS = 2
WINDOW = 128
BLOCK = 128
N_BUCKETS = 32
MAX_DISTANCE = 128
RMS_EPS = 1e-6
NEG_INF = -1e30
LOG2_E = math.log2(math.e)

LANES = 128
SUBLANES = 8
CHUNK = 16
CHUNK_W = CHUNK * SSM_GROUP
STATE_W = 4 * SSM_STATE
SEQ_TILE = SUBLANES
TILE_CHUNKS = LANES // SEQ_TILE
TOK_TILE = TILE_CHUNKS * CHUNK
TILE_ROWS = SEQ_TILE * TOK_TILE
SUB_SEQS = 4
SUB = SUB_SEQS * TOK_TILE
N_SUB = SEQ_TILE // SUB_SEQS
LANE_TILES = D_SSM // LANES
GROUPS_PER_LANE_TILE = LANES // SSM_GROUP
KEYS = 3 * BLOCK
Q_ROWS = 1024
BLOCKS_PER_ITER = 4
KV_W = N_KV_HEADS * HEAD_DIM
MM_ROWS = 512
VMEM_LIMIT = 56 * 1024 * 1024

_C_U, _C_ZS, _C_Q, _C_K, _C_V, _C_ZA, _C_END = 0, 512, 1024, 1536, 1664, 1792, 2304


def _rms(x, w):
    return (x * lax.rsqrt(jnp.mean(x * x, axis=-1, keepdims=True) + RMS_EPS)) * w


def _tile_specs(width):
    return pl.BlockSpec((SUB_SEQS, TOK_TILE, width), lambda sb, cb, s: (sb * N_SUB + s, cb, 0))


_CHUNK_TILE_SPEC = pl.BlockSpec((N_GROUPS, None, None, LANES, CHUNK_W), lambda sb, cb, s: (0, sb, cb, 0, 0))


def _const_spec(shape):
    return pl.BlockSpec(shape, lambda sb, cb, s: (0,) * len(shape))


PAIR_ROWS = 2 * SEQ_TILE * CHUNK


def _pair_perm(t_major_out):
    out_row = lax.broadcasted_iota(jnp.int32, (PAIR_ROWS, PAIR_ROWS), 0)
    in_row = lax.broadcasted_iota(jnp.int32, (PAIR_ROWS, PAIR_ROWS), 1)
    tcs, cst = (out_row, in_row) if t_major_out else (in_row, out_row)
    hit = (tcs // (2 * SEQ_TILE) == cst % CHUNK) & (tcs % (2 * SEQ_TILE) == cst // CHUNK)
    return jnp.where(hit, 1.0, 0.0).astype(BF16)


def _proj_kernel(x_ref, nw_ref, w_ref, up_ref, zs_ref, q_ref, k_ref, v_ref, za_ref, u_scr, b_scr, t_scr):
    s = pl.program_id(2)
    hb = _rms(x_ref[...].reshape(SUB, D_MODEL), nw_ref[...]).astype(BF16)

    def mm(lo, hi):
        return jnp.dot(hb, w_ref[:, lo:hi], preferred_element_type=F32)

    def put(ref, val):
        ref[...] = val.astype(BF16).reshape(ref.shape)

    uz = mm(_C_U, _C_Q)
    u_scr[pl.ds(pl.multiple_of(s * SUB, SUB), SUB), :] = uz[:, :D_SSM].astype(BF16)
    put(zs_ref, uz[:, D_SSM:])
    qkv = mm(_C_Q, _C_ZA)
    put(q_ref, qkv[:, :D_ATTN] * (HEAD_DIM ** -0.5 * LOG2_E))
    put(k_ref, qkv[:, D_ATTN:D_ATTN + KV_W])
    put(v_ref, qkv[:, D_ATTN + KV_W:])
    put(za_ref, mm(_C_ZA, _C_END))

    @pl.when(s == N_SUB - 1)
    def _():
        perm = _pair_perm(t_major_out=True)
        for cp in range(TILE_CHUNKS // 2):
            a = jnp.concatenate(
                [u_scr[pl.ds(sq * TOK_TILE + (2 * cp + c2) * CHUNK, CHUNK), :]
                 for c2 in range(2) for sq in range(SEQ_TILE)], axis=0)
            b_scr[cp] = jnp.dot(perm, a, preferred_element_type=F32).astype(BF16)
        for t in range(CHUNK):
            for j in range(LANE_TILES):
                blk = jnp.concatenate(
                    [b_scr[cp, pl.ds(t * 2 * SEQ_TILE, 2 * SEQ_TILE), pl.ds(j * LANES, LANES)]
                     for cp in range(TILE_CHUNKS // 2)], axis=0)
                t_scr[t, j] = blk.T
        for g in range(N_GROUPS):
            j, g3 = divmod(g, GROUPS_PER_LANE_TILE)
            for th in range(CHUNK_W // LANES):
                rows = jnp.concatenate(
                    [t_scr[th * 8 + t3, j, pl.ds(g3 * SSM_GROUP, SSM_GROUP), :] for t3 in range(8)], axis=0)
                up_ref[g, :, pl.ds(th * LANES, LANES)] = rows.T


def _proj_call(x, norm_w, w_in):
    batch, seq_len, _ = x.shape
    nsb, ncb = batch // SEQ_TILE, seq_len // TOK_TILE
    tok = lambda width: jax.ShapeDtypeStruct((batch, seq_len, width), BF16)
    out_shapes = (
        jax.ShapeDtypeStruct((N_GROUPS, nsb, ncb, LANES, CHUNK_W), BF16),
        tok(D_SSM),
        tok(D_ATTN),
        tok(KV_W),
        tok(KV_W),
        tok(D_ATTN),
    )
    return pl.pallas_call(
        _proj_kernel,
        grid=(nsb, ncb, N_SUB),
        in_specs=[_tile_specs(D_MODEL), _const_spec((1, D_MODEL)), _const_spec((D_MODEL, _C_END))],
        out_specs=(_CHUNK_TILE_SPEC, _tile_specs(D_SSM), _tile_specs(D_ATTN), _tile_specs(KV_W),
                   _tile_specs(KV_W), _tile_specs(D_ATTN)),
        out_shape=out_shapes,
        scratch_shapes=[
            pltpu.VMEM((TILE_ROWS, D_SSM), BF16),
            pltpu.VMEM((TILE_CHUNKS // 2, PAIR_ROWS, D_SSM), BF16),
            pltpu.VMEM((CHUNK, LANE_TILES, LANES, LANES), BF16),
        ],
        compiler_params=pltpu.CompilerParams(
            dimension_semantics=("arbitrary", "arbitrary", "arbitrary"), vmem_limit_bytes=VMEM_LIMIT),
        name="proj",
    )(x, norm_w, w_in)


def _ssm_kernel(u_ref, toep_ref, bs_ref, cs_ref, are_ref, aim_ref, y_ref, s_scr, xf_scr, xb_scr,
                *, nsb, ncb):
    nch = ncb * TILE_CHUNKS
    tiles_per_mm = MM_ROWS // LANES
    seq_rows = nsb * SEQ_TILE

    def mm_blocks():
        for sb in range(nsb):
            for cq in range(ncb // tiles_per_mm):
                yield sb, cq * tiles_per_mm, pl.ds((sb * ncb + cq * tiles_per_mm) * LANES, MM_ROWS)

    def load_u(sb, cb0):
        return jnp.concatenate([u_ref[sb, cb0 + i] for i in range(tiles_per_mm)], axis=0)

    for sb, cb0, rows in mm_blocks():
        st = jnp.dot(load_u(sb, cb0), bs_ref[...], preferred_element_type=F32)
        s_scr[0, rows, :] = st[:, :LANES]
        s_scr[1, rows, :] = st[:, LANES:]

    def chunk_rows(sb, ch):
        return pl.ds(pl.multiple_of(sb * nch * SEQ_TILE + ch * SEQ_TILE, SEQ_TILE), SEQ_TILE)

    def load(ref, part, ch):
        return jnp.concatenate([ref[part, chunk_rows(sb, ch), :] for sb in range(nsb)], axis=0)

    def store(ref, part, ch, val):
        for sb in range(nsb):
            ref[part, chunk_rows(sb, ch), :] = val[sb * SEQ_TILE:(sb + 1) * SEQ_TILE, :]

    are = jnp.broadcast_to(are_ref[...], (seq_rows, LANES))
    aim = jnp.broadcast_to(aim_ref[...], (seq_rows, LANES))
    is_fwd = lax.broadcasted_iota(jnp.int32, (seq_rows, LANES), 1) < SSM_STATE

    def step(i, carry):
        xr, xi = carry
        j = nch - 1 - i
        store(xf_scr, 0, i, xr)
        store(xf_scr, 1, i, xi)
        store(xb_scr, 0, j, xr)
        store(xb_scr, 1, j, xi)
        s_re = jnp.where(is_fwd, load(s_scr, 0, i), load(s_scr, 0, j))
        s_im = jnp.where(is_fwd, load(s_scr, 1, i), load(s_scr, 1, j))
        return (are * xr - aim * xi + s_re, are * xi + aim * xr + s_im)

    zero = jnp.zeros((seq_rows, LANES), F32)
    lax.fori_loop(0, nch, step, (zero, zero), unroll=4)

    fwd_lane = lax.broadcasted_iota(jnp.int32, (MM_ROWS, LANES), 1) < SSM_STATE
    for sb, cb0, rows in mm_blocks():
        xin = jnp.concatenate(
            [jnp.where(fwd_lane, xf_scr[c, rows, :], xb_scr[c, rows, :]) for c in range(2)], axis=1).astype(BF16)
        y = (jnp.dot(load_u(sb, cb0), toep_ref[...], preferred_element_type=F32)
             + jnp.dot(xin, cs_ref[...], preferred_element_type=F32))
        for i in range(tiles_per_mm):
            y_ref[sb, cb0 + i] = y[i * LANES:(i + 1) * LANES, :].astype(BF16)


def _ssm_call(up, ops):
    toep, bs, cs, are, aim = ops
    _, nsb, ncb, _, _ = up.shape
    n_rows = nsb * ncb * LANES
    assert ncb % (MM_ROWS // LANES) == 0
    data_spec = pl.BlockSpec((None, nsb, ncb, LANES, CHUNK_W), lambda g: (g, 0, 0, 0, 0))
    grp = lambda *shape: pl.BlockSpec((None,) + shape, lambda g: (g, 0, 0))
    return pl.pallas_call(
        functools.partial(_ssm_kernel, nsb=nsb, ncb=ncb),
        grid=(N_GROUPS,),
        in_specs=[data_spec, grp(CHUNK_W, CHUNK_W), grp(CHUNK_W, STATE_W), grp(STATE_W, CHUNK_W),
                  grp(1, LANES), grp(1, LANES)],
        out_specs=data_spec,
        out_shape=jax.ShapeDtypeStruct(up.shape, BF16),
        scratch_shapes=[pltpu.VMEM((2, n_rows, LANES), F32)] * 3,
        compiler_params=pltpu.CompilerParams(
            dimension_semantics=("arbitrary",), vmem_limit_bytes=VMEM_LIMIT),
        name="ssm",
    )(up, toep, bs, cs, are, aim)


def _attn_kernel(sink_ref, q_ref, k_ref, v_ref, bias_ref, o_ref, ka_scr, kb_scr, wa_scr, wb_scr, *, seq_len):
    jb = pl.program_id(1)
    nb = seq_len // BLOCK

    @pl.when(jb == 0)
    def _():
        low = lax.broadcasted_iota(jnp.int32, (BLOCK, LANES), 1) < HEAD_DIM
        zero_rows = jnp.zeros((HEAD_DIM, BLOCK), BF16)
        one_rows = jnp.ones((HEAD_DIM, BLOCK), BF16)

        def per_block(r, carry):
            rows = pl.ds(pl.multiple_of(r * BLOCK, BLOCK), BLOCK)
            k = k_ref[rows, :]
            k_swapped = pltpu.roll(k, HEAD_DIM, 1)
            zero = jnp.zeros_like(k)
            ka_scr[0, rows, :] = jnp.where(low, k, zero)
            kb_scr[0, rows, :] = jnp.where(low, zero, k_swapped)
            ka_scr[1, rows, :] = jnp.where(low, k_swapped, zero)
            kb_scr[1, rows, :] = jnp.where(low, zero, k)
            vt = v_ref[rows, :].T
            for kh in range(N_KV_HEADS):
                vt_kh = vt[kh * HEAD_DIM:(kh + 1) * HEAD_DIM]
                wa_scr[kh, r] = jnp.concatenate([vt_kh, zero_rows, one_rows, zero_rows], axis=0)
                wb_scr[kh, r] = jnp.concatenate([zero_rows, vt_kh, zero_rows, one_rows], axis=0)
            return carry

        lax.fori_loop(0, nb, per_block, 0)

    col_tile0 = lax.broadcasted_iota(jnp.int32, (1, 2 * LANES), 1) < LANES
    row_slot0 = lax.broadcasted_iota(jnp.int32, (BLOCK, 2 * LANES), 0) < HEAD_DIM
    def query_block(sb):
        n = jb * (Q_ROWS // BLOCK) + sb
        kb0 = jnp.clip(n - 1, 0, nb - KEYS // BLOCK)
        ks = pl.multiple_of(kb0 * BLOCK, BLOCK)
        var = jnp.where(n == 0, 0, jnp.where(n == nb - 1, 2, 1))
        qrows = pl.ds(pl.multiple_of(sb * BLOCK, BLOCK), BLOCK)
        for kh in range(N_KV_HEADS):
            kcat = jnp.concatenate([ka_scr[kh, pl.ds(ks, KEYS), :], kb_scr[kh, pl.ds(ks, KEYS), :]], axis=0)
            q2 = jnp.concatenate([q_ref[qrows, pl.ds((2 * kh + jj) * LANES, LANES)] for jj in range(2)], axis=0)
            t = lax.dot_general(kcat, q2, (((1,), (1,)), ((), ())), preferred_element_type=F32)
            t = t + bias_ref[var, kh]
            ps, sink_terms = [], []
            for hh in range(2):
                th = t[hh * KEYS:(hh + 1) * KEYS]
                sink = jnp.where(col_tile0, sink_ref[4 * kh + hh], sink_ref[4 * kh + 2 + hh])
                m = jnp.maximum(jnp.max(th, axis=0, keepdims=True), sink)
                ps.append(jnp.exp2(th - m).astype(BF16))
                sink_terms.append(jnp.exp2(sink - m))
            w = jnp.concatenate([wa_scr[kh, kb0 + i] for i in range(KEYS // BLOCK)]
                                + [wb_scr[kh, kb0 + i] for i in range(KEYS // BLOCK)], axis=1)
            o2 = jnp.dot(w, jnp.concatenate(ps, axis=0), preferred_element_type=F32)
            denom = o2[LANES:] + jnp.where(row_slot0, sink_terms[0], sink_terms[1])
            on = o2[:LANES] / denom
            for jj in range(2):
                o_ref[qrows, pl.ds((2 * kh + jj) * LANES, LANES)] = (
                    on[:, jj * LANES:(jj + 1) * LANES].T.astype(BF16))

    def query_blocks(i, carry):
        for sb in range(BLOCKS_PER_ITER):
            query_block(i * BLOCKS_PER_ITER + sb)
        return carry

    lax.fori_loop(0, Q_ROWS // (BLOCKS_PER_ITER * BLOCK), query_blocks, 0)


def _attn_call(q, k, v, bias, sink, batch, seq_len):
    assert seq_len % Q_ROWS == 0 and seq_len >= KEYS
    nq = seq_len // Q_ROWS
    nb = seq_len // BLOCK
    return pl.pallas_call(
        functools.partial(_attn_kernel, seq_len=seq_len),
        grid_spec=pltpu.PrefetchScalarGridSpec(
            num_scalar_prefetch=1,
            grid=(batch, nq),
            in_specs=[
                pl.BlockSpec((Q_ROWS, D_ATTN), lambda b, j, s: (b * nq + j, 0)),
                pl.BlockSpec((seq_len, KV_W), lambda b, j, s: (b, 0)),
                pl.BlockSpec((seq_len, KV_W), lambda b, j, s: (b, 0)),
                pl.BlockSpec((3, N_KV_HEADS, 2 * KEYS, 2 * LANES), lambda b, j, s: (0, 0, 0, 0)),
            ],
            out_specs=pl.BlockSpec((Q_ROWS, D_ATTN), lambda b, j, s: (b * nq + j, 0)),
            scratch_shapes=[
                pltpu.VMEM((N_KV_HEADS, seq_len, LANES), BF16),
                pltpu.VMEM((N_KV_HEADS, seq_len, LANES), BF16),
                pltpu.VMEM((N_KV_HEADS, nb, 2 * LANES, BLOCK), BF16),
                pltpu.VMEM((N_KV_HEADS, nb, 2 * LANES, BLOCK), BF16),
            ],
        ),
        out_shape=jax.ShapeDtypeStruct((batch * seq_len, D_ATTN), BF16),
        compiler_params=pltpu.CompilerParams(
            dimension_semantics=("arbitrary", "arbitrary"), vmem_limit_bytes=VMEM_LIMIT),
        name="attn",
    )(sink, q, k, v, bias)


def _gelu_tanh(x):
    c = math.sqrt(2.0 / math.pi)
    half = 0.5 * x
    return half + half * jnp.tanh(x * (c + (c * 0.044715) * (x * x)))


def _sigmoid(x):
    return 0.5 * jnp.tanh(0.5 * x) + 0.5


def _silu(x):
    return x * _sigmoid(x)


def _post_kernel(x_ref, yp_ref, zs_ref, ao_ref, za_ref, wglu_ref, bglu_ref, snw_ref, anw_ref,
                 wout_ref, fnw_ref, o_ref, y_scr, b_scr, t_scr):
    s = pl.program_id(2)

    @pl.when(s == 0)
    def _():
        for g in range(N_GROUPS):
            j, g3 = divmod(g, GROUPS_PER_LANE_TILE)
            for th in range(CHUNK_W // LANES):
                tt = yp_ref[g, :, pl.ds(th * LANES, LANES)].T
                for t3 in range(8):
                    t_scr[th * 8 + t3, j, pl.ds(g3 * SSM_GROUP, SSM_GROUP), :] = (
                        tt[t3 * SSM_GROUP:(t3 + 1) * SSM_GROUP, :])
        for t in range(CHUNK):
            for j in range(LANE_TILES):
                b_scr[t, :, pl.ds(j * LANES, LANES)] = t_scr[t, j].T
        perm = _pair_perm(t_major_out=False)
        for cp in range(TILE_CHUNKS // 2):
            a = jnp.concatenate(
                [b_scr[t, pl.ds(cp * 2 * SEQ_TILE, 2 * SEQ_TILE), :] for t in range(CHUNK)], axis=0)
            ys = jnp.dot(perm, a, preferred_element_type=F32)
            for c2 in range(2):
                for sq in range(SEQ_TILE):
                    r0 = (c2 * SEQ_TILE + sq) * CHUNK
                    y_scr[pl.ds(sq * TOK_TILE + (2 * cp + c2) * CHUNK, CHUNK), :] = ys[r0:r0 + CHUNK, :]

    def get(ref):
        return ref[...].reshape(SUB, ref.shape[-1]).astype(F32)

    y = y_scr[pl.ds(pl.multiple_of(s * SUB, SUB), SUB), :]
    g = _gelu_tanh(y)
    gate = _sigmoid(jnp.dot(g.astype(BF16), wglu_ref[...], preferred_element_type=F32) + bglu_ref[...])
    n_ssm = _rms(g * gate, snw_ref[...]) * _silu(get(zs_ref))
    n_attn = _rms(get(ao_ref), anw_ref[...]) * _silu(get(za_ref))
    mixed = jnp.concatenate([n_ssm, n_attn], axis=1).astype(BF16)
    res = get(x_ref) + jnp.dot(mixed, wout_ref[...], preferred_element_type=F32)
    o_ref[...] = _rms(res, fnw_ref[...]).reshape(o_ref.shape)


def _post_call(x, yp, zs, ao, za, w_glu, b_glu, ssm_norm_w, attn_norm_w, w_out, final_norm_w):
    batch, seq_len, _ = x.shape
    nsb, ncb = batch // SEQ_TILE, seq_len // TOK_TILE
    return pl.pallas_call(
        _post_kernel,
        grid=(nsb, ncb, N_SUB),
        in_specs=[
            _tile_specs(D_MODEL), _CHUNK_TILE_SPEC, _tile_specs(D_SSM), _tile_specs(D_ATTN), _tile_specs(D_ATTN),
            _const_spec((D_SSM, D_SSM)), _const_spec((1, D_SSM)), _const_spec((1, D_SSM)),
            _const_spec((1, D_ATTN)), _const_spec((D_MODEL, D_MODEL)), _const_spec((1, D_MODEL)),
        ],
        out_specs=_tile_specs(D_MODEL),
        out_shape=jax.ShapeDtypeStruct((batch, seq_len, D_MODEL), F32),
        scratch_shapes=[
            pltpu.VMEM((TILE_ROWS, D_SSM), F32),
            pltpu.VMEM((CHUNK, LANES, D_SSM), BF16),
            pltpu.VMEM((CHUNK, LANE_TILES, LANES, LANES), BF16),
        ],
        compiler_params=pltpu.CompilerParams(
            dimension_semantics=("arbitrary", "arbitrary", "arbitrary"), vmem_limit_bytes=VMEM_LIMIT),
        name="post",
    )(x, yp, zs, ao, za, w_glu, b_glu, ssm_norm_w, attn_norm_w, w_out, final_norm_w)


def _cmul(ar, ai, br, bi):
    return ar * br - ai * bi, ar * bi + ai * br


def _split_bf16(x):
    hi = x.astype(BF16)
    return hi, (x - hi.astype(F32)).astype(BF16)


def _dot_split(a, b):
    a_hi, a_lo = _split_bf16(a)
    b_hi, b_lo = _split_bf16(b)
    dot = functools.partial(jnp.dot, preferred_element_type=F32)
    return dot(a_hi, b_hi) + (dot(a_hi, b_lo) + dot(a_lo, b_hi))


def _ops_kernel(row_ref, col_ref, bt_ref, ct_ref, dsk_ref, toep_ref, bs_ref, cs_ref, a_ref):
    def discretise(lr, li, ls):
        dt = jnp.exp(ls)
        mag = jnp.exp(lr * dt)
        return mag * jnp.cos(li * dt), mag * jnp.sin(li * dt)

    lr, li = row_ref[0], row_ref[1]
    lb_re, lb_im = discretise(lr, li, row_ref[2])
    den = lr * lr + li * li
    nr = lb_re - 1.0
    coef_re = (nr * lr + lb_im * li) / den
    coef_im = (lb_im * lr - nr * li) / den
    bb_re, bb_im = _cmul(coef_re, coef_im, bt_ref[0], bt_ref[1])
    bb_re = jnp.concatenate([bb_re] * CHUNK, axis=0)
    bb_im = jnp.concatenate([bb_im] * CHUNK, axis=0)

    row_t = lax.broadcasted_iota(jnp.int32, (CHUNK_W, LANES), 0) // SSM_GROUP
    fwd_lane = lax.broadcasted_iota(jnp.int32, (CHUNK_W, LANES), 1) < SSM_STATE
    col_t = lax.broadcasted_iota(jnp.int32, (LANES, CHUNK_W), 1) // SSM_GROUP
    fwd_row = lax.broadcasted_iota(jnp.int32, (LANES, CHUNK_W), 0) < SSM_STATE
    ct_re, ct_im = ct_ref[0], ct_ref[1]
    w_re = jnp.zeros((CHUNK_W, LANES), F32)
    w_im = jnp.zeros((CHUNK_W, LANES), F32)
    toep = jnp.zeros((CHUNK_W, CHUNK_W), F32)
    for i in range(CHUNK):
        t_now = jnp.where(fwd_lane, i, CHUNK - 1 - i)
        w_re, w_im = _cmul(w_re, w_im, lb_re, lb_im)
        w_re = w_re + jnp.where(row_t == t_now, bb_re, 0.0)
        w_im = w_im + jnp.where(row_t == t_now, bb_im, 0.0)
        out_now = col_t == jnp.where(fwd_row, i, CHUNK - 1 - i)
        rhs = jnp.concatenate([jnp.where(out_now, ct_re, 0.0), jnp.where(out_now, -ct_im, 0.0)], axis=0)
        toep = toep + _dot_split(jnp.concatenate([w_re, w_im], axis=1), rhs)
    r = lax.broadcasted_iota(jnp.int32, (CHUNK_W, CHUNK_W), 0)
    c = lax.broadcasted_iota(jnp.int32, (CHUNK_W, CHUNK_W), 1)
    toep_ref[...] = (toep + jnp.where(r == c, dsk_ref[...], 0.0)).astype(BF16)
    bs_ref[...] = jnp.concatenate([w_re, w_im], axis=1).astype(BF16)

    pr, pi = discretise(col_ref[0], col_ref[1], col_ref[2])
    k = jnp.where(fwd_row, col_t + 1, CHUNK - col_t)
    acc_re = jnp.ones((LANES, CHUNK_W), F32)
    acc_im = jnp.zeros((LANES, CHUNK_W), F32)
    for bit in range(CHUNK.bit_length()):
        nre, nim = _cmul(acc_re, acc_im, pr, pi)
        take = (k & (1 << bit)) != 0
        acc_re = jnp.where(take, nre, acc_re)
        acc_im = jnp.where(take, nim, acc_im)
        pr, pi = _cmul(pr, pi, pr, pi)
    cp_re, cp_im = _cmul(ct_re, ct_im, acc_re, acc_im)
    cs_ref[...] = jnp.concatenate([cp_re, -cp_im], axis=0).astype(BF16)

    a_re, a_im = lb_re, lb_im
    for _ in range(CHUNK.bit_length() - 1):
        a_re, a_im = _cmul(a_re, a_im, a_re, a_im)
    a_ref[0] = a_re
    a_ref[1] = a_im


def _ssm_operators(lam_re, lam_im, log_step, b_re, b_im, c_re, c_im, d_skip):
    g_first = lambda a: jnp.transpose(a.astype(F32), (1, 0, 2)).reshape(N_GROUPS, 2 * SSM_STATE)
    params = jnp.stack([g_first(lam_re), g_first(lam_im),
                        jnp.repeat(log_step.astype(F32).T, SSM_STATE, axis=1)], axis=1)
    b_t = lambda b: jnp.transpose(b.astype(F32), (1, 3, 0, 2)).reshape(N_GROUPS, SSM_GROUP, 2 * SSM_STATE)
    c_t = lambda c: jnp.tile(
        jnp.transpose(c.astype(F32), (1, 0, 3, 2)).reshape(N_GROUPS, 2 * SSM_STATE, SSM_GROUP), (1, 1, CHUNK))
    bt = jnp.stack([b_t(b_re), b_t(b_im)], axis=1)
    ct = jnp.stack([c_t(c_re), c_t(c_im)], axis=1)
    dsk = jnp.tile(d_skip.astype(F32), (1, CHUNK))[:, None, :]
    grp = lambda *shape: pl.BlockSpec((None,) + shape, lambda g: (g,) + (0,) * len(shape))
    mat = jax.ShapeDtypeStruct((N_GROUPS, CHUNK_W, CHUNK_W), BF16)
    toep, bs, cs, a = pl.pallas_call(
        _ops_kernel,
        grid=(N_GROUPS,),
        in_specs=[grp(3, 1, LANES), grp(3, LANES, 1), grp(2, SSM_GROUP, LANES), grp(2, LANES, CHUNK_W),
                  grp(1, CHUNK_W)],
        out_specs=(grp(CHUNK_W, CHUNK_W), grp(CHUNK_W, STATE_W), grp(STATE_W, CHUNK_W), grp(2, 1, LANES)),
        out_shape=(mat, mat, mat, jax.ShapeDtypeStruct((N_GROUPS, 2, 1, LANES), F32)),
        compiler_params=pltpu.CompilerParams(dimension_semantics=("arbitrary",)),
        name="s5_ops",
    )(params[:, :, None, :], params[:, :, :, None], bt, ct, dsk)
    return toep, bs, cs, a[:, 0], a[:, 1]


def _t5_buckets_np(rel):
    half = N_BUCKETS // 2
    max_exact = half // 2
    ret = np.where(rel > 0, half, 0)
    n = np.abs(rel)
    nf = np.maximum(n, 1).astype(np.float64)
    large = max_exact + (np.log(nf / max_exact) / math.log(MAX_DISTANCE / max_exact)
                         * (half - max_exact)).astype(np.int32)
    large = np.minimum(large, half - 1)
    return ret + np.where(n < max_exact, n, large)


def _attn_tables(rel_bias):
    offsets = (0, -BLOCK, -2 * BLOCK)
    period = 4 * BLOCK
    dist = np.arange(period)
    dist = np.where(dist < KEYS, dist, dist - period)
    rel = np.stack([dist + off for off in offsets])
    per_dist = rel_bias.astype(F32)[_t5_buckets_np(rel)] * LOG2_E
    per_dist = jnp.where((np.abs(rel) <= WINDOW)[:, :, None], per_dist, NEG_INF)
    per_dist = jnp.transpose(per_dist, (0, 2, 1))
    neg = jnp.concatenate([per_dist[:, :, :1], per_dist[:, :, :0:-1]], axis=-1)
    skew = jnp.tile(neg, (1, 1, KEYS))[:, :, :KEYS * (period - 1)]
    bias = skew.reshape(3, N_HEADS, KEYS, period - 1)[:, :, :, :BLOCK]
    bias = bias.reshape(3, N_KV_HEADS, 2, 2, KEYS, BLOCK)
    bias = jnp.concatenate([bias[:, :, 0], bias[:, :, 1]], axis=-1)
    return bias.reshape(3, N_KV_HEADS, 2 * KEYS, 2 * LANES)


def kernel(x_prompt, x_sample, norm_w, w_in, lam_re, lam_im, log_step, b_re, b_im, c_re, c_im, d_skip,
           w_glu, b_glu, ssm_norm_w, sink, attn_norm_w, w_out, rel_bias, final_norm_w):
    assert norm_w.shape[0] == 1, "single-layer encoder"
    ops = _ssm_operators(lam_re[0], lam_im[0], log_step[0], b_re[0], b_im[0], c_re[0], c_im[0], d_skip[0])
    bias = _attn_tables(rel_bias)
    w_in_b = w_in[0].astype(BF16)
    w_glu_b = w_glu[0].astype(BF16)
    w_out_b = w_out[0].astype(BF16)
    sink_f = sink[0].astype(F32) * LOG2_E

    outs = []
    for x in (x_prompt, x_sample):
        batch, seq_len, _ = x.shape
        assert batch % SEQ_TILE == 0 and seq_len % TOK_TILE == 0
        n_tok = batch * seq_len
        up, zs, q, k, v, za = _proj_call(x, norm_w, w_in_b)
        yp = _ssm_call(up, ops)
        ao = _attn_call(q.reshape(n_tok, D_ATTN), k.reshape(n_tok, KV_W), v.reshape(n_tok, KV_W),
                        bias, sink_f, batch, seq_len)
        outs.append(_post_call(x, yp, zs, ao.reshape(batch, seq_len, D_ATTN), za, w_glu_b, b_glu, ssm_norm_w,
                               attn_norm_w, w_out_b, final_norm_w.reshape(1, D_MODEL)))
    return tuple(outs)
```

```python
import functools
import math

import numpy as np
import jax
import jax.numpy as jnp
from jax import lax
from jax.experimental import pallas as pl
from jax.experimental.pallas import tpu as pltpu

F32 = jnp.float32
BF16 = jnp.bfloat16

D_MODEL = 1024
D_SSM = 512
D_ATTN = 512
SSM_GROUP = 16
N_GROUPS = D_SSM // SSM_GROUP
SSM_STATE = 64
HEAD_DIM = 64
N_HEADS = D_ATTN // HEAD_DIM
N_KV_HEADS = 2
WINDOW = 128
BLOCK = 128
N_BUCKETS = 32
MAX_DISTANCE = 128
RMS_EPS = 1e-6
NEG_INF = -1e30
LOG2_E = math.log2(math.e)

LANES = 128
SUBLANES = 8
CHUNK = 16
CHUNK_W = CHUNK * SSM_GROUP
STATE_W = 4 * SSM_STATE
SEQ_TILE = SUBLANES
TILE_CHUNKS = LANES // SEQ_TILE
TOK_TILE = TILE_CHUNKS * CHUNK
TILE_ROWS = SEQ_TILE * TOK_TILE
SUB_SEQS = 4
SUB = SUB_SEQS * TOK_TILE
N_SUB = SEQ_TILE // SUB_SEQS
LANE_TILES = D_SSM // LANES
GROUPS_PER_LANE_TILE = LANES // SSM_GROUP
KEYS = 3 * BLOCK
Q_ROWS = 1024
BLOCKS_PER_ITER = 4
KV_W = N_KV_HEADS * HEAD_DIM
MM_ROWS = 512
VMEM_LIMIT = 56 * 1024 * 1024

_C_U, _C_ZS, _C_Q, _C_K, _C_V, _C_ZA, _C_END = 0, 512, 1024, 1536, 1664, 1792, 2304


def _rms(x, w):
    return (x * lax.rsqrt(jnp.mean(x * x, axis=-1, keepdims=True) + RMS_EPS)) * w


def _tile_specs(width):
    return pl.BlockSpec((SUB_SEQS, TOK_TILE, width), lambda sb, cb, s: (sb * N_SUB + s, cb, 0))


_CHUNK_TILE_SPEC = pl.BlockSpec((N_GROUPS, None, None, LANES, CHUNK_W), lambda sb, cb, s: (0, sb, cb, 0, 0))


def _const_spec(shape):
    return pl.BlockSpec(shape, lambda sb, cb, s: (0,) * len(shape))


PAIR_ROWS = 2 * SEQ_TILE * CHUNK


def _pair_perm(t_major_out):
    out_row = lax.broadcasted_iota(jnp.int32, (PAIR_ROWS, PAIR_ROWS), 0)
    in_row = lax.broadcasted_iota(jnp.int32, (PAIR_ROWS, PAIR_ROWS), 1)
    tcs, cst = (out_row, in_row) if t_major_out else (in_row, out_row)
    hit = (tcs // (2 * SEQ_TILE) == cst % CHUNK) & (tcs % (2 * SEQ_TILE) == cst // CHUNK)
    return jnp.where(hit, 1.0, 0.0).astype(BF16)


def _proj_kernel(x_ref, nw_ref, w_ref, up_ref, zs_ref, q_ref, k_ref, v_ref, za_ref, u0_scr, u1_scr, b_scr, t_scr,
                 *, n_tiles):
    tile = pl.program_id(0)
    s = pl.program_id(1)

    def project(u_scr):
        hb = _rms(x_ref[...].reshape(SUB, D_MODEL), nw_ref[...]).astype(BF16)

        def mm(lo, hi):
            return jnp.dot(hb, w_ref[:, lo:hi], preferred_element_type=F32)

        def put(ref, val):
            ref[...] = val.astype(BF16).reshape(ref.shape)

        uz = mm(_C_U, _C_Q)
        u_scr[pl.ds(pl.multiple_of(s * SUB, SUB), SUB), :] = uz[:, :D_SSM].astype(BF16)
        put(zs_ref, uz[:, D_SSM:])
        qkv = mm(_C_Q, _C_ZA)
        put(q_ref, qkv[:, :D_ATTN] * (HEAD_DIM ** -0.5 * LOG2_E))
        put(k_ref, qkv[:, D_ATTN:D_ATTN + KV_W])
        put(v_ref, qkv[:, D_ATTN + KV_W:])
        put(za_ref, mm(_C_ZA, _C_END))

    def relayout(u_scr):
        perm = _pair_perm(t_major_out=True)
        for cp in range(TILE_CHUNKS // 2):
            a = jnp.concatenate(
                [u_scr[pl.ds(sq * TOK_TILE + (2 * cp + c2) * CHUNK, CHUNK), :]
                 for c2 in range(2) for sq in range(SEQ_TILE)], axis=0)
            b_scr[cp] = jnp.dot(perm, a, preferred_element_type=F32).astype(BF16)
        for t in range(CHUNK):
            for j in range(LANE_TILES):
                blk = jnp.concatenate(
                    [b_scr[cp, pl.ds(t * 2 * SEQ_TILE, 2 * SEQ_TILE), pl.ds(j * LANES, LANES)]
                     for cp in range(TILE_CHUNKS // 2)], axis=0)
                t_scr[t, j] = blk.T
        for g in range(N_GROUPS):
            j, g3 = divmod(g, GROUPS_PER_LANE_TILE)
            for th in range(CHUNK_W // LANES):
                rows = jnp.concatenate(
                    [t_scr[th * 8 + t3, j, pl.ds(g3 * SSM_GROUP, SSM_GROUP), :] for t3 in range(8)], axis=0)
                up_ref[g, :, pl.ds(th * LANES, LANES)] = rows.T

    u_scrs = (u0_scr, u1_scr)
    for parity in range(2):
        mine, other = u_scrs[parity], u_scrs[1 - parity]
        is_mine = (tile % 2 == parity) & (tile < n_tiles)

        @pl.when(is_mine & (s == 0) & (tile > 0))
        def _():
            project(mine)
            relayout(other)

        @pl.when(is_mine & ((s > 0) | (tile == 0)))
        def _():
            project(mine)

    @pl.when((tile == n_tiles) & (s == 0))
    def _():
        relayout(u_scrs[(n_tiles - 1) % 2])


def _proj_call(x, norm_w, w_in):
    batch, seq_len, _ = x.shape
    nsb, ncb = batch // SEQ_TILE, seq_len // TOK_TILE
    n_tiles = nsb * ncb

    def tok_spec(width):
        def index(tile, s):
            tile_c = jnp.minimum(tile, n_tiles - 1)
            s_c = jnp.where(tile < n_tiles, s, N_SUB - 1)
            return ((tile_c // ncb) * N_SUB + s_c, tile_c % ncb, 0)
        return pl.BlockSpec((SUB_SEQS, TOK_TILE, width), index)

    def up_index(tile, s):
        prev = jnp.maximum(tile - 1, 0)
        return (0, prev // ncb, prev % ncb, 0, 0)

    const = lambda shape: pl.BlockSpec(shape, lambda tile, s: (0,) * len(shape))
    tok = lambda width: jax.ShapeDtypeStruct((batch, seq_len, width), BF16)
    out_shapes = (
        jax.ShapeDtypeStruct((N_GROUPS, nsb, ncb, LANES, CHUNK_W), BF16),
        tok(D_SSM),
        tok(D_ATTN),
        tok(KV_W),
        tok(KV_W),
        tok(D_ATTN),
    )
    return pl.pallas_call(
        functools.partial(_proj_kernel, n_tiles=n_tiles),
        grid=(n_tiles + 1, N_SUB),
        in_specs=[tok_spec(D_MODEL), const((1, D_MODEL)), const((D_MODEL, _C_END))],
        out_specs=(pl.BlockSpec((N_GROUPS, None, None, LANES, CHUNK_W), up_index), tok_spec(D_SSM),
                   tok_spec(D_ATTN), tok_spec(KV_W), tok_spec(KV_W), tok_spec(D_ATTN)),
        out_shape=out_shapes,
        scratch_shapes=[
            pltpu.VMEM((TILE_ROWS, D_SSM), BF16),
            pltpu.VMEM((TILE_ROWS, D_SSM), BF16),
            pltpu.VMEM((TILE_CHUNKS // 2, PAIR_ROWS, D_SSM), BF16),
            pltpu.VMEM((CHUNK, LANE_TILES, LANES, LANES), BF16),
        ],
        compiler_params=pltpu.CompilerParams(
            dimension_semantics=("arbitrary", "arbitrary"), vmem_limit_bytes=VMEM_LIMIT),
        name="proj",
    )(x, norm_w, w_in)


def _ssm_kernel(u_ref, toep_ref, bs_ref, cs_ref, are_ref, aim_ref, y_ref, s_scr, xf_scr, xb_scr,
                *, nsb, ncb):
    nch = ncb * TILE_CHUNKS
    tiles_per_mm = MM_ROWS // LANES
    seq_rows = nsb * SEQ_TILE

    def mm_blocks():
        for sb in range(nsb):
            for cq in range(ncb // tiles_per_mm):
                yield sb, cq * tiles_per_mm, pl.ds((sb * ncb + cq * tiles_per_mm) * LANES, MM_ROWS)

    def load_u(sb, cb0):
        return jnp.concatenate([u_ref[sb, cb0 + i] for i in range(tiles_per_mm)], axis=0)

    for sb, cb0, rows in mm_blocks():
        st = jnp.dot(load_u(sb, cb0), bs_ref[...], preferred_element_type=F32)
        s_scr[0, rows, :] = st[:, :LANES]
        s_scr[1, rows, :] = st[:, LANES:]

    def chunk_rows(sb, ch):
        return pl.ds(pl.multiple_of(sb * nch * SEQ_TILE + ch * SEQ_TILE, SEQ_TILE), SEQ_TILE)

    def load(ref, part, ch):
        return jnp.concatenate([ref[part, chunk_rows(sb, ch), :] for sb in range(nsb)], axis=0)

    def store(ref, part, ch, val):
        for sb in range(nsb):
            ref[part, chunk_rows(sb, ch), :] = val[sb * SEQ_TILE:(sb + 1) * SEQ_TILE, :]

    are = jnp.broadcast_to(are_ref[...], (seq_rows, LANES))
    aim = jnp.broadcast_to(aim_ref[...], (seq_rows, LANES))
    is_fwd = lax.broadcasted_iota(jnp.int32, (seq_rows, LANES), 1) < SSM_STATE

    def step(i, carry):
        xr, xi = carry
        j = nch - 1 - i
        store(xf_scr, 0, i, xr)
        store(xf_scr, 1, i, xi)
        store(xb_scr, 0, j, xr)
        store(xb_scr, 1, j, xi)
        s_re = jnp.where(is_fwd, load(s_scr, 0, i), load(s_scr, 0, j))
        s_im = jnp.where(is_fwd, load(s_scr, 1, i), load(s_scr, 1, j))
        return (are * xr - aim * xi + s_re, are * xi + aim * xr + s_im)

    zero = jnp.zeros((seq_rows, LANES), F32)
    lax.fori_loop(0, nch, step, (zero, zero), unroll=4)

    fwd_lane = lax.broadcasted_iota(jnp.int32, (MM_ROWS, LANES), 1) < SSM_STATE
    for sb, cb0, rows in mm_blocks():
        xin = jnp.concatenate(
            [jnp.where(fwd_lane, xf_scr[c, rows, :], xb_scr[c, rows, :]) for c in range(2)], axis=1).astype(BF16)
        y = (jnp.dot(load_u(sb, cb0), toep_ref[...], preferred_element_type=F32)
             + jnp.dot(xin, cs_ref[...], preferred_element_type=F32))
        for i in range(tiles_per_mm):
            y_ref[sb, cb0 + i] = y[i * LANES:(i + 1) * LANES, :].astype(BF16)


def _ssm_call(up, ops):
    toep, bs, cs, are, aim = ops
    _, nsb, ncb, _, _ = up.shape
    n_rows = nsb * ncb * LANES
    assert ncb % (MM_ROWS // LANES) == 0
    data_spec = pl.BlockSpec((None, nsb, ncb, LANES, CHUNK_W), lambda g: (g, 0, 0, 0, 0))
    grp = lambda *shape: pl.BlockSpec((None,) + shape, lambda g: (g, 0, 0))
    return pl.pallas_call(
        functools.partial(_ssm_kernel, nsb=nsb, ncb=ncb),
        grid=(N_GROUPS,),
        in_specs=[data_spec, grp(CHUNK_W, CHUNK_W), grp(CHUNK_W, STATE_W), grp(STATE_W, CHUNK_W),
                  grp(1, LANES), grp(1, LANES)],
        out_specs=data_spec,
        out_shape=jax.ShapeDtypeStruct(up.shape, BF16),
        scratch_shapes=[pltpu.VMEM((2, n_rows, LANES), F32)] * 3,
        compiler_params=pltpu.CompilerParams(
            dimension_semantics=("arbitrary",), vmem_limit_bytes=VMEM_LIMIT),
        name="ssm",
    )(up, toep, bs, cs, are, aim)


def _attn_kernel(sink_ref, q_ref, k_ref, v_ref, bias_ref, o_ref, ka_scr, kb_scr, wa_scr, wb_scr, *, seq_len):
    jb = pl.program_id(1)
    nb = seq_len // BLOCK

    @pl.when(jb == 0)
    def _():
        low = lax.broadcasted_iota(jnp.int32, (BLOCK, LANES), 1) < HEAD_DIM
        zero_rows = jnp.zeros((HEAD_DIM, BLOCK), BF16)
        one_rows = jnp.ones((HEAD_DIM, BLOCK), BF16)

        def per_block(r, carry):
            rows = pl.ds(pl.multiple_of(r * BLOCK, BLOCK), BLOCK)
            k = k_ref[rows, :]
            k_swapped = pltpu.roll(k, HEAD_DIM, 1)
            zero = jnp.zeros_like(k)
            ka_scr[0, rows, :] = jnp.where(low, k, zero)
            kb_scr[0, rows, :] = jnp.where(low, zero, k_swapped)
            ka_scr[1, rows, :] = jnp.where(low, k_swapped, zero)
            kb_scr[1, rows, :] = jnp.where(low, zero, k)
            vt = v_ref[rows, :].T
            for kh in range(N_KV_HEADS):
                vt_kh = vt[kh * HEAD_DIM:(kh + 1) * HEAD_DIM]
                wa_scr[kh, r] = jnp.concatenate([vt_kh, zero_rows, one_rows, zero_rows], axis=0)
                wb_scr[kh, r] = jnp.concatenate([zero_rows, vt_kh, zero_rows, one_rows], axis=0)
            return carry

        lax.fori_loop(0, nb, per_block, 0)

    col_tile0 = lax.broadcasted_iota(jnp.int32, (1, 2 * LANES), 1) < LANES
    row_slot0 = lax.broadcasted_iota(jnp.int32, (BLOCK, 2 * LANES), 0) < HEAD_DIM
    def query_block(sb):
        n = jb * (Q_ROWS // BLOCK) + sb
        kb0 = jnp.clip(n - 1, 0, nb - KEYS // BLOCK)
        ks = pl.multiple_of(kb0 * BLOCK, BLOCK)
        var = jnp.where(n == 0, 0, jnp.where(n == nb - 1, 2, 1))
        qrows = pl.ds(pl.multiple_of(sb * BLOCK, BLOCK), BLOCK)
        for kh in range(N_KV_HEADS):
            kcat = jnp.concatenate([ka_scr[kh, pl.ds(ks, KEYS), :], kb_scr[kh, pl.ds(ks, KEYS), :]], axis=0)
            q2 = jnp.concatenate([q_ref[qrows, pl.ds((2 * kh + jj) * LANES, LANES)] for jj in range(2)], axis=0)
            t = lax.dot_general(kcat, q2, (((1,), (1,)), ((), ())), preferred_element_type=F32)
            t = t + bias_ref[var, kh]
            ps, sink_terms = [], []
            for hh in range(2):
                th = t[hh * KEYS:(hh + 1) * KEYS]
                sink = jnp.where(col_tile0, sink_ref[4 * kh + hh], sink_ref[4 * kh + 2 + hh])
                m = jnp.maximum(jnp.max(th, axis=0, keepdims=True), sink)
                ps.append(jnp.exp2(th - m).astype(BF16))
                sink_terms.append(jnp.exp2(sink - m))
            w = jnp.concatenate([wa_scr[kh, kb0 + i] for i in range(KEYS // BLOCK)]
                                + [wb_scr[kh, kb0 + i] for i in range(KEYS // BLOCK)], axis=1)
            o2 = jnp.dot(w, jnp.concatenate(ps, axis=0), preferred_element_type=F32)
            denom = o2[LANES:] + jnp.where(row_slot0, sink_terms[0], sink_terms[1])
            on = o2[:LANES] / denom
            for jj in range(2):
                o_ref[qrows, pl.ds((2 * kh + jj) * LANES, LANES)] = (
                    on[:, jj * LANES:(jj + 1) * LANES].T.astype(BF16))

    def query_blocks(i, carry):
        for sb in range(BLOCKS_PER_ITER):
            query_block(i * BLOCKS_PER_ITER + sb)
        return carry

    lax.fori_loop(0, Q_ROWS // (BLOCKS_PER_ITER * BLOCK), query_blocks, 0)


def _attn_call(q, k, v, bias, sink, batch, seq_len):
    assert seq_len % Q_ROWS == 0 and seq_len >= KEYS
    nq = seq_len // Q_ROWS
    nb = seq_len // BLOCK
    return pl.pallas_call(
        functools.partial(_attn_kernel, seq_len=seq_len),
        grid_spec=pltpu.PrefetchScalarGridSpec(
            num_scalar_prefetch=1,
            grid=(batch, nq),
            in_specs=[
                pl.BlockSpec((Q_ROWS, D_ATTN), lambda b, j, s: (b * nq + j, 0)),
                pl.BlockSpec((seq_len, KV_W), lambda b, j, s: (b, 0)),
                pl.BlockSpec((seq_len, KV_W), lambda b, j, s: (b, 0)),
                pl.BlockSpec((3, N_KV_HEADS, 2 * KEYS, 2 * LANES), lambda b, j, s: (0, 0, 0, 0)),
            ],
            out_specs=pl.BlockSpec((Q_ROWS, D_ATTN), lambda b, j, s: (b * nq + j, 0)),
            scratch_shapes=[
                pltpu.VMEM((N_KV_HEADS, seq_len, LANES), BF16),
                pltpu.VMEM((N_KV_HEADS, seq_len, LANES), BF16),
                pltpu.VMEM((N_KV_HEADS, nb, 2 * LANES, BLOCK), BF16),
                pltpu.VMEM((N_KV_HEADS, nb, 2 * LANES, BLOCK), BF16),
            ],
        ),
        out_shape=jax.ShapeDtypeStruct((batch * seq_len, D_ATTN), BF16),
        compiler_params=pltpu.CompilerParams(
            dimension_semantics=("arbitrary", "arbitrary"), vmem_limit_bytes=VMEM_LIMIT),
        name="attn",
    )(sink, q, k, v, bias)


def _gelu_tanh(x):
    c = math.sqrt(2.0 / math.pi)
    half = 0.5 * x
    return half + half * jnp.tanh(x * (c + (c * 0.044715) * (x * x)))


def _sigmoid(x):
    return 0.5 * jnp.tanh(0.5 * x) + 0.5


def _silu(x):
    return x * _sigmoid(x)


def _post_kernel(x_ref, yp_ref, zs_ref, ao_ref, za_ref, wglu_ref, bglu_ref, snw_ref, anw_ref,
                 wout_ref, fnw_ref, o_ref, y_scr, b_scr, t_scr):
    s = pl.program_id(2)

    @pl.when(s == 0)
    def _():
        for g in range(N_GROUPS):
            j, g3 = divmod(g, GROUPS_PER_LANE_TILE)
            for th in range(CHUNK_W // LANES):
                tt = yp_ref[g, :, pl.ds(th * LANES, LANES)].T
                for t3 in range(8):
                    t_scr[th * 8 + t3, j, pl.ds(g3 * SSM_GROUP, SSM_GROUP), :] = (
                        tt[t3 * SSM_GROUP:(t3 + 1) * SSM_GROUP, :])
        for t in range(CHUNK):
            for j in range(LANE_TILES):
                b_scr[t, :, pl.ds(j * LANES, LANES)] = t_scr[t, j].T
        perm = _pair_perm(t_major_out=False)
        for cp in range(TILE_CHUNKS // 2):
            a = jnp.concatenate(
                [b_scr[t, pl.ds(cp * 2 * SEQ_TILE, 2 * SEQ_TILE), :] for t in range(CHUNK)], axis=0)
            ys = jnp.dot(perm, a, preferred_element_type=F32)
            for c2 in range(2):
                for sq in range(SEQ_TILE):
                    r0 = (c2 * SEQ_TILE + sq) * CHUNK
                    y_scr[pl.ds(sq * TOK_TILE + (2 * cp + c2) * CHUNK, CHUNK), :] = ys[r0:r0 + CHUNK, :]

    def get(ref):
        return ref[...].reshape(SUB, ref.shape[-1]).astype(F32)

    y = y_scr[pl.ds(pl.multiple_of(s * SUB, SUB), SUB), :]
    g = _gelu_tanh(y)
    gate = _sigmoid(jnp.dot(g.astype(BF16), wglu_ref[...], preferred_element_type=F32) + bglu_ref[...])
    n_ssm = _rms(g * gate, snw_ref[...]) * _silu(get(zs_ref))
    n_attn = _rms(get(ao_ref), anw_ref[...]) * _silu(get(za_ref))
    mixed = jnp.concatenate([n_ssm, n_attn], axis=1).astype(BF16)
    res = get(x_ref) + jnp.dot(mixed, wout_ref[...], preferred_element_type=F32)
    o_ref[...] = _rms(res, fnw_ref[...]).reshape(o_ref.shape)


def _post_call(x, yp, zs, ao, za, w_glu, b_glu, ssm_norm_w, attn_norm_w, w_out, final_norm_w):
    batch, seq_len, _ = x.shape
    nsb, ncb = batch // SEQ_TILE, seq_len // TOK_TILE
    return pl.pallas_call(
        _post_kernel,
        grid=(nsb, ncb, N_SUB),
        in_specs=[
            _tile_specs(D_MODEL), _CHUNK_TILE_SPEC, _tile_specs(D_SSM), _tile_specs(D_ATTN), _tile_specs(D_ATTN),
            _const_spec((D_SSM, D_SSM)), _const_spec((1, D_SSM)), _const_spec((1, D_SSM)),
            _const_spec((1, D_ATTN)), _const_spec((D_MODEL, D_MODEL)), _const_spec((1, D_MODEL)),
        ],
        out_specs=_tile_specs(D_MODEL),
        out_shape=jax.ShapeDtypeStruct((batch, seq_len, D_MODEL), F32),
        scratch_shapes=[
            pltpu.VMEM((TILE_ROWS, D_SSM), F32),
            pltpu.VMEM((CHUNK, LANES, D_SSM), BF16),
            pltpu.VMEM((CHUNK, LANE_TILES, LANES, LANES), BF16),
        ],
        compiler_params=pltpu.CompilerParams(
            dimension_semantics=("arbitrary", "arbitrary", "arbitrary"), vmem_limit_bytes=VMEM_LIMIT),
        name="post",
    )(x, yp, zs, ao, za, w_glu, b_glu, ssm_norm_w, attn_norm_w, w_out, final_norm_w)


def _cmul(ar, ai, br, bi):
    return ar * br - ai * bi, ar * bi + ai * br


def _split_bf16(x):
    hi = x.astype(BF16)
    return hi, (x - hi.astype(F32)).astype(BF16)


def _dot_split(a, b):
    a_hi, a_lo = _split_bf16(a)
    b_hi, b_lo = _split_bf16(b)
    dot = functools.partial(jnp.dot, preferred_element_type=F32)
    return dot(a_hi, b_hi) + (dot(a_hi, b_lo) + dot(a_lo, b_hi))


def _ops_kernel(row_ref, col_ref, bt_ref, ct_ref, dsk_ref, toep_ref, bs_ref, cs_ref, a_ref):
    def discretise(lr, li, ls):
        dt = jnp.exp(ls)
        mag = jnp.exp(lr * dt)
        return mag * jnp.cos(li * dt), mag * jnp.sin(li * dt)

    lr, li = row_ref[0], row_ref[1]
    lb_re, lb_im = discretise(lr, li, row_ref[2])
    den = lr * lr + li * li
    nr = lb_re - 1.0
    coef_re = (nr * lr + lb_im * li) / den
    coef_im = (lb_im * lr - nr * li) / den
    bb_re, bb_im = _cmul(coef_re, coef_im, bt_ref[0], bt_ref[1])
    bb_re = jnp.concatenate([bb_re] * CHUNK, axis=0)
    bb_im = jnp.concatenate([bb_im] * CHUNK, axis=0)

    row_t = lax.broadcasted_iota(jnp.int32, (CHUNK_W, LANES), 0) // SSM_GROUP
    fwd_lane = lax.broadcasted_iota(jnp.int32, (CHUNK_W, LANES), 1) < SSM_STATE
    col_t = lax.broadcasted_iota(jnp.int32, (LANES, CHUNK_W), 1) // SSM_GROUP
    fwd_row = lax.broadcasted_iota(jnp.int32, (LANES, CHUNK_W), 0) < SSM_STATE
    ct_re, ct_im = ct_ref[0], ct_ref[1]
    w_re = jnp.zeros((CHUNK_W, LANES), F32)
    w_im = jnp.zeros((CHUNK_W, LANES), F32)
    toep = jnp.zeros((CHUNK_W, CHUNK_W), F32)
    for i in range(CHUNK):
        t_now = jnp.where(fwd_lane, i, CHUNK - 1 - i)
        w_re, w_im = _cmul(w_re, w_im, lb_re, lb_im)
        w_re = w_re + jnp.where(row_t == t_now, bb_re, 0.0)
        w_im = w_im + jnp.where(row_t == t_now, bb_im, 0.0)
        out_now = col_t == jnp.where(fwd_row, i, CHUNK - 1 - i)
        rhs = jnp.concatenate([jnp.where(out_now, ct_re, 0.0), jnp.where(out_now, -ct_im, 0.0)], axis=0)
        toep = toep + _dot_split(jnp.concatenate([w_re, w_im], axis=1), rhs)
    r = lax.broadcasted_iota(jnp.int32, (CHUNK_W, CHUNK_W), 0)
    c = lax.broadcasted_iota(jnp.int32, (CHUNK_W, CHUNK_W), 1)
    toep_ref[...] = (toep + jnp.where(r == c, dsk_ref[...], 0.0)).astype(BF16)
    bs_ref[...] = jnp.concatenate([w_re, w_im], axis=1).astype(BF16)

    pr, pi = discretise(col_ref[0], col_ref[1], col_ref[2])
    k = jnp.where(fwd_row, col_t + 1, CHUNK - col_t)
    acc_re = jnp.ones((LANES, CHUNK_W), F32)
    acc_im = jnp.zeros((LANES, CHUNK_W), F32)
    for bit in range(CHUNK.bit_length()):
        nre, nim = _cmul(acc_re, acc_im, pr, pi)
        take = (k & (1 << bit)) != 0
        acc_re = jnp.where(take, nre, acc_re)
        acc_im = jnp.where(take, nim, acc_im)
        pr, pi = _cmul(pr, pi, pr, pi)
    cp_re, cp_im = _cmul(ct_re, ct_im, acc_re, acc_im)
    cs_ref[...] = jnp.concatenate([cp_re, -cp_im], axis=0).astype(BF16)

    a_re, a_im = lb_re, lb_im
    for _ in range(CHUNK.bit_length() - 1):
        a_re, a_im = _cmul(a_re, a_im, a_re, a_im)
    a_ref[0] = a_re
    a_ref[1] = a_im


def _ssm_operators(lam_re, lam_im, log_step, b_re, b_im, c_re, c_im, d_skip):
    g_first = lambda a: jnp.transpose(a.astype(F32), (1, 0, 2)).reshape(N_GROUPS, 2 * SSM_STATE)
    params = jnp.stack([g_first(lam_re), g_first(lam_im),
                        jnp.repeat(log_step.astype(F32).T, SSM_STATE, axis=1)], axis=1)
    b_t = lambda b: jnp.transpose(b.astype(F32), (1, 3, 0, 2)).reshape(N_GROUPS, SSM_GROUP, 2 * SSM_STATE)
    c_t = lambda c: jnp.tile(
        jnp.transpose(c.astype(F32), (1, 0, 3, 2)).reshape(N_GROUPS, 2 * SSM_STATE, SSM_GROUP), (1, 1, CHUNK))
    bt = jnp.stack([b_t(b_re), b_t(b_im)], axis=1)
    ct = jnp.stack([c_t(c_re), c_t(c_im)], axis=1)
    dsk = jnp.tile(d_skip.astype(F32), (1, CHUNK))[:, None, :]
    grp = lambda *shape: pl.BlockSpec((None,) + shape, lambda g: (g,) + (0,) * len(shape))
    mat = jax.ShapeDtypeStruct((N_GROUPS, CHUNK_W, CHUNK_W), BF16)
    toep, bs, cs, a = pl.pallas_call(
        _ops_kernel,
        grid=(N_GROUPS,),
        in_specs=[grp(3, 1, LANES), grp(3, LANES, 1), grp(2, SSM_GROUP, LANES), grp(2, LANES, CHUNK_W),
                  grp(1, CHUNK_W)],
        out_specs=(grp(CHUNK_W, CHUNK_W), grp(CHUNK_W, STATE_W), grp(STATE_W, CHUNK_W), grp(2, 1, LANES)),
        out_shape=(mat, mat, mat, jax.ShapeDtypeStruct((N_GROUPS, 2, 1, LANES), F32)),
        compiler_params=pltpu.CompilerParams(dimension_semantics=("arbitrary",)),
        name="s5_ops",
    )(params[:, :, None, :], params[:, :, :, None], bt, ct, dsk)
    return toep, bs, cs, a[:, 0], a[:, 1]


def _t5_buckets_np(rel):
    half = N_BUCKETS // 2
    max_exact = half // 2
    ret = np.where(rel > 0, half, 0)
    n = np.abs(rel)
    nf = np.maximum(n, 1).astype(np.float64)
    large = max_exact + (np.log(nf / max_exact) / math.log(MAX_DISTANCE / max_exact)
                         * (half - max_exact)).astype(np.int32)
    large = np.minimum(large, half - 1)
    return ret + np.where(n < max_exact, n, large)


def _attn_tables(rel_bias):
    offsets = (0, -BLOCK, -2 * BLOCK)
    period = 4 * BLOCK
    dist = np.arange(period)
    dist = np.where(dist < KEYS, dist, dist - period)
    rel = np.stack([dist + off for off in offsets])
    per_dist = rel_bias.astype(F32)[_t5_buckets_np(rel)] * LOG2_E
    per_dist = jnp.where((np.abs(rel) <= WINDOW)[:, :, None], per_dist, NEG_INF)
    per_dist = jnp.transpose(per_dist, (0, 2, 1))
    neg = jnp.concatenate([per_dist[:, :, :1], per_dist[:, :, :0:-1]], axis=-1)
    skew = jnp.tile(neg, (1, 1, KEYS))[:, :, :KEYS * (period - 1)]
    bias = skew.reshape(3, N_HEADS, KEYS, period - 1)[:, :, :, :BLOCK]
    bias = bias.reshape(3, N_KV_HEADS, 2, 2, KEYS, BLOCK)
    bias = jnp.concatenate([bias[:, :, 0], bias[:, :, 1]], axis=-1)
    return bias.reshape(3, N_KV_HEADS, 2 * KEYS, 2 * LANES)


def kernel(x_prompt, x_sample, norm_w, w_in, lam_re, lam_im, log_step, b_re, b_im, c_re, c_im, d_skip,
           w_glu, b_glu, ssm_norm_w, sink, attn_norm_w, w_out, rel_bias, final_norm_w):
    assert norm_w.shape[0] == 1, "single-layer encoder"
    ops = _ssm_operators(lam_re[0], lam_im[0], log_step[0], b_re[0], b_im[0], c_re[0], c_im[0], d_skip[0])
    bias = _attn_tables(rel_bias)
    w_in_b = w_in[0].astype(BF16)
    w_glu_b = w_glu[0].astype(BF16)
    w_out_b = w_out[0].astype(BF16)
    sink_f = sink[0].astype(F32) * LOG2_E

    outs = []
    for x in (x_prompt, x_sample):
        batch, seq_len, _ = x.shape
        assert batch % SEQ_TILE == 0 and seq_len % TOK_TILE == 0
        n_tok = batch * seq_len
        up, zs, q, k, v, za = _proj_call(x, norm_w, w_in_b)
        yp = _ssm_call(up, ops)
        ao = _attn_call(q.reshape(n_tok, D_ATTN), k.reshape(n_tok, KV_W), v.reshape(n_tok, KV_W),
                        bias, sink_f, batch, seq_len)
        outs.append(_post_call(x, yp, zs, ao.reshape(batch, seq_len, D_ATTN), za, w_glu_b, b_glu, ssm_norm_w,
                               attn_norm_w, w_out_b, final_norm_w.reshape(1, D_MODEL)))
    return tuple(outs)
```

```python
import functools
import math

import numpy as np
import jax
import jax.numpy as jnp
from jax import lax
from jax.experimental import pallas as pl
from jax.experimental.pallas import tpu as pltpu

F32 = jnp.float32
BF16 = jnp.bfloat16

D_MODEL = 1024
D_SSM = 512
D_ATTN = 512
SSM_GROUP = 16
N_GROUPS = D_SSM // SSM_GROUP
SSM_STATE = 64
HEAD_DIM = 64
N_HEADS = D_ATTN // HEAD_DIM
N_KV_HEADS = 2
WINDOW = 128
BLOCK = 128
N_BUCKETS = 32
MAX_DISTANCE = 128
RMS_EPS = 1e-6
NEG_INF = -1e30
LOG2_E = math.log2(math.e)

LANES = 128
SUBLANES = 8
CHUNK = 16
CHUNK_W = CHUNK * SSM_GROUP
STATE_W = 4 * SSM_STATE
SEQ_TILE = SUBLANES
TILE_CHUNKS = LANES // SEQ_TILE
TOK_TILE = TILE_CHUNKS * CHUNK
TILE_ROWS = SEQ_TILE * TOK_TILE
SUB_SEQS = 4
SUB = SUB_SEQS * TOK_TILE
N_SUB = SEQ_TILE // SUB_SEQS
LANE_TILES = D_SSM // LANES
GROUPS_PER_LANE_TILE = LANES // SSM_GROUP
KEYS = 3 * BLOCK
Q_ROWS = 1024
BLOCKS_PER_ITER = 4
KV_W = N_KV_HEADS * HEAD_DIM
MM_ROWS = 512
VMEM_LIMIT = 56 * 1024 * 1024

_C_U, _C_ZS, _C_Q, _C_K, _C_V, _C_ZA, _C_END = 0, 512, 1024, 1536, 1664, 1792, 2304


def _rms(x, w):
    return (x * lax.rsqrt(jnp.mean(x * x, axis=-1, keepdims=True) + RMS_EPS)) * w


def _tile_specs(width):
    return pl.BlockSpec((SUB_SEQS, TOK_TILE, width), lambda sb, cb, s: (sb * N_SUB + s, cb, 0))


_CHUNK_TILE_SPEC = pl.BlockSpec((N_GROUPS, None, None, LANES, CHUNK_W), lambda sb, cb, s: (0, sb, cb, 0, 0))


def _const_spec(shape):
    return pl.BlockSpec(shape, lambda sb, cb, s: (0,) * len(shape))


PAIR_ROWS = 2 * SEQ_TILE * CHUNK


def _pair_perm(t_major_out):
    out_row = lax.broadcasted_iota(jnp.int32, (PAIR_ROWS, PAIR_ROWS), 0)
    in_row = lax.broadcasted_iota(jnp.int32, (PAIR_ROWS, PAIR_ROWS), 1)
    tcs, cst = (out_row, in_row) if t_major_out else (in_row, out_row)
    hit = (tcs // (2 * SEQ_TILE) == cst % CHUNK) & (tcs % (2 * SEQ_TILE) == cst // CHUNK)
    return jnp.where(hit, 1.0, 0.0).astype(BF16)


def _proj_kernel(x_ref, nw_ref, w_ref, up_ref, zs_ref, q_ref, k_ref, v_ref, za_ref, u_scr, b_scr, t_scr):
    s = pl.program_id(2)
    hb = _rms(x_ref[...].reshape(SUB, D_MODEL), nw_ref[...]).astype(BF16)

    def mm(lo, hi):
        return jnp.dot(hb, w_ref[:, lo:hi], preferred_element_type=F32)

    def put(ref, val):
        ref[...] = val.astype(BF16).reshape(ref.shape)

    uz = mm(_C_U, _C_Q)
    u_scr[pl.ds(pl.multiple_of(s * SUB, SUB), SUB), :] = uz[:, :D_SSM].astype(BF16)
    put(zs_ref, uz[:, D_SSM:])
    qkv = mm(_C_Q, _C_ZA)
    put(q_ref, qkv[:, :D_ATTN] * (HEAD_DIM ** -0.5 * LOG2_E))
    put(k_ref, qkv[:, D_ATTN:D_ATTN + KV_W])
    put(v_ref, qkv[:, D_ATTN + KV_W:])
    put(za_ref, mm(_C_ZA, _C_END))

    @pl.when(s == N_SUB - 1)
    def _():
        perm = _pair_perm(t_major_out=True)
        for cp in range(TILE_CHUNKS // 2):
            a = jnp.concatenate(
                [u_scr[pl.ds(sq * TOK_TILE + (2 * cp + c2) * CHUNK, CHUNK), :]
                 for c2 in range(2) for sq in range(SEQ_TILE)], axis=0)
            b_scr[cp] = jnp.dot(perm, a, preferred_element_type=F32).astype(BF16)
        for t in range(CHUNK):
            for j in range(LANE_TILES):
                blk = jnp.concatenate(
                    [b_scr[cp, pl.ds(t * 2 * SEQ_TILE, 2 * SEQ_TILE), pl.ds(j * LANES, LANES)]
                     for cp in range(TILE_CHUNKS // 2)], axis=0)
                t_scr[t, j] = blk.T
        for g in range(N_GROUPS):
            j, g3 = divmod(g, GROUPS_PER_LANE_TILE)
            for th in range(CHUNK_W // LANES):
                rows = jnp.concatenate(
                    [t_scr[th * 8 + t3, j, pl.ds(g3 * SSM_GROUP, SSM_GROUP), :] for t3 in range(8)], axis=0)
                up_ref[g, :, pl.ds(th * LANES, LANES)] = rows.T


def _proj_call(x, norm_w, w_in):
    batch, seq_len, _ = x.shape
    nsb, ncb = batch // SEQ_TILE, seq_len // TOK_TILE
    tok = lambda width: jax.ShapeDtypeStruct((batch, seq_len, width), BF16)
    out_shapes = (
        jax.ShapeDtypeStruct((N_GROUPS, nsb, ncb, LANES, CHUNK_W), BF16),
        tok(D_SSM),
        tok(D_ATTN),
        tok(KV_W),
        tok(KV_W),
        tok(D_ATTN),
    )
    return pl.pallas_call(
        _proj_kernel,
        grid=(nsb, ncb, N_SUB),
        in_specs=[_tile_specs(D_MODEL), _const_spec((1, D_MODEL)), _const_spec((D_MODEL, _C_END))],
        out_specs=(_CHUNK_TILE_SPEC, _tile_specs(D_SSM), _tile_specs(D_ATTN), _tile_specs(KV_W),
                   _tile_specs(KV_W), _tile_specs(D_ATTN)),
        out_shape=out_shapes,
        scratch_shapes=[
            pltpu.VMEM((TILE_ROWS, D_SSM), BF16),
            pltpu.VMEM((TILE_CHUNKS // 2, PAIR_ROWS, D_SSM), BF16),
            pltpu.VMEM((CHUNK, LANE_TILES, LANES, LANES), BF16),
        ],
        compiler_params=pltpu.CompilerParams(
            dimension_semantics=("arbitrary", "arbitrary", "arbitrary"), vmem_limit_bytes=VMEM_LIMIT),
        name="proj",
    )(x, norm_w, w_in)


def _ssm_kernel(u_ref, toep_ref, bs_ref, cs_ref, are_ref, aim_ref, y_ref, s_scr, xf_scr, xb_scr,
                *, nsb, ncb):
    nch = ncb * TILE_CHUNKS
    tiles_per_mm = MM_ROWS // LANES
    seq_rows = nsb * SEQ_TILE

    def mm_blocks():
        for sb in range(nsb):
            for cq in range(ncb // tiles_per_mm):
                yield sb, cq * tiles_per_mm, pl.ds((sb * ncb + cq * tiles_per_mm) * LANES, MM_ROWS)

    def load_u(sb, cb0):
        return jnp.concatenate([u_ref[sb, cb0 + i] for i in range(tiles_per_mm)], axis=0)

    for sb, cb0, rows in mm_blocks():
        st = jnp.dot(load_u(sb, cb0), bs_ref[...], preferred_element_type=F32)
        s_scr[0, rows, :] = st[:, :LANES]
        s_scr[1, rows, :] = st[:, LANES:]

    def chunk_rows(sb, ch):
        return pl.ds(pl.multiple_of(sb * nch * SEQ_TILE + ch * SEQ_TILE, SEQ_TILE), SEQ_TILE)

    def load(ref, part, ch):
        return jnp.concatenate([ref[part, chunk_rows(sb, ch), :] for sb in range(nsb)], axis=0)

    def store(ref, part, ch, val):
        for sb in range(nsb):
            ref[part, chunk_rows(sb, ch), :] = val[sb * SEQ_TILE:(sb + 1) * SEQ_TILE, :]

    are = jnp.broadcast_to(are_ref[...], (seq_rows, LANES))
    aim = jnp.broadcast_to(aim_ref[...], (seq_rows, LANES))
    is_fwd = lax.broadcasted_iota(jnp.int32, (seq_rows, LANES), 1) < SSM_STATE

    def step(i, carry):
        xr, xi = carry
        j = nch - 1 - i
        store(xf_scr, 0, i, xr)
        store(xf_scr, 1, i, xi)
        store(xb_scr, 0, j, xr)
        store(xb_scr, 1, j, xi)
        s_re = jnp.where(is_fwd, load(s_scr, 0, i), load(s_scr, 0, j))
        s_im = jnp.where(is_fwd, load(s_scr, 1, i), load(s_scr, 1, j))
        return (are * xr - aim * xi + s_re, are * xi + aim * xr + s_im)

    zero = jnp.zeros((seq_rows, LANES), F32)
    lax.fori_loop(0, nch, step, (zero, zero), unroll=4)

    fwd_lane = lax.broadcasted_iota(jnp.int32, (MM_ROWS, LANES), 1) < SSM_STATE
    for sb, cb0, rows in mm_blocks():
        xin = jnp.concatenate(
            [jnp.where(fwd_lane, xf_scr[c, rows, :], xb_scr[c, rows, :]) for c in range(2)], axis=1).astype(BF16)
        y = (jnp.dot(load_u(sb, cb0), toep_ref[...], preferred_element_type=F32)
             + jnp.dot(xin, cs_ref[...], preferred_element_type=F32))
        for i in range(tiles_per_mm):
            y_ref[sb, cb0 + i] = y[i * LANES:(i + 1) * LANES, :].astype(BF16)


def _ssm_call(up, ops):
    toep, bs, cs, are, aim = ops
    _, nsb, ncb, _, _ = up.shape
    n_rows = nsb * ncb * LANES
    assert ncb % (MM_ROWS // LANES) == 0
    data_spec = pl.BlockSpec((None, nsb, ncb, LANES, CHUNK_W), lambda g: (g, 0, 0, 0, 0))
    grp = lambda *shape: pl.BlockSpec((None,) + shape, lambda g: (g, 0, 0))
    return pl.pallas_call(
        functools.partial(_ssm_kernel, nsb=nsb, ncb=ncb),
        grid=(N_GROUPS,),
        in_specs=[data_spec, grp(CHUNK_W, CHUNK_W), grp(CHUNK_W, STATE_W), grp(STATE_W, CHUNK_W),
                  grp(1, LANES), grp(1, LANES)],
        out_specs=data_spec,
        out_shape=jax.ShapeDtypeStruct(up.shape, BF16),
        scratch_shapes=[pltpu.VMEM((2, n_rows, LANES), F32)] * 3,
        compiler_params=pltpu.CompilerParams(
            dimension_semantics=("arbitrary",), vmem_limit_bytes=VMEM_LIMIT),
        name="ssm",
    )(up, toep, bs, cs, are, aim)


def _attn_kernel(sink_ref, q_ref, k_ref, v_ref, za_ref, bias_ref, anw_ref, o_ref,
                 ka_scr, kb_scr, wa_scr, wb_scr, heads_scr, *, seq_len):
    jb = pl.program_id(1)
    nb = seq_len // BLOCK

    @pl.when(jb == 0)
    def _():
        low = lax.broadcasted_iota(jnp.int32, (BLOCK, LANES), 1) < HEAD_DIM
        zero_rows = jnp.zeros((HEAD_DIM, BLOCK), BF16)
        one_rows = jnp.ones((HEAD_DIM, BLOCK), BF16)

        def per_block(r, carry):
            rows = pl.ds(pl.multiple_of(r * BLOCK, BLOCK), BLOCK)
            k = k_ref[rows, :]
            k_swapped = pltpu.roll(k, HEAD_DIM, 1)
            zero = jnp.zeros_like(k)
            ka_scr[0, rows, :] = jnp.where(low, k, zero)
            kb_scr[0, rows, :] = jnp.where(low, zero, k_swapped)
            ka_scr[1, rows, :] = jnp.where(low, k_swapped, zero)
            kb_scr[1, rows, :] = jnp.where(low, zero, k)
            vt = v_ref[rows, :].T
            for kh in range(N_KV_HEADS):
                vt_kh = vt[kh * HEAD_DIM:(kh + 1) * HEAD_DIM]
                wa_scr[kh, r] = jnp.concatenate([vt_kh, zero_rows, one_rows, zero_rows], axis=0)
                wb_scr[kh, r] = jnp.concatenate([zero_rows, vt_kh, zero_rows, one_rows], axis=0)
            return carry

        lax.fori_loop(0, nb, per_block, 0)

    col_tile0 = lax.broadcasted_iota(jnp.int32, (1, 2 * LANES), 1) < LANES
    row_slot0 = lax.broadcasted_iota(jnp.int32, (BLOCK, 2 * LANES), 0) < HEAD_DIM
    def query_block(sb):
        n = jb * (Q_ROWS // BLOCK) + sb
        kb0 = jnp.clip(n - 1, 0, nb - KEYS // BLOCK)
        ks = pl.multiple_of(kb0 * BLOCK, BLOCK)
        var = jnp.where(n == 0, 0, jnp.where(n == nb - 1, 2, 1))
        qrows = pl.ds(pl.multiple_of(sb * BLOCK, BLOCK), BLOCK)
        for kh in range(N_KV_HEADS):
            kcat = jnp.concatenate([ka_scr[kh, pl.ds(ks, KEYS), :], kb_scr[kh, pl.ds(ks, KEYS), :]], axis=0)
            q2 = jnp.concatenate([q_ref[qrows, pl.ds((2 * kh + jj) * LANES, LANES)] for jj in range(2)], axis=0)
            t = lax.dot_general(kcat, q2, (((1,), (1,)), ((), ())), preferred_element_type=F32)
            t = t + bias_ref[var, kh]
            ps, sink_terms = [], []
            for hh in range(2):
                th = t[hh * KEYS:(hh + 1) * KEYS]
                sink = jnp.where(col_tile0, sink_ref[4 * kh + hh], sink_ref[4 * kh + 2 + hh])
                m = jnp.maximum(jnp.max(th, axis=0, keepdims=True), sink)
                ps.append(jnp.exp2(th - m).astype(BF16))
                sink_terms.append(jnp.exp2(sink - m))
            w = jnp.concatenate([wa_scr[kh, kb0 + i] for i in range(KEYS // BLOCK)]
                                + [wb_scr[kh, kb0 + i] for i in range(KEYS // BLOCK)], axis=1)
            o2 = jnp.dot(w, jnp.concatenate(ps, axis=0), preferred_element_type=F32)
            denom = o2[LANES:] + jnp.where(row_slot0, sink_terms[0], sink_terms[1])
            on = o2[:LANES] / denom
            for jj in range(2):
                heads_scr[:, pl.ds((2 * kh + jj) * LANES, LANES)] = on[:, jj * LANES:(jj + 1) * LANES].T
        gated = _rms(heads_scr[...], anw_ref[...]) * _silu(za_ref[qrows, :].astype(F32))
        o_ref[qrows, :] = gated.astype(BF16)

    def query_blocks(i, carry):
        for sb in range(BLOCKS_PER_ITER):
            query_block(i * BLOCKS_PER_ITER + sb)
        return carry

    lax.fori_loop(0, Q_ROWS // (BLOCKS_PER_ITER * BLOCK), query_blocks, 0)


def _attn_call(q, k, v, za, bias, sink, attn_norm_w, batch, seq_len):
    assert seq_len % Q_ROWS == 0 and seq_len >= KEYS
    nq = seq_len // Q_ROWS
    nb = seq_len // BLOCK
    q_spec = pl.BlockSpec((Q_ROWS, D_ATTN), lambda b, j, s: (b * nq + j, 0))
    kv_spec = pl.BlockSpec((seq_len, KV_W), lambda b, j, s: (b, 0))
    return pl.pallas_call(
        functools.partial(_attn_kernel, seq_len=seq_len),
        grid_spec=pltpu.PrefetchScalarGridSpec(
            num_scalar_prefetch=1,
            grid=(batch, nq),
            in_specs=[
                q_spec, kv_spec, kv_spec, q_spec,
                pl.BlockSpec((3, N_KV_HEADS, 2 * KEYS, 2 * LANES), lambda b, j, s: (0, 0, 0, 0)),
                pl.BlockSpec((1, D_ATTN), lambda b, j, s: (0, 0)),
            ],
            out_specs=q_spec,
            scratch_shapes=[
                pltpu.VMEM((N_KV_HEADS, seq_len, LANES), BF16),
                pltpu.VMEM((N_KV_HEADS, seq_len, LANES), BF16),
                pltpu.VMEM((N_KV_HEADS, nb, 2 * LANES, BLOCK), BF16),
                pltpu.VMEM((N_KV_HEADS, nb, 2 * LANES, BLOCK), BF16),
                pltpu.VMEM((BLOCK, D_ATTN), F32),
            ],
        ),
        out_shape=jax.ShapeDtypeStruct((batch * seq_len, D_ATTN), BF16),
        compiler_params=pltpu.CompilerParams(
            dimension_semantics=("arbitrary", "arbitrary"), vmem_limit_bytes=VMEM_LIMIT),
        name="attn",
    )(sink, q, k, v, za, bias, attn_norm_w)


def _gelu_tanh(x):
    c = math.sqrt(2.0 / math.pi)
    half = 0.5 * x
    return half + half * jnp.tanh(x * (c + (c * 0.044715) * (x * x)))


def _sigmoid(x):
    return 0.5 * jnp.tanh(0.5 * x) + 0.5


def _silu(x):
    return x * _sigmoid(x)


def _post_kernel(x_ref, yp_ref, zs_ref, na_ref, wglu_ref, bglu_ref, snw_ref, wout_ref, fnw_ref, o_ref,
                 y_scr, b_scr, t_scr):
    s = pl.program_id(2)

    @pl.when(s == 0)
    def _():
        for g in range(N_GROUPS):
            j, g3 = divmod(g, GROUPS_PER_LANE_TILE)
            for th in range(CHUNK_W // LANES):
                tt = yp_ref[g, :, pl.ds(th * LANES, LANES)].T
                for t3 in range(8):
                    t_scr[th * 8 + t3, j, pl.ds(g3 * SSM_GROUP, SSM_GROUP), :] = (
                        tt[t3 * SSM_GROUP:(t3 + 1) * SSM_GROUP, :])
        for t in range(CHUNK):
            for j in range(LANE_TILES):
                b_scr[t, :, pl.ds(j * LANES, LANES)] = t_scr[t, j].T
        perm = _pair_perm(t_major_out=False)
        for cp in range(TILE_CHUNKS // 2):
            a = jnp.concatenate(
                [b_scr[t, pl.ds(cp * 2 * SEQ_TILE, 2 * SEQ_TILE), :] for t in range(CHUNK)], axis=0)
            ys = jnp.dot(perm, a, preferred_element_type=F32)
            for c2 in range(2):
                for sq in range(SEQ_TILE):
                    r0 = (c2 * SEQ_TILE + sq) * CHUNK
                    y_scr[pl.ds(sq * TOK_TILE + (2 * cp + c2) * CHUNK, CHUNK), :] = ys[r0:r0 + CHUNK, :]

    def get(ref):
        return ref[...].reshape(SUB, ref.shape[-1]).astype(F32)

    y = y_scr[pl.ds(pl.multiple_of(s * SUB, SUB), SUB), :]
    g = _gelu_tanh(y)
    gate = _sigmoid(jnp.dot(g.astype(BF16), wglu_ref[...], preferred_element_type=F32) + bglu_ref[...])
    n_ssm = _rms(g * gate, snw_ref[...]) * _silu(get(zs_ref))
    mixed = jnp.concatenate([n_ssm.astype(BF16), na_ref[...].reshape(SUB, D_ATTN)], axis=1)
    res = get(x_ref) + jnp.dot(mixed, wout_ref[...], preferred_element_type=F32)
    o_ref[...] = _rms(res, fnw_ref[...]).reshape(o_ref.shape)


def _post_call(x, yp, zs, n_attn, w_glu, b_glu, ssm_norm_w, w_out, final_norm_w):
    batch, seq_len, _ = x.shape
    nsb, ncb = batch // SEQ_TILE, seq_len // TOK_TILE
    return pl.pallas_call(
        _post_kernel,
        grid=(nsb, ncb, N_SUB),
        in_specs=[
            _tile_specs(D_MODEL), _CHUNK_TILE_SPEC, _tile_specs(D_SSM), _tile_specs(D_ATTN),
            _const_spec((D_SSM, D_SSM)), _const_spec((1, D_SSM)), _const_spec((1, D_SSM)),
            _const_spec((D_MODEL, D_MODEL)), _const_spec((1, D_MODEL)),
        ],
        out_specs=_tile_specs(D_MODEL),
        out_shape=jax.ShapeDtypeStruct((batch, seq_len, D_MODEL), F32),
        scratch_shapes=[
            pltpu.VMEM((TILE_ROWS, D_SSM), F32),
            pltpu.VMEM((CHUNK, LANES, D_SSM), BF16),
            pltpu.VMEM((CHUNK, LANE_TILES, LANES, LANES), BF16),
        ],
        compiler_params=pltpu.CompilerParams(
            dimension_semantics=("arbitrary", "arbitrary", "arbitrary"), vmem_limit_bytes=VMEM_LIMIT),
        name="post",
    )(x, yp, zs, n_attn, w_glu, b_glu, ssm_norm_w, w_out, final_norm_w)


def _cmul(ar, ai, br, bi):
    return ar * br - ai * bi, ar * bi + ai * br


def _split_bf16(x):
    hi = x.astype(BF16)
    return hi, (x - hi.astype(F32)).astype(BF16)


def _dot_split(a, b):
    a_hi, a_lo = _split_bf16(a)
    b_hi, b_lo = _split_bf16(b)
    dot = functools.partial(jnp.dot, preferred_element_type=F32)
    return dot(a_hi, b_hi) + (dot(a_hi, b_lo) + dot(a_lo, b_hi))


def _ops_kernel(row_ref, col_ref, bt_ref, ct_ref, dsk_ref, toep_ref, bs_ref, cs_ref, a_ref):
    def discretise(lr, li, ls):
        dt = jnp.exp(ls)
        mag = jnp.exp(lr * dt)
        return mag * jnp.cos(li * dt), mag * jnp.sin(li * dt)

    lr, li = row_ref[0], row_ref[1]
    lb_re, lb_im = discretise(lr, li, row_ref[2])
    den = lr * lr + li * li
    nr = lb_re - 1.0
    coef_re = (nr * lr + lb_im * li) / den
    coef_im = (lb_im * lr - nr * li) / den
    bb_re, bb_im = _cmul(coef_re, coef_im, bt_ref[0], bt_ref[1])
    bb_re = jnp.concatenate([bb_re] * CHUNK, axis=0)
    bb_im = jnp.concatenate([bb_im] * CHUNK, axis=0)

    row_t = lax.broadcasted_iota(jnp.int32, (CHUNK_W, LANES), 0) // SSM_GROUP
    fwd_lane = lax.broadcasted_iota(jnp.int32, (CHUNK_W, LANES), 1) < SSM_STATE
    col_t = lax.broadcasted_iota(jnp.int32, (LANES, CHUNK_W), 1) // SSM_GROUP
    fwd_row = lax.broadcasted_iota(jnp.int32, (LANES, CHUNK_W), 0) < SSM_STATE
    ct_re, ct_im = ct_ref[0], ct_ref[1]
    w_re = jnp.zeros((CHUNK_W, LANES), F32)
    w_im = jnp.zeros((CHUNK_W, LANES), F32)
    toep = jnp.zeros((CHUNK_W, CHUNK_W), F32)
    for i in range(CHUNK):
        t_now = jnp.where(fwd_lane, i, CHUNK - 1 - i)
        w_re, w_im = _cmul(w_re, w_im, lb_re, lb_im)
        w_re = w_re + jnp.where(row_t == t_now, bb_re, 0.0)
        w_im = w_im + jnp.where(row_t == t_now, bb_im, 0.0)
        out_now = col_t == jnp.where(fwd_row, i, CHUNK - 1 - i)
        rhs = jnp.concatenate([jnp.where(out_now, ct_re, 0.0), jnp.where(out_now, -ct_im, 0.0)], axis=0)
        toep = toep + _dot_split(jnp.concatenate([w_re, w_im], axis=1), rhs)
    r = lax.broadcasted_iota(jnp.int32, (CHUNK_W, CHUNK_W), 0)
    c = lax.broadcasted_iota(jnp.int32, (CHUNK_W, CHUNK_W), 1)
    toep_ref[...] = (toep + jnp.where(r == c, dsk_ref[...], 0.0)).astype(BF16)
    bs_ref[...] = jnp.concatenate([w_re, w_im], axis=1).astype(BF16)

    pr, pi = discretise(col_ref[0], col_ref[1], col_ref[2])
    k = jnp.where(fwd_row, col_t + 1, CHUNK - col_t)
    acc_re = jnp.ones((LANES, CHUNK_W), F32)
    acc_im = jnp.zeros((LANES, CHUNK_W), F32)
    for bit in range(CHUNK.bit_length()):
        nre, nim = _cmul(acc_re, acc_im, pr, pi)
        take = (k & (1 << bit)) != 0
        acc_re = jnp.where(take, nre, acc_re)
        acc_im = jnp.where(take, nim, acc_im)
        pr, pi = _cmul(pr, pi, pr, pi)
    cp_re, cp_im = _cmul(ct_re, ct_im, acc_re, acc_im)
    cs_ref[...] = jnp.concatenate([cp_re, -cp_im], axis=0).astype(BF16)

    a_re, a_im = lb_re, lb_im
    for _ in range(CHUNK.bit_length() - 1):
        a_re, a_im = _cmul(a_re, a_im, a_re, a_im)
    a_ref[0] = a_re
    a_ref[1] = a_im


def _ssm_operators(lam_re, lam_im, log_step, b_re, b_im, c_re, c_im, d_skip):
    g_first = lambda a: jnp.transpose(a.astype(F32), (1, 0, 2)).reshape(N_GROUPS, 2 * SSM_STATE)
    params = jnp.stack([g_first(lam_re), g_first(lam_im),
                        jnp.repeat(log_step.astype(F32).T, SSM_STATE, axis=1)], axis=1)
    b_t = lambda b: jnp.transpose(b.astype(F32), (1, 3, 0, 2)).reshape(N_GROUPS, SSM_GROUP, 2 * SSM_STATE)
    c_t = lambda c: jnp.tile(
        jnp.transpose(c.astype(F32), (1, 0, 3, 2)).reshape(N_GROUPS, 2 * SSM_STATE, SSM_GROUP), (1, 1, CHUNK))
    bt = jnp.stack([b_t(b_re), b_t(b_im)], axis=1)
    ct = jnp.stack([c_t(c_re), c_t(c_im)], axis=1)
    dsk = jnp.tile(d_skip.astype(F32), (1, CHUNK))[:, None, :]
    grp = lambda *shape: pl.BlockSpec((None,) + shape, lambda g: (g,) + (0,) * len(shape))
    mat = jax.ShapeDtypeStruct((N_GROUPS, CHUNK_W, CHUNK_W), BF16)
    toep, bs, cs, a = pl.pallas_call(
        _ops_kernel,
        grid=(N_GROUPS,),
        in_specs=[grp(3, 1, LANES), grp(3, LANES, 1), grp(2, SSM_GROUP, LANES), grp(2, LANES, CHUNK_W),
                  grp(1, CHUNK_W)],
        out_specs=(grp(CHUNK_W, CHUNK_W), grp(CHUNK_W, STATE_W), grp(STATE_W, CHUNK_W), grp(2, 1, LANES)),
        out_shape=(mat, mat, mat, jax.ShapeDtypeStruct((N_GROUPS, 2, 1, LANES), F32)),
        compiler_params=pltpu.CompilerParams(dimension_semantics=("arbitrary",)),
        name="s5_ops",
    )(params[:, :, None, :], params[:, :, :, None], bt, ct, dsk)
    return toep, bs, cs, a[:, 0], a[:, 1]


def _t5_buckets_np(rel):
    half = N_BUCKETS // 2
    max_exact = half // 2
    ret = np.where(rel > 0, half, 0)
    n = np.abs(rel)
    nf = np.maximum(n, 1).astype(np.float64)
    large = max_exact + (np.log(nf / max_exact) / math.log(MAX_DISTANCE / max_exact)
                         * (half - max_exact)).astype(np.int32)
    large = np.minimum(large, half - 1)
    return ret + np.where(n < max_exact, n, large)


def _attn_tables(rel_bias):
    offsets = (0, -BLOCK, -2 * BLOCK)
    period = 4 * BLOCK
    dist = np.arange(period)
    dist = np.where(dist < KEYS, dist, dist - period)
    rel = np.stack([dist + off for off in offsets])
    per_dist = rel_bias.astype(F32)[_t5_buckets_np(rel)] * LOG2_E
    per_dist = jnp.where((np.abs(rel) <= WINDOW)[:, :, None], per_dist, NEG_INF)
    per_dist = jnp.transpose(per_dist, (0, 2, 1))
    neg = jnp.concatenate([per_dist[:, :, :1], per_dist[:, :, :0:-1]], axis=-1)
    skew = jnp.tile(neg, (1, 1, KEYS))[:, :, :KEYS * (period - 1)]
    bias = skew.reshape(3, N_HEADS, KEYS, period - 1)[:, :, :, :BLOCK]
    bias = bias.reshape(3, N_KV_HEADS, 2, 2, KEYS, BLOCK)
    bias = jnp.concatenate([bias[:, :, 0], bias[:, :, 1]], axis=-1)
    return bias.reshape(3, N_KV_HEADS, 2 * KEYS, 2 * LANES)


def kernel(x_prompt, x_sample, norm_w, w_in, lam_re, lam_im, log_step, b_re, b_im, c_re, c_im, d_skip,
           w_glu, b_glu, ssm_norm_w, sink, attn_norm_w, w_out, rel_bias, final_norm_w):
    assert norm_w.shape[0] == 1, "single-layer encoder"
    ops = _ssm_operators(lam_re[0], lam_im[0], log_step[0], b_re[0], b_im[0], c_re[0], c_im[0], d_skip[0])
    bias = _attn_tables(rel_bias)
    w_in_b = w_in[0].astype(BF16)
    w_glu_b = w_glu[0].astype(BF16)
    w_out_b = w_out[0].astype(BF16)
    sink_f = sink[0].astype(F32) * LOG2_E

    outs = []
    for x in (x_prompt, x_sample):
        batch, seq_len, _ = x.shape
        assert batch % SEQ_TILE == 0 and seq_len % TOK_TILE == 0
        n_tok = batch * seq_len
        up, zs, q, k, v, za = _proj_call(x, norm_w, w_in_b)
        yp = _ssm_call(up, ops)
        n_attn = _attn_call(q.reshape(n_tok, D_ATTN), k.reshape(n_tok, KV_W), v.reshape(n_tok, KV_W),
                            za.reshape(n_tok, D_ATTN), bias, sink_f, attn_norm_w, batch, seq_len)
        outs.append(_post_call(x, yp, zs, n_attn.reshape(batch, seq_len, D_ATTN), w_glu_b, b_glu, ssm_norm_w,
                               w_out_b, final_norm_w.reshape(1, D_MODEL)))
    return tuple(outs)
```

```python
import functools
import math

import numpy as np
import jax
import jax.numpy as jnp
from jax import lax
from jax.experimental import pallas as pl
from jax.experimental.pallas import tpu as pltpu

F32 = jnp.float32
BF16 = jnp.bfloat16

D_MODEL = 1024
D_SSM = 512
D_ATTN = 512
SSM_GROUP = 16
N_GROUPS = D_SSM // SSM_GROUP
SSM_STATE = 64
HEAD_DIM = 64
N_HEADS = D_ATTN // HEAD_DIM
N_KV_HEADS = 2
WINDOW = 128
BLOCK = 128
N_BUCKETS = 32
MAX_DISTANCE = 128
RMS_EPS = 1e-6
NEG_INF = -1e30
LOG2_E = math.log2(math.e)

LANES = 128
SUBLANES = 8
CHUNK = 16
CHUNK_W = CHUNK * SSM_GROUP
STATE_W = 4 * SSM_STATE
SEQ_TILE = SUBLANES
TILE_CHUNKS = LANES // SEQ_TILE
TOK_TILE = TILE_CHUNKS * CHUNK
TILE_ROWS = SEQ_TILE * TOK_TILE
SUB_SEQS = 4
SUB = SUB_SEQS * TOK_TILE
N_SUB = SEQ_TILE // SUB_SEQS
LANE_TILES = D_SSM // LANES
GROUPS_PER_LANE_TILE = LANES // SSM_GROUP
KEYS = 3 * BLOCK
Q_ROWS = 1024
BLOCKS_PER_ITER = 4
KV_W = N_KV_HEADS * HEAD_DIM
MM_ROWS = 512
VMEM_LIMIT = 56 * 1024 * 1024

_C_U, _C_ZS, _C_Q, _C_K, _C_V, _C_ZA, _C_END = 0, 512, 1024, 1536, 1664, 1792, 2304


def _rms(x, w):
    return (x * lax.rsqrt(jnp.mean(x * x, axis=-1, keepdims=True) + RMS_EPS)) * w


def _tile_specs(width):
    return pl.BlockSpec((SUB_SEQS, TOK_TILE, width), lambda sb, cb, s: (sb * N_SUB + s, cb, 0))


_CHUNK_TILE_SPEC = pl.BlockSpec((N_GROUPS, None, None, LANES, CHUNK_W), lambda sb, cb, s: (0, sb, cb, 0, 0))


def _const_spec(shape):
    return pl.BlockSpec(shape, lambda sb, cb, s: (0,) * len(shape))


PAIR_ROWS = 2 * SEQ_TILE * CHUNK


def _pair_perm(t_major_out):
    out_row = lax.broadcasted_iota(jnp.int32, (PAIR_ROWS, PAIR_ROWS), 0)
    in_row = lax.broadcasted_iota(jnp.int32, (PAIR_ROWS, PAIR_ROWS), 1)
    tcs, cst = (out_row, in_row) if t_major_out else (in_row, out_row)
    hit = (tcs // (2 * SEQ_TILE) == cst % CHUNK) & (tcs % (2 * SEQ_TILE) == cst // CHUNK)
    return jnp.where(hit, 1.0, 0.0).astype(BF16)


def _proj_kernel(x_ref, nw_ref, w_ref, up_ref, zs_ref, q_ref, k_ref, v_ref, za_ref, u_scr, b_scr, t_scr):
    s = pl.program_id(2)
    hb = _rms(x_ref[...].reshape(SUB, D_MODEL), nw_ref[...]).astype(BF16)

    def mm(lo, hi):
        return jnp.dot(hb, w_ref[:, lo:hi], preferred_element_type=F32)

    def put(ref, val):
        ref[...] = val.astype(BF16).reshape(ref.shape)

    uz = mm(_C_U, _C_Q)
    u_scr[pl.ds(pl.multiple_of(s * SUB, SUB), SUB), :] = uz[:, :D_SSM].astype(BF16)
    put(zs_ref, uz[:, D_SSM:])
    qkv = mm(_C_Q, _C_ZA)
    put(q_ref, qkv[:, :D_ATTN] * (HEAD_DIM ** -0.5 * LOG2_E))
    put(k_ref, qkv[:, D_ATTN:D_ATTN + KV_W])
    put(v_ref, qkv[:, D_ATTN + KV_W:])
    put(za_ref, mm(_C_ZA, _C_END))

    @pl.when(s == N_SUB - 1)
    def _():
        perm = _pair_perm(t_major_out=True)
        for cp in range(TILE_CHUNKS // 2):
            a = jnp.concatenate(
                [u_scr[pl.ds(sq * TOK_TILE + (2 * cp + c2) * CHUNK, CHUNK), :]
                 for c2 in range(2) for sq in range(SEQ_TILE)], axis=0)
            b_scr[cp] = jnp.dot(perm, a, preferred_element_type=F32).astype(BF16)
        for t in range(CHUNK):
            for j in range(LANE_TILES):
                blk = jnp.concatenate(
                    [b_scr[cp, pl.ds(t * 2 * SEQ_TILE, 2 * SEQ_TILE), pl.ds(j * LANES, LANES)]
                     for cp in range(TILE_CHUNKS // 2)], axis=0)
                t_scr[t, j] = blk.T
        for g in range(N_GROUPS):
            j, g3 = divmod(g, GROUPS_PER_LANE_TILE)
            for th in range(CHUNK_W // LANES):
                rows = jnp.concatenate(
                    [t_scr[th * 8 + t3, j, pl.ds(g3 * SSM_GROUP, SSM_GROUP), :] for t3 in range(8)], axis=0)
                up_ref[g, :, pl.ds(th * LANES, LANES)] = rows.T


def _proj_call(x, norm_w, w_in):
    batch, seq_len, _ = x.shape
    nsb, ncb = batch // SEQ_TILE, seq_len // TOK_TILE
    tok = lambda width: jax.ShapeDtypeStruct((batch, seq_len, width), BF16)
    out_shapes = (
        jax.ShapeDtypeStruct((N_GROUPS, nsb, ncb, LANES, CHUNK_W), BF16),
        tok(D_SSM),
        tok(D_ATTN),
        tok(KV_W),
        tok(KV_W),
        tok(D_ATTN),
    )
    return pl.pallas_call(
        _proj_kernel,
        grid=(nsb, ncb, N_SUB),
        in_specs=[_tile_specs(D_MODEL), _const_spec((1, D_MODEL)), _const_spec((D_MODEL, _C_END))],
        out_specs=(_CHUNK_TILE_SPEC, _tile_specs(D_SSM), _tile_specs(D_ATTN), _tile_specs(KV_W),
                   _tile_specs(KV_W), _tile_specs(D_ATTN)),
        out_shape=out_shapes,
        scratch_shapes=[
            pltpu.VMEM((TILE_ROWS, D_SSM), BF16),
            pltpu.VMEM((TILE_CHUNKS // 2, PAIR_ROWS, D_SSM), BF16),
            pltpu.VMEM((CHUNK, LANE_TILES, LANES, LANES), BF16),
        ],
        compiler_params=pltpu.CompilerParams(
            dimension_semantics=("arbitrary", "arbitrary", "arbitrary"), vmem_limit_bytes=VMEM_LIMIT),
        name="proj",
    )(x, norm_w, w_in)


def _ssm_kernel(u_ref, toep_ref, bs_ref, cs_ref, are_ref, aim_ref, y_ref, s_scr, xf_scr, xb_scr,
                *, nsb, ncb):
    nch = ncb * TILE_CHUNKS
    tiles_per_mm = MM_ROWS // LANES
    seq_rows = nsb * SEQ_TILE

    def mm_blocks():
        for sb in range(nsb):
            for cq in range(ncb // tiles_per_mm):
                yield sb, cq * tiles_per_mm, pl.ds((sb * ncb + cq * tiles_per_mm) * LANES, MM_ROWS)

    def load_u(sb, cb0):
        return jnp.concatenate([u_ref[sb, cb0 + i] for i in range(tiles_per_mm)], axis=0)

    for sb, cb0, rows in mm_blocks():
        st = jnp.dot(load_u(sb, cb0), bs_ref[...], preferred_element_type=F32)
        s_scr[0, rows, :] = st[:, :LANES]
        s_scr[1, rows, :] = st[:, LANES:]

    def chunk_rows(sb, ch):
        return pl.ds(pl.multiple_of(sb * nch * SEQ_TILE + ch * SEQ_TILE, SEQ_TILE), SEQ_TILE)

    def load(ref, part, ch):
        return jnp.concatenate([ref[part, chunk_rows(sb, ch), :] for sb in range(nsb)], axis=0)

    def store(ref, part, ch, val):
        for sb in range(nsb):
            ref[part, chunk_rows(sb, ch), :] = val[sb * SEQ_TILE:(sb + 1) * SEQ_TILE, :]

    are = jnp.broadcast_to(are_ref[...], (seq_rows, LANES))
    aim = jnp.broadcast_to(aim_ref[...], (seq_rows, LANES))
    is_fwd = lax.broadcasted_iota(jnp.int32, (seq_rows, LANES), 1) < SSM_STATE

    def step(i, carry):
        xr, xi = carry
        j = nch - 1 - i
        store(xf_scr, 0, i, xr)
        store(xf_scr, 1, i, xi)
        store(xb_scr, 0, j, xr)
        store(xb_scr, 1, j, xi)
        s_re = jnp.where(is_fwd, load(s_scr, 0, i), load(s_scr, 0, j))
        s_im = jnp.where(is_fwd, load(s_scr, 1, i), load(s_scr, 1, j))
        return (are * xr - aim * xi + s_re, are * xi + aim * xr + s_im)

    zero = jnp.zeros((seq_rows, LANES), F32)
    lax.fori_loop(0, nch, step, (zero, zero), unroll=4)

    fwd_lane = lax.broadcasted_iota(jnp.int32, (MM_ROWS, LANES), 1) < SSM_STATE
    for sb, cb0, rows in mm_blocks():
        xin = jnp.concatenate(
            [jnp.where(fwd_lane, xf_scr[c, rows, :], xb_scr[c, rows, :]) for c in range(2)], axis=1).astype(BF16)
        y = (jnp.dot(load_u(sb, cb0), toep_ref[...], preferred_element_type=F32)
             + jnp.dot(xin, cs_ref[...], preferred_element_type=F32))
        for i in range(tiles_per_mm):
            y_ref[sb, cb0 + i] = y[i * LANES:(i + 1) * LANES, :].astype(BF16)


def _ssm_call(up, ops):
    toep, bs, cs, are, aim = ops
    _, nsb, ncb, _, _ = up.shape
    n_rows = nsb * ncb * LANES
    assert ncb % (MM_ROWS // LANES) == 0
    data_spec = pl.BlockSpec((None, nsb, ncb, LANES, CHUNK_W), lambda g: (g, 0, 0, 0, 0))
    grp = lambda *shape: pl.BlockSpec((None,) + shape, lambda g: (g, 0, 0))
    return pl.pallas_call(
        functools.partial(_ssm_kernel, nsb=nsb, ncb=ncb),
        grid=(N_GROUPS,),
        in_specs=[data_spec, grp(CHUNK_W, CHUNK_W), grp(CHUNK_W, STATE_W), grp(STATE_W, CHUNK_W),
                  grp(1, LANES), grp(1, LANES)],
        out_specs=data_spec,
        out_shape=jax.ShapeDtypeStruct(up.shape, BF16),
        scratch_shapes=[pltpu.VMEM((2, n_rows, LANES), F32)] * 3,
        compiler_params=pltpu.CompilerParams(
            dimension_semantics=("arbitrary",), vmem_limit_bytes=VMEM_LIMIT),
        name="ssm",
    )(up, toep, bs, cs, are, aim)


def _attn_kernel(sink_ref, q_ref, k_ref, v_ref, bias_ref, o_ref, ka_scr, kb_scr, wa_scr, wb_scr, *, seq_len):
    jb = pl.program_id(1)
    nb = seq_len // BLOCK

    @pl.when(jb == 0)
    def _():
        low = lax.broadcasted_iota(jnp.int32, (BLOCK, LANES), 1) < HEAD_DIM
        zero_rows = jnp.zeros((HEAD_DIM, BLOCK), BF16)
        one_rows = jnp.ones((HEAD_DIM, BLOCK), BF16)

        def per_block(r, carry):
            rows = pl.ds(pl.multiple_of(r * BLOCK, BLOCK), BLOCK)
            k = k_ref[rows, :]
            k_swapped = pltpu.roll(k, HEAD_DIM, 1)
            zero = jnp.zeros_like(k)
            ka_scr[0, rows, :] = jnp.where(low, k, zero)
            kb_scr[0, rows, :] = jnp.where(low, zero, k_swapped)
            ka_scr[1, rows, :] = jnp.where(low, k_swapped, zero)
            kb_scr[1, rows, :] = jnp.where(low, zero, k)
            vt = v_ref[rows, :].T
            for kh in range(N_KV_HEADS):
                vt_kh = vt[kh * HEAD_DIM:(kh + 1) * HEAD_DIM]
                wa_scr[kh, r] = jnp.concatenate([vt_kh, zero_rows, one_rows, zero_rows], axis=0)
                wb_scr[kh, r] = jnp.concatenate([zero_rows, vt_kh, zero_rows, one_rows], axis=0)
            return carry

        lax.fori_loop(0, nb, per_block, 0)

    col_tile0 = lax.broadcasted_iota(jnp.int32, (1, 2 * LANES), 1) < LANES
    row_slot0 = lax.broadcasted_iota(jnp.int32, (BLOCK, 2 * LANES), 0) < HEAD_DIM
    def query_block(sb):
        n = jb * (Q_ROWS // BLOCK) + sb
        kb0 = jnp.clip(n - 1, 0, nb - KEYS // BLOCK)
        ks = pl.multiple_of(kb0 * BLOCK, BLOCK)
        var = jnp.where(n == 0, 0, jnp.where(n == nb - 1, 2, 1))
        qrows = pl.ds(pl.multiple_of(sb * BLOCK, BLOCK), BLOCK)
        for kh in range(N_KV_HEADS):
            kcat = jnp.concatenate([ka_scr[kh, pl.ds(ks, KEYS), :], kb_scr[kh, pl.ds(ks, KEYS), :]], axis=0)
            q2 = jnp.concatenate([q_ref[qrows, pl.ds((2 * kh + jj) * LANES, LANES)] for jj in range(2)], axis=0)
            t = lax.dot_general(kcat, q2, (((1,), (1,)), ((), ())), preferred_element_type=F32)
            t = t + bias_ref[var, kh]
            ps, sink_terms = [], []
            for hh in range(2):
                th = t[hh * KEYS:(hh + 1) * KEYS]
                sink = jnp.where(col_tile0, sink_ref[4 * kh + hh], sink_ref[4 * kh + 2 + hh])
                m = jnp.maximum(jnp.max(th, axis=0, keepdims=True), sink)
                ps.append(jnp.exp2(th - m).astype(BF16))
                sink_terms.append(jnp.exp2(sink - m))
            w = jnp.concatenate([wa_scr[kh, kb0 + i] for i in range(KEYS // BLOCK)]
                                + [wb_scr[kh, kb0 + i] for i in range(KEYS // BLOCK)], axis=1)
            o2 = jnp.dot(w, jnp.concatenate(ps, axis=0), preferred_element_type=F32)
            denom = o2[LANES:] + jnp.where(row_slot0, sink_terms[0], sink_terms[1])
            on = o2[:LANES] / denom
            for jj in range(2):
                o_ref[qrows, pl.ds((2 * kh + jj) * LANES, LANES)] = (
                    on[:, jj * LANES:(jj + 1) * LANES].T.astype(BF16))

    def query_blocks(i, carry):
        for sb in range(BLOCKS_PER_ITER):
            query_block(i * BLOCKS_PER_ITER + sb)
        return carry

    lax.fori_loop(0, Q_ROWS // (BLOCKS_PER_ITER * BLOCK), query_blocks, 0)


def _attn_call(q, k, v, bias, sink, batch, seq_len):
    assert seq_len % Q_ROWS == 0 and seq_len >= KEYS
    nq = seq_len // Q_ROWS
    nb = seq_len // BLOCK
    return pl.pallas_call(
        functools.partial(_attn_kernel, seq_len=seq_len),
        grid_spec=pltpu.PrefetchScalarGridSpec(
            num_scalar_prefetch=1,
            grid=(batch, nq),
            in_specs=[
                pl.BlockSpec((Q_ROWS, D_ATTN), lambda b, j, s: (b * nq + j, 0)),
                pl.BlockSpec((seq_len, KV_W), lambda b, j, s: (b, 0)),
                pl.BlockSpec((seq_len, KV_W), lambda b, j, s: (b, 0)),
                pl.BlockSpec((3, N_KV_HEADS, 2 * KEYS, 2 * LANES), lambda b, j, s: (0, 0, 0, 0)),
            ],
            out_specs=pl.BlockSpec((Q_ROWS, D_ATTN), lambda b, j, s: (b * nq + j, 0)),
            scratch_shapes=[
                pltpu.VMEM((N_KV_HEADS, seq_len, LANES), BF16),
                pltpu.VMEM((N_KV_HEADS, seq_len, LANES), BF16),
                pltpu.VMEM((N_KV_HEADS, nb, 2 * LANES, BLOCK), BF16),
                pltpu.VMEM((N_KV_HEADS, nb, 2 * LANES, BLOCK), BF16),
            ],
        ),
        out_shape=jax.ShapeDtypeStruct((batch * seq_len, D_ATTN), BF16),
        compiler_params=pltpu.CompilerParams(
            dimension_semantics=("arbitrary", "arbitrary"), vmem_limit_bytes=VMEM_LIMIT),
        name="attn",
    )(sink, q, k, v, bias)


def _gelu_tanh(x):
    c = math.sqrt(2.0 / math.pi)
    half = 0.5 * x
    return half + half * jnp.tanh(x * (c + (c * 0.044715) * (x * x)))


def _sigmoid(x):
    return 0.5 * jnp.tanh(0.5 * x) + 0.5


def _silu(x):
    return x * _sigmoid(x)


def _post_kernel(x_ref, yp_ref, zs_ref, ao_ref, za_ref, wglu_ref, bglu_ref, snw_ref, anw_ref,
                 wout_ref, fnw_ref, o_ref, y_scr, b_scr, t_scr):
    s = pl.program_id(2)

    @pl.when(s == 0)
    def _():
        for g in range(N_GROUPS):
            j, g3 = divmod(g, GROUPS_PER_LANE_TILE)
            for th in range(CHUNK_W // LANES):
                tt = yp_ref[g, :, pl.ds(th * LANES, LANES)].T
                for t3 in range(8):
                    t_scr[th * 8 + t3, j, pl.ds(g3 * SSM_GROUP, SSM_GROUP), :] = (
                        tt[t3 * SSM_GROUP:(t3 + 1) * SSM_GROUP, :])
        for t in range(CHUNK):
            for j in range(LANE_TILES):
                b_scr[t, :, pl.ds(j * LANES, LANES)] = t_scr[t, j].T
        perm = _pair_perm(t_major_out=False)
        for cp in range(TILE_CHUNKS // 2):
            a = jnp.concatenate(
                [b_scr[t, pl.ds(cp * 2 * SEQ_TILE, 2 * SEQ_TILE), :] for t in range(CHUNK)], axis=0)
            ys = jnp.dot(perm, a, preferred_element_type=F32)
            for c2 in range(2):
                for sq in range(SEQ_TILE):
                    r0 = (c2 * SEQ_TILE + sq) * CHUNK
                    y_scr[pl.ds(sq * TOK_TILE + (2 * cp + c2) * CHUNK, CHUNK), :] = ys[r0:r0 + CHUNK, :]

    def get(ref):
        return ref[...].reshape(SUB, ref.shape[-1]).astype(F32)

    y = y_scr[pl.ds(pl.multiple_of(s * SUB, SUB), SUB), :]
    g = _gelu_tanh(y)
    gate = _sigmoid(jnp.dot(g.astype(BF16), wglu_ref[...], preferred_element_type=F32) + bglu_ref[...])
    n_ssm = _rms(g * gate, snw_ref[...]) * _silu(get(zs_ref))
    n_attn = _rms(get(ao_ref), anw_ref[...]) * _silu(get(za_ref))
    mixed = jnp.concatenate([n_ssm, n_attn], axis=1).astype(BF16)
    res = get(x_ref) + jnp.dot(mixed, wout_ref[...], preferred_element_type=F32)
    o_ref[...] = _rms(res, fnw_ref[...]).reshape(o_ref.shape)


def _post_call(x, yp, zs, ao, za, w_glu, b_glu, ssm_norm_w, attn_norm_w, w_out, final_norm_w):
    batch, seq_len, _ = x.shape
    nsb, ncb = batch // SEQ_TILE, seq_len // TOK_TILE
    return pl.pallas_call(
        _post_kernel,
        grid=(nsb, ncb, N_SUB),
        in_specs=[
            _tile_specs(D_MODEL), _CHUNK_TILE_SPEC, _tile_specs(D_SSM), _tile_specs(D_ATTN), _tile_specs(D_ATTN),
            _const_spec((D_SSM, D_SSM)), _const_spec((1, D_SSM)), _const_spec((1, D_SSM)),
            _const_spec((1, D_ATTN)), _const_spec((D_MODEL, D_MODEL)), _const_spec((1, D_MODEL)),
        ],
        out_specs=_tile_specs(D_MODEL),
        out_shape=jax.ShapeDtypeStruct((batch, seq_len, D_MODEL), F32),
        scratch_shapes=[
            pltpu.VMEM((TILE_ROWS, D_SSM), F32),
            pltpu.VMEM((CHUNK, LANES, D_SSM), BF16),
            pltpu.VMEM((CHUNK, LANE_TILES, LANES, LANES), BF16),
        ],
        compiler_params=pltpu.CompilerParams(
            dimension_semantics=("arbitrary", "arbitrary", "arbitrary"), vmem_limit_bytes=VMEM_LIMIT),
        name="post",
    )(x, yp, zs, ao, za, w_glu, b_glu, ssm_norm_w, attn_norm_w, w_out, final_norm_w)


def _cmul(ar, ai, br, bi):
    return ar * br - ai * bi, ar * bi + ai * br


def _ops_kernel(row_ref, col_ref, bt_ref, ct_ref, dsk_ref, toep_ref, bs_ref, cs_ref, a_ref):
    def discretise(lr, li, ls):
        dt = jnp.exp(ls)
        mag = jnp.exp(lr * dt)
        return mag * jnp.cos(li * dt), mag * jnp.sin(li * dt)

    lr, li = row_ref[0], row_ref[1]
    lb_re, lb_im = discretise(lr, li, row_ref[2])
    den = lr * lr + li * li
    nr = lb_re - 1.0
    coef_re = (nr * lr + lb_im * li) / den
    coef_im = (lb_im * lr - nr * li) / den
    bb_re, bb_im = _cmul(coef_re, coef_im, bt_ref[0], bt_ref[1])
    bb_re = jnp.concatenate([bb_re] * CHUNK, axis=0)
    bb_im = jnp.concatenate([bb_im] * CHUNK, axis=0)

    row_t = lax.broadcasted_iota(jnp.int32, (CHUNK_W, LANES), 0) // SSM_GROUP
    fwd_lane = lax.broadcasted_iota(jnp.int32, (CHUNK_W, LANES), 1) < SSM_STATE
    col_t = lax.broadcasted_iota(jnp.int32, (LANES, CHUNK_W), 1) // SSM_GROUP
    fwd_row = lax.broadcasted_iota(jnp.int32, (LANES, CHUNK_W), 0) < SSM_STATE
    ct_re, ct_im = ct_ref[0], ct_ref[1]
    w_re = jnp.zeros((CHUNK_W, LANES), F32)
    w_im = jnp.zeros((CHUNK_W, LANES), F32)
    toep = jnp.zeros((CHUNK_W, CHUNK_W), F32)
    for i in range(CHUNK):
        t_now = jnp.where(fwd_lane, i, CHUNK - 1 - i)
        w_re, w_im = _cmul(w_re, w_im, lb_re, lb_im)
        w_re = w_re + jnp.where(row_t == t_now, bb_re, 0.0)
        w_im = w_im + jnp.where(row_t == t_now, bb_im, 0.0)
        out_now = col_t == jnp.where(fwd_row, i, CHUNK - 1 - i)
        rhs = jnp.concatenate([jnp.where(out_now, ct_re, 0.0), jnp.where(out_now, -ct_im, 0.0)], axis=0)
        lhs = jnp.concatenate([w_re, w_im], axis=1)
        toep = toep + jnp.dot(lhs.astype(BF16), rhs.astype(BF16), preferred_element_type=F32)
    r = lax.broadcasted_iota(jnp.int32, (CHUNK_W, CHUNK_W), 0)
    c = lax.broadcasted_iota(jnp.int32, (CHUNK_W, CHUNK_W), 1)
    toep_ref[...] = (toep + jnp.where(r == c, dsk_ref[...], 0.0)).astype(BF16)
    bs_ref[...] = jnp.concatenate([w_re, w_im], axis=1).astype(BF16)

    pr, pi = discretise(col_ref[0], col_ref[1], col_ref[2])
    k = jnp.where(fwd_row, col_t + 1, CHUNK - col_t)
    acc_re = jnp.ones((LANES, CHUNK_W), F32)
    acc_im = jnp.zeros((LANES, CHUNK_W), F32)
    for bit in range(CHUNK.bit_length()):
        nre, nim = _cmul(acc_re, acc_im, pr, pi)
        take = (k & (1 << bit)) != 0
        acc_re = jnp.where(take, nre, acc_re)
        acc_im = jnp.where(take, nim, acc_im)
        pr, pi = _cmul(pr, pi, pr, pi)
    cp_re, cp_im = _cmul(ct_re, ct_im, acc_re, acc_im)
    cs_ref[...] = jnp.concatenate([cp_re, -cp_im], axis=0).astype(BF16)

    a_re, a_im = lb_re, lb_im
    for _ in range(CHUNK.bit_length() - 1):
        a_re, a_im = _cmul(a_re, a_im, a_re, a_im)
    a_ref[0] = a_re
    a_ref[1] = a_im


def _ssm_operators(lam_re, lam_im, log_step, b_re, b_im, c_re, c_im, d_skip):
    g_first = lambda a: jnp.transpose(a.astype(F32), (1, 0, 2)).reshape(N_GROUPS, 2 * SSM_STATE)
    params = jnp.stack([g_first(lam_re), g_first(lam_im),
                        jnp.repeat(log_step.astype(F32).T, SSM_STATE, axis=1)], axis=1)
    b_t = lambda b: jnp.transpose(b.astype(F32), (1, 3, 0, 2)).reshape(N_GROUPS, SSM_GROUP, 2 * SSM_STATE)
    c_t = lambda c: jnp.tile(
        jnp.transpose(c.astype(F32), (1, 0, 3, 2)).reshape(N_GROUPS, 2 * SSM_STATE, SSM_GROUP), (1, 1, CHUNK))
    bt = jnp.stack([b_t(b_re), b_t(b_im)], axis=1)
    ct = jnp.stack([c_t(c_re), c_t(c_im)], axis=1)
    dsk = jnp.tile(d_skip.astype(F32), (1, CHUNK))[:, None, :]
    grp = lambda *shape: pl.BlockSpec((None,) + shape, lambda g: (g,) + (0,) * len(shape))
    mat = jax.ShapeDtypeStruct((N_GROUPS, CHUNK_W, CHUNK_W), BF16)
    toep, bs, cs, a = pl.pallas_call(
        _ops_kernel,
        grid=(N_GROUPS,),
        in_specs=[grp(3, 1, LANES), grp(3, LANES, 1), grp(2, SSM_GROUP, LANES), grp(2, LANES, CHUNK_W),
                  grp(1, CHUNK_W)],
        out_specs=(grp(CHUNK_W, CHUNK_W), grp(CHUNK_W, STATE_W), grp(STATE_W, CHUNK_W), grp(2, 1, LANES)),
        out_shape=(mat, mat, mat, jax.ShapeDtypeStruct((N_GROUPS, 2, 1, LANES), F32)),
        compiler_params=pltpu.CompilerParams(dimension_semantics=("arbitrary",)),
        name="s5_ops",
    )(params[:, :, None, :], params[:, :, :, None], bt, ct, dsk)
    return toep, bs, cs, a[:, 0], a[:, 1]


def _t5_buckets_np(rel):
    half = N_BUCKETS // 2
    max_exact = half // 2
    ret = np.where(rel > 0, half, 0)
    n = np.abs(rel)
    nf = np.maximum(n, 1).astype(np.float64)
    large = max_exact + (np.log(nf / max_exact) / math.log(MAX_DISTANCE / max_exact)
                         * (half - max_exact)).astype(np.int32)
    large = np.minimum(large, half - 1)
    return ret + np.where(n < max_exact, n, large)


def _attn_tables(rel_bias):
    offsets = (0, -BLOCK, -2 * BLOCK)
    dist = np.arange(BIAS_PERIOD)
    dist = np.where(dist < KEYS, dist, dist - BIAS_PERIOD)
    rel = np.stack([dist + off for off in offsets])
    per_dist = rel_bias.astype(F32)[_t5_buckets_np(rel)] * LOG2_E
    per_dist = jnp.where((np.abs(rel) <= WINDOW)[:, :, None], per_dist, NEG_INF)
    per_dist = jnp.transpose(per_dist, (0, 2, 1))
    neg = jnp.concatenate([per_dist[:, :, :1], per_dist[:, :, :0:-1]], axis=-1)
    return pl.pallas_call(
        _bias_kernel,
        grid=(len(offsets),),
        in_specs=[pl.BlockSpec((None, N_HEADS, 1, BIAS_PERIOD), lambda v: (v, 0, 0, 0))],
        out_specs=pl.BlockSpec((None, N_KV_HEADS, 2 * KEYS, 2 * LANES), lambda v: (v, 0, 0, 0)),
        out_shape=jax.ShapeDtypeStruct((len(offsets), N_KV_HEADS, 2 * KEYS, 2 * LANES), F32),
        compiler_params=pltpu.CompilerParams(dimension_semantics=("arbitrary",)),
        name="bias",
    )(neg[:, :, None, :])


BIAS_PERIOD = 4 * BLOCK


def _bias_kernel(neg_ref, o_ref):
    for kh in range(N_KV_HEADS):
        for tile in range(2):
            for slot in range(2):
                rows = jnp.broadcast_to(neg_ref[4 * kh + 2 * tile + slot], (KEYS, BIAS_PERIOD))
                skewed = pltpu.roll(rows, 0, 1, stride=1, stride_axis=0)
                o_ref[kh, pl.ds(slot * KEYS, KEYS), pl.ds(tile * LANES, LANES)] = skewed[:, :BLOCK]


def kernel(x_prompt, x_sample, norm_w, w_in, lam_re, lam_im, log_step, b_re, b_im, c_re, c_im, d_skip,
           w_glu, b_glu, ssm_norm_w, sink, attn_norm_w, w_out, rel_bias, final_norm_w):
    assert norm_w.shape[0] == 1, "single-layer encoder"
    ops = _ssm_operators(lam_re[0], lam_im[0], log_step[0], b_re[0], b_im[0], c_re[0], c_im[0], d_skip[0])
    bias = _attn_tables(rel_bias)
    w_in_b = w_in[0].astype(BF16)
    w_glu_b = w_glu[0].astype(BF16)
    w_out_b = w_out[0].astype(BF16)
    sink_f = sink[0].astype(F32) * LOG2_E

    outs = []
    for x in (x_prompt, x_sample):
        batch, seq_len, _ = x.shape
        assert batch % SEQ_TILE == 0 and seq_len % TOK_TILE == 0
        n_tok = batch * seq_len
        up, zs, q, k, v, za = _proj_call(x, norm_w, w_in_b)
        yp = _ssm_call(up, ops)
        ao = _attn_call(q.reshape(n_tok, D_ATTN), k.reshape(n_tok, KV_W), v.reshape(n_tok, KV_W),
                        bias, sink_f, batch, seq_len)
        outs.append(_post_call(x, yp, zs, ao.reshape(batch, seq_len, D_ATTN), za, w_glu_b, b_glu, ssm_norm_w,
                               attn_norm_w, w_out_b, final_norm_w.reshape(1, D_MODEL)))
    return tuple(outs)
```

```python
import functools
import math

import numpy as np
import jax
import jax.numpy as jnp
from jax import lax
from jax.experimental import pallas as pl
from jax.experimental.pallas import tpu as pltpu

F32 = jnp.float32
BF16 = jnp.bfloat16

D_MODEL = 1024
D_SSM = 512
D_ATTN = 512
SSM_GROUP = 16
N_GROUPS = D_SSM // SSM_GROUP
SSM_STATE = 64
HEAD_DIM = 64
N_HEADS = D_ATTN // HEAD_DIM
N_KV_HEADS = 2
WINDOW = 128
BLOCK = 128
N_BUCKETS = 32
MAX_DISTANCE = 128
RMS_EPS = 1e-6
NEG_INF = -1e30
LOG2_E = math.log2(math.e)

LANES = 128
SUBLANES = 8
CHUNK = 16
CHUNK_W = CHUNK * SSM_GROUP
STATE_W = 4 * SSM_STATE
SEQ_TILE = SUBLANES
TILE_CHUNKS = LANES // SEQ_TILE
TOK_TILE = TILE_CHUNKS * CHUNK
TILE_ROWS = SEQ_TILE * TOK_TILE
SUB_SEQS = 4
SUB = SUB_SEQS * TOK_TILE
N_SUB = SEQ_TILE // SUB_SEQS
LANE_TILES = D_SSM // LANES
GROUPS_PER_LANE_TILE = LANES // SSM_GROUP
KEYS = 3 * BLOCK
Q_ROWS = 1024
BLOCKS_PER_ITER = 4
KV_W = N_KV_HEADS * HEAD_DIM
MM_ROWS = 512
VMEM_LIMIT = 56 * 1024 * 1024

_C_U, _C_ZS, _C_Q, _C_K, _C_V, _C_ZA, _C_END = 0, 512, 1024, 1536, 1664, 1792, 2304


def _rms(x, w):
    return (x * lax.rsqrt(jnp.mean(x * x, axis=-1, keepdims=True) + RMS_EPS)) * w


def _tile_specs(width):
    return pl.BlockSpec((SUB_SEQS, TOK_TILE, width), lambda sb, cb, s: (sb * N_SUB + s, cb, 0))


_CHUNK_TILE_SPEC = pl.BlockSpec((N_GROUPS, None, None, LANES, CHUNK_W), lambda sb, cb, s: (0, sb, cb, 0, 0))


def _const_spec(shape):
    return pl.BlockSpec(shape, lambda sb, cb, s: (0,) * len(shape))


PAIR_ROWS = 2 * SEQ_TILE * CHUNK


def _pair_perm(t_major_out):
    out_row = lax.broadcasted_iota(jnp.int32, (PAIR_ROWS, PAIR_ROWS), 0)
    in_row = lax.broadcasted_iota(jnp.int32, (PAIR_ROWS, PAIR_ROWS), 1)
    tcs, cst = (out_row, in_row) if t_major_out else (in_row, out_row)
    hit = (tcs // (2 * SEQ_TILE) == cst % CHUNK) & (tcs % (2 * SEQ_TILE) == cst // CHUNK)
    return jnp.where(hit, 1.0, 0.0).astype(BF16)


def _proj_kernel(x_ref, nw_ref, w_ref, up_ref, zs_ref, q_ref, k_ref, v_ref, za_ref, u_scr, b_scr, t_scr):
    s = pl.program_id(2)
    hb = _rms(x_ref[...].reshape(SUB, D_MODEL), nw_ref[...]).astype(BF16)

    def mm(lo, hi):
        return jnp.dot(hb, w_ref[:, lo:hi], preferred_element_type=F32)

    def put(ref, val):
        ref[...] = val.astype(BF16).reshape(ref.shape)

    uz = mm(_C_U, _C_Q)
    u_scr[pl.ds(pl.multiple_of(s * SUB, SUB), SUB), :] = uz[:, :D_SSM].astype(BF16)
    put(zs_ref, uz[:, D_SSM:])
    qkv = mm(_C_Q, _C_ZA)
    put(q_ref, qkv[:, :D_ATTN] * (HEAD_DIM ** -0.5 * LOG2_E))
    put(k_ref, qkv[:, D_ATTN:D_ATTN + KV_W])
    put(v_ref, qkv[:, D_ATTN + KV_W:])
    put(za_ref, mm(_C_ZA, _C_END))

    @pl.when(s == N_SUB - 1)
    def _():
        perm = _pair_perm(t_major_out=True)
        for cp in range(TILE_CHUNKS // 2):
            a = jnp.concatenate(
                [u_scr[pl.ds(sq * TOK_TILE + (2 * cp + c2) * CHUNK, CHUNK), :]
                 for c2 in range(2) for sq in range(SEQ_TILE)], axis=0)
            b_scr[cp] = jnp.dot(perm, a, preferred_element_type=F32).astype(BF16)
        for t in range(CHUNK):
            for j in range(LANE_TILES):
                blk = jnp.concatenate(
                    [b_scr[cp, pl.ds(t * 2 * SEQ_TILE, 2 * SEQ_TILE), pl.ds(j * LANES, LANES)]
                     for cp in range(TILE_CHUNKS // 2)], axis=0)
                t_scr[t, j] = blk.T
        for g in range(N_GROUPS):
            j, g3 = divmod(g, GROUPS_PER_LANE_TILE)
            for th in range(CHUNK_W // LANES):
                rows = jnp.concatenate(
                    [t_scr[th * 8 + t3, j, pl.ds(g3 * SSM_GROUP, SSM_GROUP), :] for t3 in range(8)], axis=0)
                up_ref[g, :, pl.ds(th * LANES, LANES)] = rows.T


def _proj_call(x, norm_w, w_in):
    batch, seq_len, _ = x.shape
    nsb, ncb = batch // SEQ_TILE, seq_len // TOK_TILE
    tok = lambda width: jax.ShapeDtypeStruct((batch, seq_len, width), BF16)
    out_shapes = (
        jax.ShapeDtypeStruct((N_GROUPS, nsb, ncb, LANES, CHUNK_W), BF16),
        tok(D_SSM),
        tok(D_ATTN),
        tok(KV_W),
        tok(KV_W),
        tok(D_ATTN),
    )
    return pl.pallas_call(
        _proj_kernel,
        grid=(nsb, ncb, N_SUB),
        in_specs=[_tile_specs(D_MODEL), _const_spec((1, D_MODEL)), _const_spec((D_MODEL, _C_END))],
        out_specs=(_CHUNK_TILE_SPEC, _tile_specs(D_SSM), _tile_specs(D_ATTN), _tile_specs(KV_W),
                   _tile_specs(KV_W), _tile_specs(D_ATTN)),
        out_shape=out_shapes,
        scratch_shapes=[
            pltpu.VMEM((TILE_ROWS, D_SSM), BF16),
            pltpu.VMEM((TILE_CHUNKS // 2, PAIR_ROWS, D_SSM), BF16),
            pltpu.VMEM((CHUNK, LANE_TILES, LANES, LANES), BF16),
        ],
        compiler_params=pltpu.CompilerParams(
            dimension_semantics=("arbitrary", "arbitrary", "arbitrary"), vmem_limit_bytes=VMEM_LIMIT),
        name="proj",
    )(x, norm_w, w_in)


def _ssm_kernel(u_ref, toep_ref, bs_ref, cs_ref, are_ref, aim_ref, y_ref, s_scr, xf_scr, xb_scr,
                *, nsb, ncb):
    nch = ncb * TILE_CHUNKS
    tiles_per_mm = MM_ROWS // LANES
    seq_rows = nsb * SEQ_TILE

    def mm_blocks():
        for sb in range(nsb):
            for cq in range(ncb // tiles_per_mm):
                yield sb, cq * tiles_per_mm, pl.ds((sb * ncb + cq * tiles_per_mm) * LANES, MM_ROWS)

    def load_u(sb, cb0):
        return jnp.concatenate([u_ref[sb, cb0 + i] for i in range(tiles_per_mm)], axis=0)

    for sb, cb0, rows in mm_blocks():
        st = jnp.dot(load_u(sb, cb0), bs_ref[...], preferred_element_type=F32)
        s_scr[0, rows, :] = st[:, :LANES]
        s_scr[1, rows, :] = st[:, LANES:]

    def chunk_rows(sb, ch):
        return pl.ds(pl.multiple_of(sb * nch * SEQ_TILE + ch * SEQ_TILE, SEQ_TILE), SEQ_TILE)

    def load(ref, part, ch):
        return jnp.concatenate([ref[part, chunk_rows(sb, ch), :] for sb in range(nsb)], axis=0)

    def store(ref, part, ch, val):
        for sb in range(nsb):
            ref[part, chunk_rows(sb, ch), :] = val[sb * SEQ_TILE:(sb + 1) * SEQ_TILE, :]

    are = jnp.broadcast_to(are_ref[...], (seq_rows, LANES))
    aim = jnp.broadcast_to(aim_ref[...], (seq_rows, LANES))
    is_fwd = lax.broadcasted_iota(jnp.int32, (seq_rows, LANES), 1) < SSM_STATE

    def step(i, carry):
        xr, xi = carry
        j = nch - 1 - i
        store(xf_scr, 0, i, xr)
        store(xf_scr, 1, i, xi)
        store(xb_scr, 0, j, xr)
        store(xb_scr, 1, j, xi)
        s_re = jnp.where(is_fwd, load(s_scr, 0, i), load(s_scr, 0, j))
        s_im = jnp.where(is_fwd, load(s_scr, 1, i), load(s_scr, 1, j))
        return (are * xr - aim * xi + s_re, are * xi + aim * xr + s_im)

    zero = jnp.zeros((seq_rows, LANES), F32)
    lax.fori_loop(0, nch, step, (zero, zero), unroll=4)

    fwd_lane = lax.broadcasted_iota(jnp.int32, (MM_ROWS, LANES), 1) < SSM_STATE
    for sb, cb0, rows in mm_blocks():
        xin = jnp.concatenate(
            [jnp.where(fwd_lane, xf_scr[c, rows, :], xb_scr[c, rows, :]) for c in range(2)], axis=1).astype(BF16)
        y = (jnp.dot(load_u(sb, cb0), toep_ref[...], preferred_element_type=F32)
             + jnp.dot(xin, cs_ref[...], preferred_element_type=F32))
        for i in range(tiles_per_mm):
            y_ref[sb, cb0 + i] = y[i * LANES:(i + 1) * LANES, :].astype(BF16)


def _ssm_call(up, ops):
    toep, bs, cs, are, aim = ops
    _, nsb, ncb, _, _ = up.shape
    n_rows = nsb * ncb * LANES
    assert ncb % (MM_ROWS // LANES) == 0
    data_spec = pl.BlockSpec((None, nsb, ncb, LANES, CHUNK_W), lambda g: (g, 0, 0, 0, 0))
    grp = lambda *shape: pl.BlockSpec((None,) + shape, lambda g: (g, 0, 0))
    return pl.pallas_call(
        functools.partial(_ssm_kernel, nsb=nsb, ncb=ncb),
        grid=(N_GROUPS,),
        in_specs=[data_spec, grp(CHUNK_W, CHUNK_W), grp(CHUNK_W, STATE_W), grp(STATE_W, CHUNK_W),
                  grp(1, LANES), grp(1, LANES)],
        out_specs=data_spec,
        out_shape=jax.ShapeDtypeStruct(up.shape, BF16),
        scratch_shapes=[pltpu.VMEM((2, n_rows, LANES), F32)] * 3,
        compiler_params=pltpu.CompilerParams(
            dimension_semantics=("arbitrary",), vmem_limit_bytes=VMEM_LIMIT),
        name="ssm",
    )(up, toep, bs, cs, are, aim)


def _attn_kernel(sink_ref, q_ref, k_ref, v_ref, bias_ref, o_ref, ka_scr, kb_scr, wa_scr, wb_scr, *, seq_len):
    jb = pl.program_id(1)
    nb = seq_len // BLOCK

    @pl.when(jb == 0)
    def _():
        low = lax.broadcasted_iota(jnp.int32, (BLOCK, LANES), 1) < HEAD_DIM
        zero_rows = jnp.zeros((HEAD_DIM, BLOCK), BF16)
        one_rows = jnp.ones((HEAD_DIM, BLOCK), BF16)

        def per_block(r, carry):
            rows = pl.ds(pl.multiple_of(r * BLOCK, BLOCK), BLOCK)
            k = k_ref[rows, :]
            k_swapped = pltpu.roll(k, HEAD_DIM, 1)
            zero = jnp.zeros_like(k)
            ka_scr[0, rows, :] = jnp.where(low, k, zero)
            kb_scr[0, rows, :] = jnp.where(low, zero, k_swapped)
            ka_scr[1, rows, :] = jnp.where(low, k_swapped, zero)
            kb_scr[1, rows, :] = jnp.where(low, zero, k)
            vt = v_ref[rows, :].T
            for kh in range(N_KV_HEADS):
                vt_kh = vt[kh * HEAD_DIM:(kh + 1) * HEAD_DIM]
                wa_scr[kh, r] = jnp.concatenate([vt_kh, zero_rows, one_rows, zero_rows], axis=0)
                wb_scr[kh, r] = jnp.concatenate([zero_rows, vt_kh, zero_rows, one_rows], axis=0)
            return carry

        lax.fori_loop(0, nb, per_block, 0)

    col_tile0 = lax.broadcasted_iota(jnp.int32, (1, 2 * LANES), 1) < LANES
    row_slot0 = lax.broadcasted_iota(jnp.int32, (BLOCK, 2 * LANES), 0) < HEAD_DIM
    def query_block(sb):
        n = jb * (Q_ROWS // BLOCK) + sb
        kb0 = jnp.clip(n - 1, 0, nb - KEYS // BLOCK)
        ks = pl.multiple_of(kb0 * BLOCK, BLOCK)
        var = jnp.where(n == 0, 0, jnp.where(n == nb - 1, 2, 1))
        qrows = pl.ds(pl.multiple_of(sb * BLOCK, BLOCK), BLOCK)
        for kh in range(N_KV_HEADS):
            kcat = jnp.concatenate([ka_scr[kh, pl.ds(ks, KEYS), :], kb_scr[kh, pl.ds(ks, KEYS), :]], axis=0)
            q2 = jnp.concatenate([q_ref[qrows, pl.ds((2 * kh + jj) * LANES, LANES)] for jj in range(2)], axis=0)
            t = lax.dot_general(kcat, q2, (((1,), (1,)), ((), ())), preferred_element_type=F32)
            t = t + bias_ref[var, kh]
            ps, sink_terms = [], []
            for hh in range(2):
                th = t[hh * KEYS:(hh + 1) * KEYS]
                sink = jnp.where(col_tile0, sink_ref[4 * kh + hh], sink_ref[4 * kh + 2 + hh])
                m = jnp.maximum(jnp.max(th, axis=0, keepdims=True), sink)
                ps.append(jnp.exp2(th - m).astype(BF16))
                sink_terms.append(jnp.exp2(sink - m))
            w = jnp.concatenate([wa_scr[kh, kb0 + i] for i in range(KEYS // BLOCK)]
                                + [wb_scr[kh, kb0 + i] for i in range(KEYS // BLOCK)], axis=1)
            o2 = jnp.dot(w, jnp.concatenate(ps, axis=0), preferred_element_type=F32)
            denom = o2[LANES:] + jnp.where(row_slot0, sink_terms[0], sink_terms[1])
            on = o2[:LANES] / denom
            for jj in range(2):
                o_ref[qrows, pl.ds((2 * kh + jj) * LANES, LANES)] = (
                    on[:, jj * LANES:(jj + 1) * LANES].T.astype(BF16))

    def query_blocks(i, carry):
        for sb in range(BLOCKS_PER_ITER):
            query_block(i * BLOCKS_PER_ITER + sb)
        return carry

    lax.fori_loop(0, Q_ROWS // (BLOCKS_PER_ITER * BLOCK), query_blocks, 0)


def _attn_call(q, k, v, bias, sink, batch, seq_len):
    assert seq_len % Q_ROWS == 0 and seq_len >= KEYS
    nq = seq_len // Q_ROWS
    nb = seq_len // BLOCK
    return pl.pallas_call(
        functools.partial(_attn_kernel, seq_len=seq_len),
        grid_spec=pltpu.PrefetchScalarGridSpec(
            num_scalar_prefetch=1,
            grid=(batch, nq),
            in_specs=[
                pl.BlockSpec((Q_ROWS, D_ATTN), lambda b, j, s: (b * nq + j, 0)),
                pl.BlockSpec((seq_len, KV_W), lambda b, j, s: (b, 0)),
                pl.BlockSpec((seq_len, KV_W), lambda b, j, s: (b, 0)),
                pl.BlockSpec((3, N_KV_HEADS, 2 * KEYS, 2 * LANES), lambda b, j, s: (0, 0, 0, 0)),
            ],
            out_specs=pl.BlockSpec((Q_ROWS, D_ATTN), lambda b, j, s: (b * nq + j, 0)),
            scratch_shapes=[
                pltpu.VMEM((N_KV_HEADS, seq_len, LANES), BF16),
                pltpu.VMEM((N_KV_HEADS, seq_len, LANES), BF16),
                pltpu.VMEM((N_KV_HEADS, nb, 2 * LANES, BLOCK), BF16),
                pltpu.VMEM((N_KV_HEADS, nb, 2 * LANES, BLOCK), BF16),
            ],
        ),
        out_shape=jax.ShapeDtypeStruct((batch * seq_len, D_ATTN), BF16),
        compiler_params=pltpu.CompilerParams(
            dimension_semantics=("arbitrary", "arbitrary"), vmem_limit_bytes=VMEM_LIMIT),
        name="attn",
    )(sink, q, k, v, bias)


def _gelu_tanh(x):
    c = math.sqrt(2.0 / math.pi)
    half = 0.5 * x
    return half + half * jnp.tanh(x * (c + (c * 0.044715) * (x * x)))


def _sigmoid(x):
    return 0.5 * jnp.tanh(0.5 * x) + 0.5


def _silu(x):
    return x * _sigmoid(x)


def _post_kernel(x_ref, yp_ref, zs_ref, ao_ref, za_ref, wglu_ref, bglu_ref, snw_ref, anw_ref,
                 wout_ref, fnw_ref, o_ref, y_scr, b_scr, t_scr):
    s = pl.program_id(2)

    @pl.when(s == 0)
    def _():
        for g in range(N_GROUPS):
            j, g3 = divmod(g, GROUPS_PER_LANE_TILE)
            for th in range(CHUNK_W // LANES):
                tt = yp_ref[g, :, pl.ds(th * LANES, LANES)].T
                for t3 in range(8):
                    t_scr[th * 8 + t3, j, pl.ds(g3 * SSM_GROUP, SSM_GROUP), :] = (
                        tt[t3 * SSM_GROUP:(t3 + 1) * SSM_GROUP, :])
        for t in range(CHUNK):
            for j in range(LANE_TILES):
                b_scr[t, :, pl.ds(j * LANES, LANES)] = t_scr[t, j].T
        perm = _pair_perm(t_major_out=False)
        for cp in range(TILE_CHUNKS // 2):
            a = jnp.concatenate(
                [b_scr[t, pl.ds(cp * 2 * SEQ_TILE, 2 * SEQ_TILE), :] for t in range(CHUNK)], axis=0)
            ys = jnp.dot(perm, a, preferred_element_type=F32)
            for c2 in range(2):
                for sq in range(SEQ_TILE):
                    r0 = (c2 * SEQ_TILE + sq) * CHUNK
                    y_scr[pl.ds(sq * TOK_TILE + (2 * cp + c2) * CHUNK, CHUNK), :] = ys[r0:r0 + CHUNK, :]

    def get(ref):
        return ref[...].reshape(SUB, ref.shape[-1]).astype(F32)

    y = y_scr[pl.ds(pl.multiple_of(s * SUB, SUB), SUB), :]
    g = _gelu_tanh(y)
    gate = _sigmoid(jnp.dot(g.astype(BF16), wglu_ref[...], preferred_element_type=F32) + bglu_ref[...])
    def gate_of(ref):
        return _silu(ref[...].reshape(SUB, ref.shape[-1])).astype(F32)

    n_ssm = _rms(g * gate, snw_ref[...]) * gate_of(zs_ref)
    n_attn = _rms(get(ao_ref), anw_ref[...]) * gate_of(za_ref)
    mixed = jnp.concatenate([n_ssm, n_attn], axis=1).astype(BF16)
    res = get(x_ref) + jnp.dot(mixed, wout_ref[...], preferred_element_type=F32)
    o_ref[...] = _rms(res, fnw_ref[...]).reshape(o_ref.shape)


def _post_call(x, yp, zs, ao, za, w_glu, b_glu, ssm_norm_w, attn_norm_w, w_out, final_norm_w):
    batch, seq_len, _ = x.shape
    nsb, ncb = batch // SEQ_TILE, seq_len // TOK_TILE
    return pl.pallas_call(
        _post_kernel,
        grid=(nsb, ncb, N_SUB),
        in_specs=[
            _tile_specs(D_MODEL), _CHUNK_TILE_SPEC, _tile_specs(D_SSM), _tile_specs(D_ATTN), _tile_specs(D_ATTN),
            _const_spec((D_SSM, D_SSM)), _const_spec((1, D_SSM)), _const_spec((1, D_SSM)),
            _const_spec((1, D_ATTN)), _const_spec((D_MODEL, D_MODEL)), _const_spec((1, D_MODEL)),
        ],
        out_specs=_tile_specs(D_MODEL),
        out_shape=jax.ShapeDtypeStruct((batch, seq_len, D_MODEL), F32),
        scratch_shapes=[
            pltpu.VMEM((TILE_ROWS, D_SSM), F32),
            pltpu.VMEM((CHUNK, LANES, D_SSM), BF16),
            pltpu.VMEM((CHUNK, LANE_TILES, LANES, LANES), BF16),
        ],
        compiler_params=pltpu.CompilerParams(
            dimension_semantics=("arbitrary", "arbitrary", "arbitrary"), vmem_limit_bytes=VMEM_LIMIT),
        name="post",
    )(x, yp, zs, ao, za, w_glu, b_glu, ssm_norm_w, attn_norm_w, w_out, final_norm_w)


def _cmul(ar, ai, br, bi):
    return ar * br - ai * bi, ar * bi + ai * br


def _ops_kernel(row_ref, col_ref, bt_ref, ct_ref, dsk_ref, toep_ref, bs_ref, cs_ref, a_ref):
    def discretise(lr, li, ls):
        dt = jnp.exp(ls)
        mag = jnp.exp(lr * dt)
        return mag * jnp.cos(li * dt), mag * jnp.sin(li * dt)

    lr, li = row_ref[0], row_ref[1]
    lb_re, lb_im = discretise(lr, li, row_ref[2])
    den = lr * lr + li * li
    nr = lb_re - 1.0
    coef_re = (nr * lr + lb_im * li) / den
    coef_im = (lb_im * lr - nr * li) / den
    bb_re, bb_im = _cmul(coef_re, coef_im, bt_ref[0], bt_ref[1])
    bb_re = jnp.concatenate([bb_re] * CHUNK, axis=0)
    bb_im = jnp.concatenate([bb_im] * CHUNK, axis=0)

    row_t = lax.broadcasted_iota(jnp.int32, (CHUNK_W, LANES), 0) // SSM_GROUP
    fwd_lane = lax.broadcasted_iota(jnp.int32, (CHUNK_W, LANES), 1) < SSM_STATE
    col_t = lax.broadcasted_iota(jnp.int32, (LANES, CHUNK_W), 1) // SSM_GROUP
    fwd_row = lax.broadcasted_iota(jnp.int32, (LANES, CHUNK_W), 0) < SSM_STATE
    ct_re, ct_im = ct_ref[0], ct_ref[1]
    w_re = jnp.zeros((CHUNK_W, LANES), F32)
    w_im = jnp.zeros((CHUNK_W, LANES), F32)
    toep = jnp.zeros((CHUNK_W, CHUNK_W), F32)
    for i in range(CHUNK):
        t_now = jnp.where(fwd_lane, i, CHUNK - 1 - i)
        w_re, w_im = _cmul(w_re, w_im, lb_re, lb_im)
        w_re = w_re + jnp.where(row_t == t_now, bb_re, 0.0)
        w_im = w_im + jnp.where(row_t == t_now, bb_im, 0.0)
        out_now = col_t == jnp.where(fwd_row, i, CHUNK - 1 - i)
        rhs = jnp.concatenate([jnp.where(out_now, ct_re, 0.0), jnp.where(out_now, -ct_im, 0.0)], axis=0)
        lhs = jnp.concatenate([w_re, w_im], axis=1)
        toep = toep + jnp.dot(lhs.astype(BF16), rhs.astype(BF16), preferred_element_type=F32)
    r = lax.broadcasted_iota(jnp.int32, (CHUNK_W, CHUNK_W), 0)
    c = lax.broadcasted_iota(jnp.int32, (CHUNK_W, CHUNK_W), 1)
    toep_ref[...] = (toep + jnp.where(r == c, dsk_ref[...], 0.0)).astype(BF16)
    bs_ref[...] = jnp.concatenate([w_re, w_im], axis=1).astype(BF16)

    pr, pi = discretise(col_ref[0], col_ref[1], col_ref[2])
    k = jnp.where(fwd_row, col_t + 1, CHUNK - col_t)
    acc_re = jnp.ones((LANES, CHUNK_W), F32)
    acc_im = jnp.zeros((LANES, CHUNK_W), F32)
    for bit in range(CHUNK.bit_length()):
        nre, nim = _cmul(acc_re, acc_im, pr, pi)
        take = (k & (1 << bit)) != 0
        acc_re = jnp.where(take, nre, acc_re)
        acc_im = jnp.where(take, nim, acc_im)
        pr, pi = _cmul(pr, pi, pr, pi)
    cp_re, cp_im = _cmul(ct_re, ct_im, acc_re, acc_im)
    cs_ref[...] = jnp.concatenate([cp_re, -cp_im], axis=0).astype(BF16)

    a_re, a_im = lb_re, lb_im
    for _ in range(CHUNK.bit_length() - 1):
        a_re, a_im = _cmul(a_re, a_im, a_re, a_im)
    a_ref[0] = a_re
    a_ref[1] = a_im


def _ssm_operators(lam_re, lam_im, log_step, b_re, b_im, c_re, c_im, d_skip):
    g_first = lambda a: jnp.transpose(a.astype(F32), (1, 0, 2)).reshape(N_GROUPS, 2 * SSM_STATE)
    params = jnp.stack([g_first(lam_re), g_first(lam_im),
                        jnp.repeat(log_step.astype(F32).T, SSM_STATE, axis=1)], axis=1)
    b_t = lambda b: jnp.transpose(b.astype(F32), (1, 3, 0, 2)).reshape(N_GROUPS, SSM_GROUP, 2 * SSM_STATE)
    c_t = lambda c: jnp.tile(
        jnp.transpose(c.astype(F32), (1, 0, 3, 2)).reshape(N_GROUPS, 2 * SSM_STATE, SSM_GROUP), (1, 1, CHUNK))
    bt = jnp.stack([b_t(b_re), b_t(b_im)], axis=1)
    ct = jnp.stack([c_t(c_re), c_t(c_im)], axis=1)
    dsk = jnp.tile(d_skip.astype(F32), (1, CHUNK))[:, None, :]
    grp = lambda *shape: pl.BlockSpec((None,) + shape, lambda g: (g,) + (0,) * len(shape))
    mat = jax.ShapeDtypeStruct((N_GROUPS, CHUNK_W, CHUNK_W), BF16)
    toep, bs, cs, a = pl.pallas_call(
        _ops_kernel,
        grid=(N_GROUPS,),
        in_specs=[grp(3, 1, LANES), grp(3, LANES, 1), grp(2, SSM_GROUP, LANES), grp(2, LANES, CHUNK_W),
                  grp(1, CHUNK_W)],
        out_specs=(grp(CHUNK_W, CHUNK_W), grp(CHUNK_W, STATE_W), grp(STATE_W, CHUNK_W), grp(2, 1, LANES)),
        out_shape=(mat, mat, mat, jax.ShapeDtypeStruct((N_GROUPS, 2, 1, LANES), F32)),
        compiler_params=pltpu.CompilerParams(dimension_semantics=("arbitrary",)),
        name="s5_ops",
    )(params[:, :, None, :], params[:, :, :, None], bt, ct, dsk)
    return toep, bs, cs, a[:, 0], a[:, 1]


def _t5_buckets_np(rel):
    half = N_BUCKETS // 2
    max_exact = half // 2
    ret = np.where(rel > 0, half, 0)
    n = np.abs(rel)
    nf = np.maximum(n, 1).astype(np.float64)
    large = max_exact + (np.log(nf / max_exact) / math.log(MAX_DISTANCE / max_exact)
                         * (half - max_exact)).astype(np.int32)
    large = np.minimum(large, half - 1)
    return ret + np.where(n < max_exact, n, large)


def _attn_tables(rel_bias):
    offsets = (0, -BLOCK, -2 * BLOCK)
    dist = np.arange(BIAS_PERIOD)
    dist = np.where(dist < KEYS, dist, dist - BIAS_PERIOD)
    rel = np.stack([dist + off for off in offsets])
    per_dist = rel_bias.astype(F32)[_t5_buckets_np(rel)] * LOG2_E
    per_dist = jnp.where((np.abs(rel) <= WINDOW)[:, :, None], per_dist, NEG_INF)
    per_dist = jnp.transpose(per_dist, (0, 2, 1))
    neg = jnp.concatenate([per_dist[:, :, :1], per_dist[:, :, :0:-1]], axis=-1)
    return pl.pallas_call(
        _bias_kernel,
        grid=(len(offsets),),
        in_specs=[pl.BlockSpec((None, N_HEADS, 1, BIAS_PERIOD), lambda v: (v, 0, 0, 0))],
        out_specs=pl.BlockSpec((None, N_KV_HEADS, 2 * KEYS, 2 * LANES), lambda v: (v, 0, 0, 0)),
        out_shape=jax.ShapeDtypeStruct((len(offsets), N_KV_HEADS, 2 * KEYS, 2 * LANES), F32),
        compiler_params=pltpu.CompilerParams(dimension_semantics=("arbitrary",)),
        name="bias",
    )(neg[:, :, None, :])


BIAS_PERIOD = 4 * BLOCK


def _bias_kernel(neg_ref, o_ref):
    for kh in range(N_KV_HEADS):
        for tile in range(2):
            for slot in range(2):
                rows = jnp.broadcast_to(neg_ref[4 * kh + 2 * tile + slot], (KEYS, BIAS_PERIOD))
                skewed = pltpu.roll(rows, 0, 1, stride=1, stride_axis=0)
                o_ref[kh, pl.ds(slot * KEYS, KEYS), pl.ds(tile * LANES, LANES)] = skewed[:, :BLOCK]


def kernel(x_prompt, x_sample, norm_w, w_in, lam_re, lam_im, log_step, b_re, b_im, c_re, c_im, d_skip,
           w_glu, b_glu, ssm_norm_w, sink, attn_norm_w, w_out, rel_bias, final_norm_w):
    assert norm_w.shape[0] == 1, "single-layer encoder"
    ops = _ssm_operators(lam_re[0], lam_im[0], log_step[0], b_re[0], b_im[0], c_re[0], c_im[0], d_skip[0])
    bias = _attn_tables(rel_bias)
    w_in_b = w_in[0].astype(BF16)
    w_glu_b = w_glu[0].astype(BF16)
    w_out_b = w_out[0].astype(BF16)
    sink_f = sink[0].astype(F32) * LOG2_E

    outs = []
    for x in (x_prompt, x_sample):
        batch, seq_len, _ = x.shape
        assert batch % SEQ_TILE == 0 and seq_len % TOK_TILE == 0
        n_tok = batch * seq_len
        up, zs, q, k, v, za = _proj_call(x, norm_w, w_in_b)
        yp = _ssm_call(up, ops)
        ao = _attn_call(q.reshape(n_tok, D_ATTN), k.reshape(n_tok, KV_W), v.reshape(n_tok, KV_W),
                        bias, sink_f, batch, seq_len)
        outs.append(_post_call(x, yp, zs, ao.reshape(batch, seq_len, D_ATTN), za, w_glu_b, b_glu, ssm_norm_w,
                               attn_norm_w, w_out_b, final_norm_w.reshape(1, D_MODEL)))
    return tuple(outs)
```

```python
import functools
import math

import numpy as np
import jax
import jax.numpy as jnp
from jax import lax
from jax.experimental import pallas as pl
from jax.experimental.pallas import tpu as pltpu

F32 = jnp.float32
BF16 = jnp.bfloat16

D_MODEL = 1024
D_SSM = 512
D_ATTN = 512
SSM_GROUP = 16
N_GROUPS = D_SSM // SSM_GROUP
SSM_STATE = 64
HEAD_DIM = 64
N_HEADS = D_ATTN // HEAD_DIM
N_KV_HEADS = 2
WINDOW = 128
BLOCK = 128
N_BUCKETS = 32
MAX_DISTANCE = 128
RMS_EPS = 1e-6
NEG_INF = -1e30
LOG2_E = math.log2(math.e)

LANES = 128
SUBLANES = 8
CHUNK = 16
CHUNK_W = CHUNK * SSM_GROUP
STATE_W = 4 * SSM_STATE
SEQ_TILE = SUBLANES
TILE_CHUNKS = LANES // SEQ_TILE
TOK_TILE = TILE_CHUNKS * CHUNK
TILE_ROWS = SEQ_TILE * TOK_TILE
SUB_SEQS = 4
SUB = SUB_SEQS * TOK_TILE
N_SUB = SEQ_TILE // SUB_SEQS
LANE_TILES = D_SSM // LANES
GROUPS_PER_LANE_TILE = LANES // SSM_GROUP
KEYS = 3 * BLOCK
Q_ROWS = 1024
BLOCKS_PER_ITER = 4
KV_W = N_KV_HEADS * HEAD_DIM
MM_ROWS = 512
VMEM_LIMIT = 56 * 1024 * 1024

_C_U, _C_ZS, _C_Q, _C_K, _C_V, _C_ZA, _C_END = 0, 512, 1024, 1536, 1664, 1792, 2304


def _rms(x, w):
    return (x * lax.rsqrt(jnp.mean(x * x, axis=-1, keepdims=True) + RMS_EPS)) * w


def _tile_specs(width):
    return pl.BlockSpec((SUB_SEQS, TOK_TILE, width), lambda sb, cb, s: (sb * N_SUB + s, cb, 0))


_CHUNK_TILE_SPEC = pl.BlockSpec((N_GROUPS, None, None, LANES, CHUNK_W), lambda sb, cb, s: (0, sb, cb, 0, 0))


def _const_spec(shape):
    return pl.BlockSpec(shape, lambda sb, cb, s: (0,) * len(shape))


PAIR_ROWS = 2 * SEQ_TILE * CHUNK


def _pair_perm(t_major_out):
    out_row = lax.broadcasted_iota(jnp.int32, (PAIR_ROWS, PAIR_ROWS), 0)
    in_row = lax.broadcasted_iota(jnp.int32, (PAIR_ROWS, PAIR_ROWS), 1)
    tcs, cst = (out_row, in_row) if t_major_out else (in_row, out_row)
    hit = (tcs // (2 * SEQ_TILE) == cst % CHUNK) & (tcs % (2 * SEQ_TILE) == cst // CHUNK)
    return jnp.where(hit, 1.0, 0.0).astype(BF16)


def _proj_kernel(x_ref, nw_ref, w_ref, up_ref, zs_ref, q_ref, k_ref, v_ref, za_ref, u_scr, b_scr, t_scr):
    s = pl.program_id(2)
    hb = _rms(x_ref[...].reshape(SUB, D_MODEL), nw_ref[...]).astype(BF16)

    def mm(lo, hi):
        return jnp.dot(hb, w_ref[:, lo:hi], preferred_element_type=F32)

    def put(ref, val):
        ref[...] = val.astype(BF16).reshape(ref.shape)

    uz = mm(_C_U, _C_Q)
    u_scr[pl.ds(pl.multiple_of(s * SUB, SUB), SUB), :] = uz[:, :D_SSM].astype(BF16)
    put(zs_ref, uz[:, D_SSM:])
    qkv = mm(_C_Q, _C_ZA)
    put(q_ref, qkv[:, :D_ATTN] * (HEAD_DIM ** -0.5 * LOG2_E))
    put(k_ref, qkv[:, D_ATTN:D_ATTN + KV_W])
    put(v_ref, qkv[:, D_ATTN + KV_W:])
    put(za_ref, mm(_C_ZA, _C_END))

    @pl.when(s == N_SUB - 1)
    def _():
        perm = _pair_perm(t_major_out=True)
        for cp in range(TILE_CHUNKS // 2):
            a = jnp.concatenate(
                [u_scr[pl.ds(sq * TOK_TILE + (2 * cp + c2) * CHUNK, CHUNK), :]
                 for c2 in range(2) for sq in range(SEQ_TILE)], axis=0)
            b_scr[cp] = jnp.dot(perm, a, preferred_element_type=F32).astype(BF16)
        for t in range(CHUNK):
            for j in range(LANE_TILES):
                blk = jnp.concatenate(
                    [b_scr[cp, pl.ds(t * 2 * SEQ_TILE, 2 * SEQ_TILE), pl.ds(j * LANES, LANES)]
                     for cp in range(TILE_CHUNKS // 2)], axis=0)
                t_scr[t, j] = blk.T
        for g in range(N_GROUPS):
            j, g3 = divmod(g, GROUPS_PER_LANE_TILE)
            for th in range(CHUNK_W // LANES):
                rows = jnp.concatenate(
                    [t_scr[th * 8 + t3, j, pl.ds(g3 * SSM_GROUP, SSM_GROUP), :] for t3 in range(8)], axis=0)
                up_ref[g, :, pl.ds(th * LANES, LANES)] = rows.T


def _proj_call(x, norm_w, w_in):
    batch, seq_len, _ = x.shape
    nsb, ncb = batch // SEQ_TILE, seq_len // TOK_TILE
    tok = lambda width: jax.ShapeDtypeStruct((batch, seq_len, width), BF16)
    out_shapes = (
        jax.ShapeDtypeStruct((N_GROUPS, nsb, ncb, LANES, CHUNK_W), BF16),
        tok(D_SSM),
        tok(D_ATTN),
        tok(KV_W),
        tok(KV_W),
        tok(D_ATTN),
    )
    return pl.pallas_call(
        _proj_kernel,
        grid=(nsb, ncb, N_SUB),
        in_specs=[_tile_specs(D_MODEL), _const_spec((1, D_MODEL)), _const_spec((D_MODEL, _C_END))],
        out_specs=(_CHUNK_TILE_SPEC, _tile_specs(D_SSM), _tile_specs(D_ATTN), _tile_specs(KV_W),
                   _tile_specs(KV_W), _tile_specs(D_ATTN)),
        out_shape=out_shapes,
        scratch_shapes=[
            pltpu.VMEM((TILE_ROWS, D_SSM), BF16),
            pltpu.VMEM((TILE_CHUNKS // 2, PAIR_ROWS, D_SSM), BF16),
            pltpu.VMEM((CHUNK, LANE_TILES, LANES, LANES), BF16),
        ],
        compiler_params=pltpu.CompilerParams(
            dimension_semantics=("arbitrary", "arbitrary", "arbitrary"), vmem_limit_bytes=VMEM_LIMIT),
        name="proj",
    )(x, norm_w, w_in)


def _ssm_kernel(u_ref, toep_ref, bs_ref, cs_ref, are_ref, aim_ref, y_ref, s_scr, xf_scr, xb_scr,
                *, nsb, ncb):
    nch = ncb * TILE_CHUNKS
    tiles_per_mm = MM_ROWS // LANES
    seq_rows = nsb * SEQ_TILE

    def mm_blocks():
        for sb in range(nsb):
            for cq in range(ncb // tiles_per_mm):
                yield sb, cq * tiles_per_mm, pl.ds((sb * ncb + cq * tiles_per_mm) * LANES, MM_ROWS)

    def load_u(sb, cb0):
        return jnp.concatenate([u_ref[sb, cb0 + i] for i in range(tiles_per_mm)], axis=0)

    for sb, cb0, rows in mm_blocks():
        st = jnp.dot(load_u(sb, cb0), bs_ref[...], preferred_element_type=F32)
        s_scr[0, rows, :] = st[:, :LANES]
        s_scr[1, rows, :] = st[:, LANES:]

    def chunk_rows(sb, ch):
        return pl.ds(pl.multiple_of(sb * nch * SEQ_TILE + ch * SEQ_TILE, SEQ_TILE), SEQ_TILE)

    def load(ref, part, ch):
        return jnp.concatenate([ref[part, chunk_rows(sb, ch), :] for sb in range(nsb)], axis=0)

    def store(ref, part, ch, val):
        for sb in range(nsb):
            ref[part, chunk_rows(sb, ch), :] = val[sb * SEQ_TILE:(sb + 1) * SEQ_TILE, :]

    are = jnp.broadcast_to(are_ref[...], (seq_rows, LANES))
    aim = jnp.broadcast_to(aim_ref[...], (seq_rows, LANES))
    is_fwd = lax.broadcasted_iota(jnp.int32, (seq_rows, LANES), 1) < SSM_STATE

    def step(i, carry):
        xr, xi = carry
        j = nch - 1 - i
        store(xf_scr, 0, i, xr)
        store(xf_scr, 1, i, xi)
        store(xb_scr, 0, j, xr)
        store(xb_scr, 1, j, xi)
        s_re = jnp.where(is_fwd, load(s_scr, 0, i), load(s_scr, 0, j))
        s_im = jnp.where(is_fwd, load(s_scr, 1, i), load(s_scr, 1, j))
        return (are * xr - aim * xi + s_re, are * xi + aim * xr + s_im)

    zero = jnp.zeros((seq_rows, LANES), F32)
    lax.fori_loop(0, nch, step, (zero, zero), unroll=4)

    fwd_lane = lax.broadcasted_iota(jnp.int32, (MM_ROWS, LANES), 1) < SSM_STATE
    for sb, cb0, rows in mm_blocks():
        xin = jnp.concatenate(
            [jnp.where(fwd_lane, xf_scr[c, rows, :], xb_scr[c, rows, :]) for c in range(2)], axis=1).astype(BF16)
        y = (jnp.dot(load_u(sb, cb0), toep_ref[...], preferred_element_type=F32)
             + jnp.dot(xin, cs_ref[...], preferred_element_type=F32))
        for i in range(tiles_per_mm):
            y_ref[sb, cb0 + i] = y[i * LANES:(i + 1) * LANES, :].astype(BF16)


def _ssm_call(up, ops):
    toep, bs, cs, are, aim = ops
    _, nsb, ncb, _, _ = up.shape
    n_rows = nsb * ncb * LANES
    assert ncb % (MM_ROWS // LANES) == 0
    data_spec = pl.BlockSpec((None, nsb, ncb, LANES, CHUNK_W), lambda g: (g, 0, 0, 0, 0))
    grp = lambda *shape: pl.BlockSpec((None,) + shape, lambda g: (g, 0, 0))
    return pl.pallas_call(
        functools.partial(_ssm_kernel, nsb=nsb, ncb=ncb),
        grid=(N_GROUPS,),
        in_specs=[data_spec, grp(CHUNK_W, CHUNK_W), grp(CHUNK_W, STATE_W), grp(STATE_W, CHUNK_W),
                  grp(1, LANES), grp(1, LANES)],
        out_specs=data_spec,
        out_shape=jax.ShapeDtypeStruct(up.shape, BF16),
        scratch_shapes=[pltpu.VMEM((2, n_rows, LANES), F32)] * 3,
        compiler_params=pltpu.CompilerParams(
            dimension_semantics=("arbitrary",), vmem_limit_bytes=VMEM_LIMIT),
        name="ssm",
    )(up, toep, bs, cs, are, aim)


def _attn_kernel(sink_ref, q_ref, k_ref, v_ref, bias_ref, o_ref, ka_scr, kb_scr, wa_scr, wb_scr, *, seq_len):
    jb = pl.program_id(1)
    nb = seq_len // BLOCK

    @pl.when((pl.program_id(0) == 0) & (jb == 0))
    def _():
        zero_rows = jnp.zeros((HEAD_DIM, BLOCK), BF16)
        one_rows = jnp.ones((HEAD_DIM, BLOCK), BF16)

        def per_block(r, carry):
            for kh in range(N_KV_HEADS):
                wa_scr[kh, r, pl.ds(HEAD_DIM, 3 * HEAD_DIM), :] = jnp.concatenate(
                    [zero_rows, one_rows, zero_rows], axis=0)
                wb_scr[kh, r, pl.ds(0, HEAD_DIM), :] = zero_rows
                wb_scr[kh, r, pl.ds(2 * HEAD_DIM, 2 * HEAD_DIM), :] = jnp.concatenate([zero_rows, one_rows], axis=0)
            return carry

        lax.fori_loop(0, nb, per_block, 0)

    @pl.when(jb == 0)
    def _():
        low = lax.broadcasted_iota(jnp.int32, (BLOCK, LANES), 1) < HEAD_DIM

        def per_block(r, carry):
            rows = pl.ds(pl.multiple_of(r * BLOCK, BLOCK), BLOCK)
            k = k_ref[rows, :]
            k_swapped = pltpu.roll(k, HEAD_DIM, 1)
            zero = jnp.zeros_like(k)
            ka_scr[0, rows, :] = jnp.where(low, k, zero)
            kb_scr[0, rows, :] = jnp.where(low, zero, k_swapped)
            ka_scr[1, rows, :] = jnp.where(low, k_swapped, zero)
            kb_scr[1, rows, :] = jnp.where(low, zero, k)
            vt = v_ref[rows, :].T
            for kh in range(N_KV_HEADS):
                vt_kh = vt[kh * HEAD_DIM:(kh + 1) * HEAD_DIM]
                wa_scr[kh, r, pl.ds(0, HEAD_DIM), :] = vt_kh
                wb_scr[kh, r, pl.ds(HEAD_DIM, HEAD_DIM), :] = vt_kh
            return carry

        lax.fori_loop(0, nb, per_block, 0)

    col_tile0 = lax.broadcasted_iota(jnp.int32, (1, 2 * LANES), 1) < LANES
    def query_block(sb):
        n = jb * (Q_ROWS // BLOCK) + sb
        kb0 = jnp.clip(n - 1, 0, nb - KEYS // BLOCK)
        ks = pl.multiple_of(kb0 * BLOCK, BLOCK)
        var = jnp.where(n == 0, 0, jnp.where(n == nb - 1, 2, 1))
        qrows = pl.ds(pl.multiple_of(sb * BLOCK, BLOCK), BLOCK)
        for kh in range(N_KV_HEADS):
            kcat = jnp.concatenate([ka_scr[kh, pl.ds(ks, KEYS), :], kb_scr[kh, pl.ds(ks, KEYS), :]], axis=0)
            q2 = jnp.concatenate([q_ref[qrows, pl.ds((2 * kh + jj) * LANES, LANES)] for jj in range(2)], axis=0)
            t = lax.dot_general(kcat, q2, (((1,), (1,)), ((), ())), preferred_element_type=F32)
            t = t + bias_ref[var, kh]
            ps, sink_terms = [], []
            for hh in range(2):
                th = t[hh * KEYS:(hh + 1) * KEYS]
                sink = jnp.where(col_tile0, sink_ref[4 * kh + hh], sink_ref[4 * kh + 2 + hh])
                m = jnp.maximum(jnp.max(th, axis=0, keepdims=True), sink)
                ps.append(jnp.exp2(th - m).astype(BF16))
                sink_terms.append(jnp.exp2(sink - m))
            w = jnp.concatenate([wa_scr[kh, kb0 + i] for i in range(KEYS // BLOCK)]
                                + [wb_scr[kh, kb0 + i] for i in range(KEYS // BLOCK)], axis=1)
            o2 = jnp.dot(w, jnp.concatenate(ps, axis=0), preferred_element_type=F32)
            invs = [1.0 / (o2[LANES + hh * HEAD_DIM:LANES + hh * HEAD_DIM + SUBLANES] + sink_terms[hh])
                    for hh in range(2)]
            inv = jnp.concatenate([invs[hh] for hh in range(2) for _ in range(HEAD_DIM // SUBLANES)], axis=0)
            on = o2[:LANES] * inv
            for jj in range(2):
                o_ref[qrows, pl.ds((2 * kh + jj) * LANES, LANES)] = (
                    on[:, jj * LANES:(jj + 1) * LANES].T.astype(BF16))

    def query_blocks(i, carry):
        for sb in range(BLOCKS_PER_ITER):
            query_block(i * BLOCKS_PER_ITER + sb)
        return carry

    lax.fori_loop(0, Q_ROWS // (BLOCKS_PER_ITER * BLOCK), query_blocks, 0)


def _attn_call(q, k, v, bias, sink, batch, seq_len):
    assert seq_len % Q_ROWS == 0 and seq_len >= KEYS
    nq = seq_len // Q_ROWS
    nb = seq_len // BLOCK
    return pl.pallas_call(
        functools.partial(_attn_kernel, seq_len=seq_len),
        grid_spec=pltpu.PrefetchScalarGridSpec(
            num_scalar_prefetch=1,
            grid=(batch, nq),
            in_specs=[
                pl.BlockSpec((Q_ROWS, D_ATTN), lambda b, j, s: (b * nq + j, 0)),
                pl.BlockSpec((seq_len, KV_W), lambda b, j, s: (b, 0)),
                pl.BlockSpec((seq_len, KV_W), lambda b, j, s: (b, 0)),
                pl.BlockSpec((3, N_KV_HEADS, 2 * KEYS, 2 * LANES), lambda b, j, s: (0, 0, 0, 0)),
            ],
            out_specs=pl.BlockSpec((Q_ROWS, D_ATTN), lambda b, j, s: (b * nq + j, 0)),
            scratch_shapes=[
                pltpu.VMEM((N_KV_HEADS, seq_len, LANES), BF16),
                pltpu.VMEM((N_KV_HEADS, seq_len, LANES), BF16),
                pltpu.VMEM((N_KV_HEADS, nb, 2 * LANES, BLOCK), BF16),
                pltpu.VMEM((N_KV_HEADS, nb, 2 * LANES, BLOCK), BF16),
            ],
        ),
        out_shape=jax.ShapeDtypeStruct((batch * seq_len, D_ATTN), BF16),
        compiler_params=pltpu.CompilerParams(
            dimension_semantics=("arbitrary", "arbitrary"), vmem_limit_bytes=VMEM_LIMIT),
        name="attn",
    )(sink, q, k, v, bias)


def _gelu_tanh(x):
    c = math.sqrt(2.0 / math.pi)
    half = 0.5 * x
    return half + half * jnp.tanh(x * (c + (c * 0.044715) * (x * x)))


def _sigmoid(x):
    return 0.5 * jnp.tanh(0.5 * x) + 0.5


def _silu(x):
    return x * _sigmoid(x)


def _post_kernel(x_ref, yp_ref, zs_ref, ao_ref, za_ref, wglu_ref, bglu_ref, snw_ref, anw_ref,
                 wout_ref, fnw_ref, o_ref, y_scr, b_scr, t_scr):
    s = pl.program_id(2)

    @pl.when(s == 0)
    def _():
        for g in range(N_GROUPS):
            j, g3 = divmod(g, GROUPS_PER_LANE_TILE)
            for th in range(CHUNK_W // LANES):
                tt = yp_ref[g, :, pl.ds(th * LANES, LANES)].T
                for t3 in range(8):
                    t_scr[th * 8 + t3, j, pl.ds(g3 * SSM_GROUP, SSM_GROUP), :] = (
                        tt[t3 * SSM_GROUP:(t3 + 1) * SSM_GROUP, :])
        for t in range(CHUNK):
            for j in range(LANE_TILES):
                b_scr[t, :, pl.ds(j * LANES, LANES)] = t_scr[t, j].T
        perm = _pair_perm(t_major_out=False)
        for cp in range(TILE_CHUNKS // 2):
            a = jnp.concatenate(
                [b_scr[t, pl.ds(cp * 2 * SEQ_TILE, 2 * SEQ_TILE), :] for t in range(CHUNK)], axis=0)
            ys = jnp.dot(perm, a, preferred_element_type=F32)
            for c2 in range(2):
                for sq in range(SEQ_TILE):
                    r0 = (c2 * SEQ_TILE + sq) * CHUNK
                    y_scr[pl.ds(sq * TOK_TILE + (2 * cp + c2) * CHUNK, CHUNK), :] = ys[r0:r0 + CHUNK, :]

    def get(ref):
        return ref[...].reshape(SUB, ref.shape[-1]).astype(F32)

    y = y_scr[pl.ds(pl.multiple_of(s * SUB, SUB), SUB), :]
    g = _gelu_tanh(y)
    gate = _sigmoid(jnp.dot(g.astype(BF16), wglu_ref[...], preferred_element_type=F32) + bglu_ref[...])
    def gate_of(ref):
        return _silu(ref[...].reshape(SUB, ref.shape[-1])).astype(F32)

    n_ssm = _rms(g * gate, snw_ref[...]) * gate_of(zs_ref)
    n_attn = _rms(get(ao_ref), anw_ref[...]) * gate_of(za_ref)
    mixed = jnp.concatenate([n_ssm, n_attn], axis=1).astype(BF16)
    res = get(x_ref) + jnp.dot(mixed, wout_ref[...], preferred_element_type=F32)
    o_ref[...] = _rms(res, fnw_ref[...]).reshape(o_ref.shape)


def _post_call(x, yp, zs, ao, za, w_glu, b_glu, ssm_norm_w, attn_norm_w, w_out, final_norm_w):
    batch, seq_len, _ = x.shape
    nsb, ncb = batch // SEQ_TILE, seq_len // TOK_TILE
    return pl.pallas_call(
        _post_kernel,
        grid=(nsb, ncb, N_SUB),
        in_specs=[
            _tile_specs(D_MODEL), _CHUNK_TILE_SPEC, _tile_specs(D_SSM), _tile_specs(D_ATTN), _tile_specs(D_ATTN),
            _const_spec((D_SSM, D_SSM)), _const_spec((1, D_SSM)), _const_spec((1, D_SSM)),
            _const_spec((1, D_ATTN)), _const_spec((D_MODEL, D_MODEL)), _const_spec((1, D_MODEL)),
        ],
        out_specs=_tile_specs(D_MODEL),
        out_shape=jax.ShapeDtypeStruct((batch, seq_len, D_MODEL), F32),
        scratch_shapes=[
            pltpu.VMEM((TILE_ROWS, D_SSM), F32),
            pltpu.VMEM((CHUNK, LANES, D_SSM), BF16),
            pltpu.VMEM((CHUNK, LANE_TILES, LANES, LANES), BF16),
        ],
        compiler_params=pltpu.CompilerParams(
            dimension_semantics=("arbitrary", "arbitrary", "arbitrary"), vmem_limit_bytes=VMEM_LIMIT),
        name="post",
    )(x, yp, zs, ao, za, w_glu, b_glu, ssm_norm_w, attn_norm_w, w_out, final_norm_w)


def _cmul(ar, ai, br, bi):
    return ar * br - ai * bi, ar * bi + ai * br


def _ops_kernel(row_ref, col_ref, bt_ref, ct_ref, dsk_ref, toep_ref, bs_ref, cs_ref, a_ref):
    def discretise(lr, li, ls):
        dt = jnp.exp(ls)
        mag = jnp.exp(lr * dt)
        return mag * jnp.cos(li * dt), mag * jnp.sin(li * dt)

    lr, li = row_ref[0], row_ref[1]
    lb_re, lb_im = discretise(lr, li, row_ref[2])
    den = lr * lr + li * li
    nr = lb_re - 1.0
    coef_re = (nr * lr + lb_im * li) / den
    coef_im = (lb_im * lr - nr * li) / den
    bb_re, bb_im = _cmul(coef_re, coef_im, bt_ref[0], bt_ref[1])
    bb_re = jnp.concatenate([bb_re] * CHUNK, axis=0)
    bb_im = jnp.concatenate([bb_im] * CHUNK, axis=0)

    row_t = lax.broadcasted_iota(jnp.int32, (CHUNK_W, LANES), 0) // SSM_GROUP
    fwd_lane = lax.broadcasted_iota(jnp.int32, (CHUNK_W, LANES), 1) < SSM_STATE
    col_t = lax.broadcasted_iota(jnp.int32, (LANES, CHUNK_W), 1) // SSM_GROUP
    fwd_row = lax.broadcasted_iota(jnp.int32, (LANES, CHUNK_W), 0) < SSM_STATE
    ct_re, ct_im = ct_ref[0], ct_ref[1]
    w_re = jnp.zeros((CHUNK_W, LANES), F32)
    w_im = jnp.zeros((CHUNK_W, LANES), F32)
    toep = jnp.zeros((CHUNK_W, CHUNK_W), F32)
    for i in range(CHUNK):
        t_now = jnp.where(fwd_lane, i, CHUNK - 1 - i)
        w_re, w_im = _cmul(w_re, w_im, lb_re, lb_im)
        w_re = w_re + jnp.where(row_t == t_now, bb_re, 0.0)
        w_im = w_im + jnp.where(row_t == t_now, bb_im, 0.0)
        out_now = col_t == jnp.where(fwd_row, i, CHUNK - 1 - i)
        rhs = jnp.concatenate([jnp.where(out_now, ct_re, 0.0), jnp.where(out_now, -ct_im, 0.0)], axis=0)
        lhs = jnp.concatenate([w_re, w_im], axis=1)
        toep = toep + jnp.dot(lhs.astype(BF16), rhs.astype(BF16), preferred_element_type=F32)
    r = lax.broadcasted_iota(jnp.int32, (CHUNK_W, CHUNK_W), 0)
    c = lax.broadcasted_iota(jnp.int32, (CHUNK_W, CHUNK_W), 1)
    toep_ref[...] = (toep + jnp.where(r == c, dsk_ref[...], 0.0)).astype(BF16)
    bs_ref[...] = jnp.concatenate([w_re, w_im], axis=1).astype(BF16)

    pr, pi = discretise(col_ref[0], col_ref[1], col_ref[2])
    k = jnp.where(fwd_row, col_t + 1, CHUNK - col_t)
    acc_re = jnp.ones((LANES, CHUNK_W), F32)
    acc_im = jnp.zeros((LANES, CHUNK_W), F32)
    for bit in range(CHUNK.bit_length()):
        nre, nim = _cmul(acc_re, acc_im, pr, pi)
        take = (k & (1 << bit)) != 0
        acc_re = jnp.where(take, nre, acc_re)
        acc_im = jnp.where(take, nim, acc_im)
        pr, pi = _cmul(pr, pi, pr, pi)
    cp_re, cp_im = _cmul(ct_re, ct_im, acc_re, acc_im)
    cs_ref[...] = jnp.concatenate([cp_re, -cp_im], axis=0).astype(BF16)

    a_re, a_im = lb_re, lb_im
    for _ in range(CHUNK.bit_length() - 1):
        a_re, a_im = _cmul(a_re, a_im, a_re, a_im)
    a_ref[0] = a_re
    a_ref[1] = a_im


def _ssm_operators(lam_re, lam_im, log_step, b_re, b_im, c_re, c_im, d_skip):
    g_first = lambda a: jnp.transpose(a.astype(F32), (1, 0, 2)).reshape(N_GROUPS, 2 * SSM_STATE)
    params = jnp.stack([g_first(lam_re), g_first(lam_im),
                        jnp.repeat(log_step.astype(F32).T, SSM_STATE, axis=1)], axis=1)
    b_t = lambda b: jnp.transpose(b.astype(F32), (1, 3, 0, 2)).reshape(N_GROUPS, SSM_GROUP, 2 * SSM_STATE)
    c_t = lambda c: jnp.tile(
        jnp.transpose(c.astype(F32), (1, 0, 3, 2)).reshape(N_GROUPS, 2 * SSM_STATE, SSM_GROUP), (1, 1, CHUNK))
    bt = jnp.stack([b_t(b_re), b_t(b_im)], axis=1)
    ct = jnp.stack([c_t(c_re), c_t(c_im)], axis=1)
    dsk = jnp.tile(d_skip.astype(F32), (1, CHUNK))[:, None, :]
    grp = lambda *shape: pl.BlockSpec((None,) + shape, lambda g: (g,) + (0,) * len(shape))
    mat = jax.ShapeDtypeStruct((N_GROUPS, CHUNK_W, CHUNK_W), BF16)
    toep, bs, cs, a = pl.pallas_call(
        _ops_kernel,
        grid=(N_GROUPS,),
        in_specs=[grp(3, 1, LANES), grp(3, LANES, 1), grp(2, SSM_GROUP, LANES), grp(2, LANES, CHUNK_W),
                  grp(1, CHUNK_W)],
        out_specs=(grp(CHUNK_W, CHUNK_W), grp(CHUNK_W, STATE_W), grp(STATE_W, CHUNK_W), grp(2, 1, LANES)),
        out_shape=(mat, mat, mat, jax.ShapeDtypeStruct((N_GROUPS, 2, 1, LANES), F32)),
        compiler_params=pltpu.CompilerParams(dimension_semantics=("arbitrary",)),
        name="s5_ops",
    )(params[:, :, None, :], params[:, :, :, None], bt, ct, dsk)
    return toep, bs, cs, a[:, 0], a[:, 1]


def _t5_buckets_np(rel):
    half = N_BUCKETS // 2
    max_exact = half // 2
    ret = np.where(rel > 0, half, 0)
    n = np.abs(rel)
    nf = np.maximum(n, 1).astype(np.float64)
    large = max_exact + (np.log(nf / max_exact) / math.log(MAX_DISTANCE / max_exact)
                         * (half - max_exact)).astype(np.int32)
    large = np.minimum(large, half - 1)
    return ret + np.where(n < max_exact, n, large)


def _attn_tables(rel_bias):
    offsets = (0, -BLOCK, -2 * BLOCK)
    dist = np.arange(BIAS_PERIOD)
    dist = np.where(dist < KEYS, dist, dist - BIAS_PERIOD)
    rel = np.stack([dist + off for off in offsets])
    per_dist = rel_bias.astype(F32)[_t5_buckets_np(rel)] * LOG2_E
    per_dist = jnp.where((np.abs(rel) <= WINDOW)[:, :, None], per_dist, NEG_INF)
    per_dist = jnp.transpose(per_dist, (0, 2, 1))
    neg = jnp.concatenate([per_dist[:, :, :1], per_dist[:, :, :0:-1]], axis=-1)
    return pl.pallas_call(
        _bias_kernel,
        grid=(len(offsets),),
        in_specs=[pl.BlockSpec((None, N_HEADS, 1, BIAS_PERIOD), lambda v: (v, 0, 0, 0))],
        out_specs=pl.BlockSpec((None, N_KV_HEADS, 2 * KEYS, 2 * LANES), lambda v: (v, 0, 0, 0)),
        out_shape=jax.ShapeDtypeStruct((len(offsets), N_KV_HEADS, 2 * KEYS, 2 * LANES), F32),
        compiler_params=pltpu.CompilerParams(dimension_semantics=("arbitrary",)),
        name="bias",
    )(neg[:, :, None, :])


BIAS_PERIOD = 4 * BLOCK


def _bias_kernel(neg_ref, o_ref):
    for kh in range(N_KV_HEADS):
        for tile in range(2):
            for slot in range(2):
                rows = jnp.broadcast_to(neg_ref[4 * kh + 2 * tile + slot], (KEYS, BIAS_PERIOD))
                skewed = pltpu.roll(rows, 0, 1, stride=1, stride_axis=0)
                o_ref[kh, pl.ds(slot * KEYS, KEYS), pl.ds(tile * LANES, LANES)] = skewed[:, :BLOCK]


def kernel(x_prompt, x_sample, norm_w, w_in, lam_re, lam_im, log_step, b_re, b_im, c_re, c_im, d_skip,
           w_glu, b_glu, ssm_norm_w, sink, attn_norm_w, w_out, rel_bias, final_norm_w):
    assert norm_w.shape[0] == 1, "single-layer encoder"
    ops = _ssm_operators(lam_re[0], lam_im[0], log_step[0], b_re[0], b_im[0], c_re[0], c_im[0], d_skip[0])
    bias = _attn_tables(rel_bias)
    w_in_b = w_in[0].astype(BF16)
    w_glu_b = w_glu[0].astype(BF16)
    w_out_b = w_out[0].astype(BF16)
    sink_f = sink[0].astype(F32) * LOG2_E

    outs = []
    for x in (x_prompt, x_sample):
        batch, seq_len, _ = x.shape
        assert batch % SEQ_TILE == 0 and seq_len % TOK_TILE == 0
        n_tok = batch * seq_len
        up, zs, q, k, v, za = _proj_call(x, norm_w, w_in_b)
        yp = _ssm_call(up, ops)
        ao = _attn_call(q.reshape(n_tok, D_ATTN), k.reshape(n_tok, KV_W), v.reshape(n_tok, KV_W),
                        bias, sink_f, batch, seq_len)
        outs.append(_post_call(x, yp, zs, ao.reshape(batch, seq_len, D_ATTN), za, w_glu_b, b_glu, ssm_norm_w,
                               attn_norm_w, w_out_b, final_norm_w.reshape(1, D_MODEL)))
    return tuple(outs)
```

```python
import functools
import math

import numpy as np
import jax
import jax.numpy as jnp
from jax import lax
from jax.experimental import pallas as pl
from jax.experimental.pallas import tpu as pltpu

F32 = jnp.float32
BF16 = jnp.bfloat16

D_MODEL = 1024
D_SSM = 512
D_ATTN = 512
SSM_GROUP = 16
N_GROUPS = D_SSM // SSM_GROUP
SSM_STATE = 64
HEAD_DIM = 64
N_HEADS = D_ATTN // HEAD_DIM
N_KV_HEADS = 2
WINDOW = 128
BLOCK = 128
N_BUCKETS = 32
MAX_DISTANCE = 128
RMS_EPS = 1e-6
NEG_INF = -1e30
LOG2_E = math.log2(math.e)

LANES = 128
SUBLANES = 8
CHUNK = 16
CHUNK_W = CHUNK * SSM_GROUP
STATE_W = 4 * SSM_STATE
SEQ_TILE = SUBLANES
TILE_CHUNKS = LANES // SEQ_TILE
TOK_TILE = TILE_CHUNKS * CHUNK
TILE_ROWS = SEQ_TILE * TOK_TILE
SUB_SEQS = 4
SUB = SUB_SEQS * TOK_TILE
N_SUB = SEQ_TILE // SUB_SEQS
LANE_TILES = D_SSM // LANES
GROUPS_PER_LANE_TILE = LANES // SSM_GROUP
KEYS = 3 * BLOCK
Q_ROWS = 1024
BLOCKS_PER_ITER = 2
KV_W = N_KV_HEADS * HEAD_DIM
MM_ROWS = 1024
VMEM_LIMIT = 56 * 1024 * 1024

_C_U, _C_ZS, _C_Q, _C_K, _C_V, _C_ZA, _C_END = 0, 512, 1024, 1536, 1664, 1792, 2304


def _rms(x, w):
    return (x * lax.rsqrt(jnp.mean(x * x, axis=-1, keepdims=True) + RMS_EPS)) * w


def _tile_specs(width):
    return pl.BlockSpec((SUB_SEQS, TOK_TILE, width), lambda sb, cb, s: (sb * N_SUB + s, cb, 0))


_CHUNK_TILE_SPEC = pl.BlockSpec((N_GROUPS, None, None, LANES, CHUNK_W), lambda sb, cb, s: (0, sb, cb, 0, 0))


def _const_spec(shape):
    return pl.BlockSpec(shape, lambda sb, cb, s: (0,) * len(shape))


PAIR_ROWS = 2 * SEQ_TILE * CHUNK


def _pair_perm(t_major_out):
    out_row = lax.broadcasted_iota(jnp.int32, (PAIR_ROWS, PAIR_ROWS), 0)
    in_row = lax.broadcasted_iota(jnp.int32, (PAIR_ROWS, PAIR_ROWS), 1)
    tcs, cst = (out_row, in_row) if t_major_out else (in_row, out_row)
    hit = (tcs // (2 * SEQ_TILE) == cst % CHUNK) & (tcs % (2 * SEQ_TILE) == cst // CHUNK)
    return jnp.where(hit, 1.0, 0.0).astype(BF16)


def _proj_kernel(x_ref, nw_ref, w_ref, up_ref, zs_ref, q_ref, k_ref, v_ref, za_ref, u_scr, b_scr, t_scr):
    s = pl.program_id(2)
    hb = _rms(x_ref[...].reshape(SUB, D_MODEL), nw_ref[...]).astype(BF16)

    def mm(lo, hi):
        return jnp.dot(hb, w_ref[:, lo:hi], preferred_element_type=F32)

    def put(ref, val):
        ref[...] = val.astype(BF16).reshape(ref.shape)

    uz = mm(_C_U, _C_Q)
    u_scr[pl.ds(pl.multiple_of(s * SUB, SUB), SUB), :] = uz[:, :D_SSM].astype(BF16)
    put(zs_ref, uz[:, D_SSM:])
    qkv = mm(_C_Q, _C_ZA)
    put(q_ref, qkv[:, :D_ATTN] * (HEAD_DIM ** -0.5 * LOG2_E))
    put(k_ref, qkv[:, D_ATTN:D_ATTN + KV_W])
    put(v_ref, qkv[:, D_ATTN + KV_W:])
    put(za_ref, mm(_C_ZA, _C_END))

    @pl.when(s == N_SUB - 1)
    def _():
        perm = _pair_perm(t_major_out=True)
        for cp in range(TILE_CHUNKS // 2):
            a = jnp.concatenate(
                [u_scr[pl.ds(sq * TOK_TILE + (2 * cp + c2) * CHUNK, CHUNK), :]
                 for c2 in range(2) for sq in range(SEQ_TILE)], axis=0)
            b_scr[cp] = jnp.dot(perm, a, preferred_element_type=F32).astype(BF16)
        for t in range(CHUNK):
            for j in range(LANE_TILES):
                blk = jnp.concatenate(
                    [b_scr[cp, pl.ds(t * 2 * SEQ_TILE, 2 * SEQ_TILE), pl.ds(j * LANES, LANES)]
                     for cp in range(TILE_CHUNKS // 2)], axis=0)
                t_scr[t, j] = blk.T
        for g in range(N_GROUPS):
            j, g3 = divmod(g, GROUPS_PER_LANE_TILE)
            for th in range(CHUNK_W // LANES):
                rows = jnp.concatenate(
                    [t_scr[th * 8 + t3, j, pl.ds(g3 * SSM_GROUP, SSM_GROUP), :] for t3 in range(8)], axis=0)
                up_ref[g, :, pl.ds(th * LANES, LANES)] = rows.T


def _proj_call(x, norm_w, w_in):
    batch, seq_len, _ = x.shape
    nsb, ncb = batch // SEQ_TILE, seq_len // TOK_TILE
    tok = lambda width: jax.ShapeDtypeStruct((batch, seq_len, width), BF16)
    out_shapes = (
        jax.ShapeDtypeStruct((N_GROUPS, nsb, ncb, LANES, CHUNK_W), BF16),
        tok(D_SSM),
        tok(D_ATTN),
        tok(KV_W),
        tok(KV_W),
        tok(D_ATTN),
    )
    return pl.pallas_call(
        _proj_kernel,
        grid=(nsb, ncb, N_SUB),
        in_specs=[_tile_specs(D_MODEL), _const_spec((1, D_MODEL)), _const_spec((D_MODEL, _C_END))],
        out_specs=(_CHUNK_TILE_SPEC, _tile_specs(D_SSM), _tile_specs(D_ATTN), _tile_specs(KV_W),
                   _tile_specs(KV_W), _tile_specs(D_ATTN)),
        out_shape=out_shapes,
        scratch_shapes=[
            pltpu.VMEM((TILE_ROWS, D_SSM), BF16),
            pltpu.VMEM((TILE_CHUNKS // 2, PAIR_ROWS, D_SSM), BF16),
            pltpu.VMEM((CHUNK, LANE_TILES, LANES, LANES), BF16),
        ],
        compiler_params=pltpu.CompilerParams(
            dimension_semantics=("arbitrary", "arbitrary", "arbitrary"), vmem_limit_bytes=VMEM_LIMIT),
        name="proj",
    )(x, norm_w, w_in)


def _ssm_kernel(u_ref, toep_ref, bs_ref, cs_ref, are_ref, aim_ref, y_ref, s_scr, xf_scr, xb_scr,
                *, nsb, ncb):
    nch = ncb * TILE_CHUNKS
    tiles_per_mm = MM_ROWS // LANES
    seq_rows = nsb * SEQ_TILE

    def mm_blocks():
        for sb in range(nsb):
            for cq in range(ncb // tiles_per_mm):
                yield sb, cq * tiles_per_mm, pl.ds((sb * ncb + cq * tiles_per_mm) * LANES, MM_ROWS)

    def load_u(sb, cb0):
        return jnp.concatenate([u_ref[sb, cb0 + i] for i in range(tiles_per_mm)], axis=0)

    for sb, cb0, rows in mm_blocks():
        st = jnp.dot(load_u(sb, cb0), bs_ref[...], preferred_element_type=F32)
        s_scr[0, rows, :] = st[:, :LANES]
        s_scr[1, rows, :] = st[:, LANES:]

    def chunk_rows(sb, ch):
        return pl.ds(pl.multiple_of(sb * nch * SEQ_TILE + ch * SEQ_TILE, SEQ_TILE), SEQ_TILE)

    def load(ref, part, ch):
        return jnp.concatenate([ref[part, chunk_rows(sb, ch), :] for sb in range(nsb)], axis=0)

    def store(ref, part, ch, val):
        for sb in range(nsb):
            ref[part, chunk_rows(sb, ch), :] = val[sb * SEQ_TILE:(sb + 1) * SEQ_TILE, :]

    are = jnp.broadcast_to(are_ref[...], (seq_rows, LANES))
    aim = jnp.broadcast_to(aim_ref[...], (seq_rows, LANES))
    is_fwd = lax.broadcasted_iota(jnp.int32, (seq_rows, LANES), 1) < SSM_STATE

    def step(i, carry):
        xr, xi = carry
        j = nch - 1 - i
        store(xf_scr, 0, i, xr)
        store(xf_scr, 1, i, xi)
        store(xb_scr, 0, j, xr)
        store(xb_scr, 1, j, xi)
        s_re = jnp.where(is_fwd, load(s_scr, 0, i), load(s_scr, 0, j))
        s_im = jnp.where(is_fwd, load(s_scr, 1, i), load(s_scr, 1, j))
        return (are * xr - aim * xi + s_re, are * xi + aim * xr + s_im)

    zero = jnp.zeros((seq_rows, LANES), F32)
    lax.fori_loop(0, nch, step, (zero, zero), unroll=8)

    fwd_lane = lax.broadcasted_iota(jnp.int32, (MM_ROWS, LANES), 1) < SSM_STATE
    for sb, cb0, rows in mm_blocks():
        xin = jnp.concatenate(
            [jnp.where(fwd_lane, xf_scr[c, rows, :], xb_scr[c, rows, :]) for c in range(2)], axis=1).astype(BF16)
        y = (jnp.dot(load_u(sb, cb0), toep_ref[...], preferred_element_type=F32)
             + jnp.dot(xin, cs_ref[...], preferred_element_type=F32))
        for i in range(tiles_per_mm):
            y_ref[sb, cb0 + i] = y[i * LANES:(i + 1) * LANES, :].astype(BF16)


def _ssm_call(up, ops):
    toep, bs, cs, are, aim = ops
    _, nsb, ncb, _, _ = up.shape
    n_rows = nsb * ncb * LANES
    assert ncb % (MM_ROWS // LANES) == 0
    data_spec = pl.BlockSpec((None, nsb, ncb, LANES, CHUNK_W), lambda g: (g, 0, 0, 0, 0))
    grp = lambda *shape: pl.BlockSpec((None,) + shape, lambda g: (g, 0, 0))
    return pl.pallas_call(
        functools.partial(_ssm_kernel, nsb=nsb, ncb=ncb),
        grid=(N_GROUPS,),
        in_specs=[data_spec, grp(CHUNK_W, CHUNK_W), grp(CHUNK_W, STATE_W), grp(STATE_W, CHUNK_W),
                  grp(1, LANES), grp(1, LANES)],
        out_specs=data_spec,
        out_shape=jax.ShapeDtypeStruct(up.shape, BF16),
        scratch_shapes=[pltpu.VMEM((2, n_rows, LANES), F32)] * 3,
        compiler_params=pltpu.CompilerParams(
            dimension_semantics=("arbitrary",), vmem_limit_bytes=VMEM_LIMIT),
        name="ssm",
    )(up, toep, bs, cs, are, aim)


def _attn_kernel(sink_ref, q_ref, k_ref, v_ref, bias_ref, o_ref, ka_scr, kb_scr, wa_scr, wb_scr, *, seq_len):
    jb = pl.program_id(1)
    nb = seq_len // BLOCK

    @pl.when((pl.program_id(0) == 0) & (jb == 0))
    def _():
        zero_rows = jnp.zeros((HEAD_DIM, BLOCK), BF16)
        one_rows = jnp.ones((HEAD_DIM, BLOCK), BF16)

        def per_block(r, carry):
            for kh in range(N_KV_HEADS):
                wa_scr[kh, r, pl.ds(HEAD_DIM, 3 * HEAD_DIM), :] = jnp.concatenate(
                    [zero_rows, one_rows, zero_rows], axis=0)
                wb_scr[kh, r, pl.ds(0, HEAD_DIM), :] = zero_rows
                wb_scr[kh, r, pl.ds(2 * HEAD_DIM, 2 * HEAD_DIM), :] = jnp.concatenate([zero_rows, one_rows], axis=0)
            return carry

        lax.fori_loop(0, nb, per_block, 0)

    @pl.when(jb == 0)
    def _():
        low = lax.broadcasted_iota(jnp.int32, (BLOCK, LANES), 1) < HEAD_DIM

        def per_block(r, carry):
            rows = pl.ds(pl.multiple_of(r * BLOCK, BLOCK), BLOCK)
            k = k_ref[rows, :]
            k_swapped = pltpu.roll(k, HEAD_DIM, 1)
            zero = jnp.zeros_like(k)
            ka_scr[0, rows, :] = jnp.where(low, k, zero)
            kb_scr[0, rows, :] = jnp.where(low, zero, k_swapped)
            ka_scr[1, rows, :] = jnp.where(low, k_swapped, zero)
            kb_scr[1, rows, :] = jnp.where(low, zero, k)
            vt = v_ref[rows, :].T
            for kh in range(N_KV_HEADS):
                vt_kh = vt[kh * HEAD_DIM:(kh + 1) * HEAD_DIM]
                wa_scr[kh, r, pl.ds(0, HEAD_DIM), :] = vt_kh
                wb_scr[kh, r, pl.ds(HEAD_DIM, HEAD_DIM), :] = vt_kh
            return carry

        lax.fori_loop(0, nb, per_block, 0)

    col_tile0 = lax.broadcasted_iota(jnp.int32, (1, 2 * LANES), 1) < LANES
    def query_block(sb):
        n = jb * (Q_ROWS // BLOCK) + sb
        kb0 = jnp.clip(n - 1, 0, nb - KEYS // BLOCK)
        ks = pl.multiple_of(kb0 * BLOCK, BLOCK)
        var = jnp.where(n == 0, 0, jnp.where(n == nb - 1, 2, 1))
        qrows = pl.ds(pl.multiple_of(sb * BLOCK, BLOCK), BLOCK)
        for kh in range(N_KV_HEADS):
            kcat = jnp.concatenate([ka_scr[kh, pl.ds(ks, KEYS), :], kb_scr[kh, pl.ds(ks, KEYS), :]], axis=0)
            q2 = jnp.concatenate([q_ref[qrows, pl.ds((2 * kh + jj) * LANES, LANES)] for jj in range(2)], axis=0)
            t = lax.dot_general(kcat, q2, (((1,), (1,)), ((), ())), preferred_element_type=F32)
            t = t + bias_ref[var, kh]
            ps, sink_terms = [], []
            for hh in range(2):
                th = t[hh * KEYS:(hh + 1) * KEYS]
                sink = jnp.where(col_tile0, sink_ref[4 * kh + hh], sink_ref[4 * kh + 2 + hh])
                m = jnp.maximum(jnp.max(th, axis=0, keepdims=True), sink)
                ps.append(jnp.exp2(th - m).astype(BF16))
                sink_terms.append(jnp.exp2(sink - m))
            w = jnp.concatenate([wa_scr[kh, kb0 + i] for i in range(KEYS // BLOCK)]
                                + [wb_scr[kh, kb0 + i] for i in range(KEYS // BLOCK)], axis=1)
            o2 = jnp.dot(w, jnp.concatenate(ps, axis=0), preferred_element_type=F32)
            invs = [1.0 / (o2[LANES + hh * HEAD_DIM:LANES + hh * HEAD_DIM + SUBLANES] + sink_terms[hh])
                    for hh in range(2)]
            inv = jnp.concatenate([invs[hh] for hh in range(2) for _ in range(HEAD_DIM // SUBLANES)], axis=0)
            on = o2[:LANES] * inv
            for jj in range(2):
                o_ref[qrows, pl.ds((2 * kh + jj) * LANES, LANES)] = (
                    on[:, jj * LANES:(jj + 1) * LANES].T.astype(BF16))

    def query_blocks(i, carry):
        for sb in range(BLOCKS_PER_ITER):
            query_block(i * BLOCKS_PER_ITER + sb)
        return carry

    lax.fori_loop(0, Q_ROWS // (BLOCKS_PER_ITER * BLOCK), query_blocks, 0)


def _attn_call(q, k, v, bias, sink, batch, seq_len):
    assert seq_len % Q_ROWS == 0 and seq_len >= KEYS
    nq = seq_len // Q_ROWS
    nb = seq_len // BLOCK
    return pl.pallas_call(
        functools.partial(_attn_kernel, seq_len=seq_len),
        grid_spec=pltpu.PrefetchScalarGridSpec(
            num_scalar_prefetch=1,
            grid=(batch, nq),
            in_specs=[
                pl.BlockSpec((Q_ROWS, D_ATTN), lambda b, j, s: (b * nq + j, 0)),
                pl.BlockSpec((seq_len, KV_W), lambda b, j, s: (b, 0)),
                pl.BlockSpec((seq_len, KV_W), lambda b, j, s: (b, 0)),
                pl.BlockSpec((3, N_KV_HEADS, 2 * KEYS, 2 * LANES), lambda b, j, s: (0, 0, 0, 0)),
            ],
            out_specs=pl.BlockSpec((Q_ROWS, D_ATTN), lambda b, j, s: (b * nq + j, 0)),
            scratch_shapes=[
                pltpu.VMEM((N_KV_HEADS, seq_len, LANES), BF16),
                pltpu.VMEM((N_KV_HEADS, seq_len, LANES), BF16),
                pltpu.VMEM((N_KV_HEADS, nb, 2 * LANES, BLOCK), BF16),
                pltpu.VMEM((N_KV_HEADS, nb, 2 * LANES, BLOCK), BF16),
            ],
        ),
        out_shape=jax.ShapeDtypeStruct((batch * seq_len, D_ATTN), BF16),
        compiler_params=pltpu.CompilerParams(
            dimension_semantics=("arbitrary", "arbitrary"), vmem_limit_bytes=VMEM_LIMIT),
        name="attn",
    )(sink, q, k, v, bias)


def _gelu_tanh(x):
    c = math.sqrt(2.0 / math.pi)
    half = 0.5 * x
    return half + half * jnp.tanh(x * (c + (c * 0.044715) * (x * x)))


def _sigmoid(x):
    return 0.5 * jnp.tanh(0.5 * x) + 0.5


def _silu(x):
    return x * _sigmoid(x)


def _post_kernel(x_ref, yp_ref, zs_ref, ao_ref, za_ref, wglu_ref, bglu_ref, snw_ref, anw_ref,
                 wout_ref, fnw_ref, o_ref, y_scr, b_scr, t_scr):
    s = pl.program_id(2)

    @pl.when(s == 0)
    def _():
        for g in range(N_GROUPS):
            j, g3 = divmod(g, GROUPS_PER_LANE_TILE)
            for th in range(CHUNK_W // LANES):
                tt = yp_ref[g, :, pl.ds(th * LANES, LANES)].T
                for t3 in range(8):
                    t_scr[th * 8 + t3, j, pl.ds(g3 * SSM_GROUP, SSM_GROUP), :] = (
                        tt[t3 * SSM_GROUP:(t3 + 1) * SSM_GROUP, :])
        for t in range(CHUNK):
            for j in range(LANE_TILES):
                b_scr[t, :, pl.ds(j * LANES, LANES)] = t_scr[t, j].T
        perm = _pair_perm(t_major_out=False)
        for cp in range(TILE_CHUNKS // 2):
            a = jnp.concatenate(
                [b_scr[t, pl.ds(cp * 2 * SEQ_TILE, 2 * SEQ_TILE), :] for t in range(CHUNK)], axis=0)
            ys = jnp.dot(perm, a, preferred_element_type=F32)
            for c2 in range(2):
                for sq in range(SEQ_TILE):
                    r0 = (c2 * SEQ_TILE + sq) * CHUNK
                    y_scr[pl.ds(sq * TOK_TILE + (2 * cp + c2) * CHUNK, CHUNK), :] = ys[r0:r0 + CHUNK, :]

    def get(ref):
        return ref[...].reshape(SUB, ref.shape[-1]).astype(F32)

    y = y_scr[pl.ds(pl.multiple_of(s * SUB, SUB), SUB), :]
    g = _gelu_tanh(y)
    gate = _sigmoid(jnp.dot(g.astype(BF16), wglu_ref[...], preferred_element_type=F32) + bglu_ref[...])
    def gate_of(ref):
        return _silu(ref[...].reshape(SUB, ref.shape[-1])).astype(F32)

    n_ssm = _rms(g * gate, snw_ref[...]) * gate_of(zs_ref)
    n_attn = _rms(get(ao_ref), anw_ref[...]) * gate_of(za_ref)
    mixed = jnp.concatenate([n_ssm, n_attn], axis=1).astype(BF16)
    res = get(x_ref) + jnp.dot(mixed, wout_ref[...], preferred_element_type=F32)
    o_ref[...] = _rms(res, fnw_ref[...]).reshape(o_ref.shape)


def _post_call(x, yp, zs, ao, za, w_glu, b_glu, ssm_norm_w, attn_norm_w, w_out, final_norm_w):
    batch, seq_len, _ = x.shape
    nsb, ncb = batch // SEQ_TILE, seq_len // TOK_TILE
    return pl.pallas_call(
        _post_kernel,
        grid=(nsb, ncb, N_SUB),
        in_specs=[
            _tile_specs(D_MODEL), _CHUNK_TILE_SPEC, _tile_specs(D_SSM), _tile_specs(D_ATTN), _tile_specs(D_ATTN),
            _const_spec((D_SSM, D_SSM)), _const_spec((1, D_SSM)), _const_spec((1, D_SSM)),
            _const_spec((1, D_ATTN)), _const_spec((D_MODEL, D_MODEL)), _const_spec((1, D_MODEL)),
        ],
        out_specs=_tile_specs(D_MODEL),
        out_shape=jax.ShapeDtypeStruct((batch, seq_len, D_MODEL), F32),
        scratch_shapes=[
            pltpu.VMEM((TILE_ROWS, D_SSM), F32),
            pltpu.VMEM((CHUNK, LANES, D_SSM), BF16),
            pltpu.VMEM((CHUNK, LANE_TILES, LANES, LANES), BF16),
        ],
        compiler_params=pltpu.CompilerParams(
            dimension_semantics=("arbitrary", "arbitrary", "arbitrary"), vmem_limit_bytes=VMEM_LIMIT),
        name="post",
    )(x, yp, zs, ao, za, w_glu, b_glu, ssm_norm_w, attn_norm_w, w_out, final_norm_w)


def _cmul(ar, ai, br, bi):
    return ar * br - ai * bi, ar * bi + ai * br


def _ops_kernel(row_ref, col_ref, bt_ref, ct_ref, dsk_ref, toep_ref, bs_ref, cs_ref, a_ref):
    def discretise(lr, li, ls):
        dt = jnp.exp(ls)
        mag = jnp.exp(lr * dt)
        return mag * jnp.cos(li * dt), mag * jnp.sin(li * dt)

    lr, li = row_ref[0], row_ref[1]
    lb_re, lb_im = discretise(lr, li, row_ref[2])
    den = lr * lr + li * li
    nr = lb_re - 1.0
    coef_re = (nr * lr + lb_im * li) / den
    coef_im = (lb_im * lr - nr * li) / den
    bb_re, bb_im = _cmul(coef_re, coef_im, bt_ref[0], bt_ref[1])
    bb_re = jnp.concatenate([bb_re] * CHUNK, axis=0)
    bb_im = jnp.concatenate([bb_im] * CHUNK, axis=0)

    row_t = lax.broadcasted_iota(jnp.int32, (CHUNK_W, LANES), 0) // SSM_GROUP
    fwd_lane = lax.broadcasted_iota(jnp.int32, (CHUNK_W, LANES), 1) < SSM_STATE
    col_t = lax.broadcasted_iota(jnp.int32, (LANES, CHUNK_W), 1) // SSM_GROUP
    fwd_row = lax.broadcasted_iota(jnp.int32, (LANES, CHUNK_W), 0) < SSM_STATE
    ct_re, ct_im = ct_ref[0], ct_ref[1]
    w_re = jnp.zeros((CHUNK_W, LANES), F32)
    w_im = jnp.zeros((CHUNK_W, LANES), F32)
    toep = jnp.zeros((CHUNK_W, CHUNK_W), F32)
    for i in range(CHUNK):
        t_now = jnp.where(fwd_lane, i, CHUNK - 1 - i)
        w_re, w_im = _cmul(w_re, w_im, lb_re, lb_im)
        w_re = w_re + jnp.where(row_t == t_now, bb_re, 0.0)
        w_im = w_im + jnp.where(row_t == t_now, bb_im, 0.0)
        out_now = col_t == jnp.where(fwd_row, i, CHUNK - 1 - i)
        rhs = jnp.concatenate([jnp.where(out_now, ct_re, 0.0), jnp.where(out_now, -ct_im, 0.0)], axis=0)
        lhs = jnp.concatenate([w_re, w_im], axis=1)
        toep = toep + jnp.dot(lhs.astype(BF16), rhs.astype(BF16), preferred_element_type=F32)
    r = lax.broadcasted_iota(jnp.int32, (CHUNK_W, CHUNK_W), 0)
    c = lax.broadcasted_iota(jnp.int32, (CHUNK_W, CHUNK_W), 1)
    toep_ref[...] = (toep + jnp.where(r == c, dsk_ref[...], 0.0)).astype(BF16)
    bs_ref[...] = jnp.concatenate([w_re, w_im], axis=1).astype(BF16)

    pr, pi = discretise(col_ref[0], col_ref[1], col_ref[2])
    k = jnp.where(fwd_row, col_t + 1, CHUNK - col_t)
    acc_re = jnp.ones((LANES, CHUNK_W), F32)
    acc_im = jnp.zeros((LANES, CHUNK_W), F32)
    for bit in range(CHUNK.bit_length()):
        nre, nim = _cmul(acc_re, acc_im, pr, pi)
        take = (k & (1 << bit)) != 0
        acc_re = jnp.where(take, nre, acc_re)
        acc_im = jnp.where(take, nim, acc_im)
        pr, pi = _cmul(pr, pi, pr, pi)
    cp_re, cp_im = _cmul(ct_re, ct_im, acc_re, acc_im)
    cs_ref[...] = jnp.concatenate([cp_re, -cp_im], axis=0).astype(BF16)

    a_re, a_im = lb_re, lb_im
    for _ in range(CHUNK.bit_length() - 1):
        a_re, a_im = _cmul(a_re, a_im, a_re, a_im)
    a_ref[0] = a_re
    a_ref[1] = a_im


def _ssm_operators(lam_re, lam_im, log_step, b_re, b_im, c_re, c_im, d_skip):
    g_first = lambda a: jnp.transpose(a.astype(F32), (1, 0, 2)).reshape(N_GROUPS, 2 * SSM_STATE)
    params = jnp.stack([g_first(lam_re), g_first(lam_im),
                        jnp.repeat(log_step.astype(F32).T, SSM_STATE, axis=1)], axis=1)
    b_t = lambda b: jnp.transpose(b.astype(F32), (1, 3, 0, 2)).reshape(N_GROUPS, SSM_GROUP, 2 * SSM_STATE)
    c_t = lambda c: jnp.tile(
        jnp.transpose(c.astype(F32), (1, 0, 3, 2)).reshape(N_GROUPS, 2 * SSM_STATE, SSM_GROUP), (1, 1, CHUNK))
    bt = jnp.stack([b_t(b_re), b_t(b_im)], axis=1)
    ct = jnp.stack([c_t(c_re), c_t(c_im)], axis=1)
    dsk = jnp.tile(d_skip.astype(F32), (1, CHUNK))[:, None, :]
    grp = lambda *shape: pl.BlockSpec((None,) + shape, lambda g: (g,) + (0,) * len(shape))
    mat = jax.ShapeDtypeStruct((N_GROUPS, CHUNK_W, CHUNK_W), BF16)
    toep, bs, cs, a = pl.pallas_call(
        _ops_kernel,
        grid=(N_GROUPS,),
        in_specs=[grp(3, 1, LANES), grp(3, LANES, 1), grp(2, SSM_GROUP, LANES), grp(2, LANES, CHUNK_W),
                  grp(1, CHUNK_W)],
        out_specs=(grp(CHUNK_W, CHUNK_W), grp(CHUNK_W, STATE_W), grp(STATE_W, CHUNK_W), grp(2, 1, LANES)),
        out_shape=(mat, mat, mat, jax.ShapeDtypeStruct((N_GROUPS, 2, 1, LANES), F32)),
        compiler_params=pltpu.CompilerParams(dimension_semantics=("arbitrary",)),
        name="s5_ops",
    )(params[:, :, None, :], params[:, :, :, None], bt, ct, dsk)
    return toep, bs, cs, a[:, 0], a[:, 1]


def _t5_buckets_np(rel):
    half = N_BUCKETS // 2
    max_exact = half // 2
    ret = np.where(rel > 0, half, 0)
    n = np.abs(rel)
    nf = np.maximum(n, 1).astype(np.float64)
    large = max_exact + (np.log(nf / max_exact) / math.log(MAX_DISTANCE / max_exact)
                         * (half - max_exact)).astype(np.int32)
    large = np.minimum(large, half - 1)
    return ret + np.where(n < max_exact, n, large)


def _attn_tables(rel_bias):
    offsets = (0, -BLOCK, -2 * BLOCK)
    dist = np.arange(BIAS_PERIOD)
    dist = np.where(dist < KEYS, dist, dist - BIAS_PERIOD)
    rel = np.stack([dist + off for off in offsets])
    per_dist = rel_bias.astype(F32)[_t5_buckets_np(rel)] * LOG2_E
    per_dist = jnp.where((np.abs(rel) <= WINDOW)[:, :, None], per_dist, NEG_INF)
    per_dist = jnp.transpose(per_dist, (0, 2, 1))
    neg = jnp.concatenate([per_dist[:, :, :1], per_dist[:, :, :0:-1]], axis=-1)
    return pl.pallas_call(
        _bias_kernel,
        grid=(len(offsets),),
        in_specs=[pl.BlockSpec((None, N_HEADS, 1, BIAS_PERIOD), lambda v: (v, 0, 0, 0))],
        out_specs=pl.BlockSpec((None, N_KV_HEADS, 2 * KEYS, 2 * LANES), lambda v: (v, 0, 0, 0)),
        out_shape=jax.ShapeDtypeStruct((len(offsets), N_KV_HEADS, 2 * KEYS, 2 * LANES), F32),
        compiler_params=pltpu.CompilerParams(dimension_semantics=("arbitrary",)),
        name="bias",
    )(neg[:, :, None, :])


BIAS_PERIOD = 4 * BLOCK


def _bias_kernel(neg_ref, o_ref):
    for kh in range(N_KV_HEADS):
        for tile in range(2):
            for slot in range(2):
                rows = jnp.broadcast_to(neg_ref[4 * kh + 2 * tile + slot], (KEYS, BIAS_PERIOD))
                skewed = pltpu.roll(rows, 0, 1, stride=1, stride_axis=0)
                o_ref[kh, pl.ds(slot * KEYS, KEYS), pl.ds(tile * LANES, LANES)] = skewed[:, :BLOCK]


def kernel(x_prompt, x_sample, norm_w, w_in, lam_re, lam_im, log_step, b_re, b_im, c_re, c_im, d_skip,
           w_glu, b_glu, ssm_norm_w, sink, attn_norm_w, w_out, rel_bias, final_norm_w):
    assert norm_w.shape[0] == 1, "single-layer encoder"
    ops = _ssm_operators(lam_re[0], lam_im[0], log_step[0], b_re[0], b_im[0], c_re[0], c_im[0], d_skip[0])
    bias = _attn_tables(rel_bias)
    w_in_b = w_in[0].astype(BF16)
    w_glu_b = w_glu[0].astype(BF16)
    w_out_b = w_out[0].astype(BF16)
    sink_f = sink[0].astype(F32) * LOG2_E

    outs = []
    for x in (x_prompt, x_sample):
        batch, seq_len, _ = x.shape
        assert batch % SEQ_TILE == 0 and seq_len % TOK_TILE == 0
        n_tok = batch * seq_len
        up, zs, q, k, v, za = _proj_call(x, norm_w, w_in_b)
        yp = _ssm_call(up, ops)
        ao = _attn_call(q.reshape(n_tok, D_ATTN), k.reshape(n_tok, KV_W), v.reshape(n_tok, KV_W),
                        bias, sink_f, batch, seq_len)
        outs.append(_post_call(x, yp, zs, ao.reshape(batch, seq_len, D_ATTN), za, w_glu_b, b_glu, ssm_norm_w,
                               attn_norm_w, w_out_b, final_norm_w.reshape(1, D_MODEL)))
    return tuple(outs)
```

```python
import functools
import math

import numpy as np
import jax
import jax.numpy as jnp
from jax import lax
from jax.experimental import pallas as pl
from jax.experimental.pallas import tpu as pltpu

F32 = jnp.float32
BF16 = jnp.bfloat16

D_MODEL = 1024
D_SSM = 512
D_ATTN = 512
SSM_GROUP = 16
N_GROUPS = D_SSM // SSM_GROUP
SSM_STATE = 64
HEAD_DIM = 64
N_HEADS = D_ATTN // HEAD_DIM
N_KV_HEADS = 2
WINDOW = 128
BLOCK = 128
N_BUCKETS = 32
MAX_DISTANCE = 128
RMS_EPS = 1e-6
NEG_INF = -1e30
LOG2_E = math.log2(math.e)

LANES = 128
SUBLANES = 8
CHUNK = 16
CHUNK_W = CHUNK * SSM_GROUP
STATE_W = 4 * SSM_STATE
SEQ_TILE = SUBLANES
TILE_CHUNKS = LANES // SEQ_TILE
TOK_TILE = TILE_CHUNKS * CHUNK
TILE_ROWS = SEQ_TILE * TOK_TILE
SUB_SEQS = 4
SUB = SUB_SEQS * TOK_TILE
N_SUB = SEQ_TILE // SUB_SEQS
LANE_TILES = D_SSM // LANES
GROUPS_PER_LANE_TILE = LANES // SSM_GROUP
KEYS = 3 * BLOCK
Q_ROWS = 1024
BLOCKS_PER_ITER = 8
KV_W = N_KV_HEADS * HEAD_DIM
MM_ROWS = 512
VMEM_LIMIT = 56 * 1024 * 1024

_C_U, _C_ZS, _C_Q, _C_K, _C_V, _C_ZA, _C_END = 0, 512, 1024, 1536, 1664, 1792, 2304


def _rms(x, w):
    return (x * lax.rsqrt(jnp.mean(x * x, axis=-1, keepdims=True) + RMS_EPS)) * w


def _tile_specs(width):
    return pl.BlockSpec((SUB_SEQS, TOK_TILE, width), lambda sb, cb, s: (sb * N_SUB + s, cb, 0))


_CHUNK_TILE_SPEC = pl.BlockSpec((N_GROUPS, None, None, LANES, CHUNK_W), lambda sb, cb, s: (0, sb, cb, 0, 0))


def _const_spec(shape):
    return pl.BlockSpec(shape, lambda sb, cb, s: (0,) * len(shape))


PAIR_ROWS = 2 * SEQ_TILE * CHUNK


def _pair_perm(t_major_out):
    out_row = lax.broadcasted_iota(jnp.int32, (PAIR_ROWS, PAIR_ROWS), 0)
    in_row = lax.broadcasted_iota(jnp.int32, (PAIR_ROWS, PAIR_ROWS), 1)
    tcs, cst = (out_row, in_row) if t_major_out else (in_row, out_row)
    hit = (tcs // (2 * SEQ_TILE) == cst % CHUNK) & (tcs % (2 * SEQ_TILE) == cst // CHUNK)
    return jnp.where(hit, 1.0, 0.0).astype(BF16)


def _proj_kernel(x_ref, nw_ref, w_ref, up_ref, zs_ref, q_ref, k_ref, v_ref, za_ref, u_scr, b_scr, t_scr):
    s = pl.program_id(2)
    hb = _rms(x_ref[...].reshape(SUB, D_MODEL), nw_ref[...]).astype(BF16)

    def mm(lo, hi):
        return jnp.dot(hb, w_ref[:, lo:hi], preferred_element_type=F32)

    def put(ref, val):
        ref[...] = val.astype(BF16).reshape(ref.shape)

    uz = mm(_C_U, _C_Q)
    u_scr[pl.ds(pl.multiple_of(s * SUB, SUB), SUB), :] = uz[:, :D_SSM].astype(BF16)
    put(zs_ref, uz[:, D_SSM:])
    qkv = mm(_C_Q, _C_ZA)
    put(q_ref, qkv[:, :D_ATTN] * (HEAD_DIM ** -0.5 * LOG2_E))
    put(k_ref, qkv[:, D_ATTN:D_ATTN + KV_W])
    put(v_ref, qkv[:, D_ATTN + KV_W:])
    put(za_ref, mm(_C_ZA, _C_END))

    @pl.when(s == N_SUB - 1)
    def _():
        perm = _pair_perm(t_major_out=True)
        for cp in range(TILE_CHUNKS // 2):
            a = jnp.concatenate(
                [u_scr[pl.ds(sq * TOK_TILE + (2 * cp + c2) * CHUNK, CHUNK), :]
                 for c2 in range(2) for sq in range(SEQ_TILE)], axis=0)
            b_scr[cp] = jnp.dot(perm, a, preferred_element_type=F32).astype(BF16)
        for t in range(CHUNK):
            for j in range(LANE_TILES):
                blk = jnp.concatenate(
                    [b_scr[cp, pl.ds(t * 2 * SEQ_TILE, 2 * SEQ_TILE), pl.ds(j * LANES, LANES)]
                     for cp in range(TILE_CHUNKS // 2)], axis=0)
                t_scr[t, j] = blk.T
        for g in range(N_GROUPS):
            j, g3 = divmod(g, GROUPS_PER_LANE_TILE)
            for th in range(CHUNK_W // LANES):
                rows = jnp.concatenate(
                    [t_scr[th * 8 + t3, j, pl.ds(g3 * SSM_GROUP, SSM_GROUP), :] for t3 in range(8)], axis=0)
                up_ref[g, :, pl.ds(th * LANES, LANES)] = rows.T


def _proj_call(x, norm_w, w_in):
    batch, seq_len, _ = x.shape
    nsb, ncb = batch // SEQ_TILE, seq_len // TOK_TILE
    tok = lambda width: jax.ShapeDtypeStruct((batch, seq_len, width), BF16)
    out_shapes = (
        jax.ShapeDtypeStruct((N_GROUPS, nsb, ncb, LANES, CHUNK_W), BF16),
        tok(D_SSM),
        tok(D_ATTN),
        tok(KV_W),
        tok(KV_W),
        tok(D_ATTN),
    )
    return pl.pallas_call(
        _proj_kernel,
        grid=(nsb, ncb, N_SUB),
        in_specs=[_tile_specs(D_MODEL), _const_spec((1, D_MODEL)), _const_spec((D_MODEL, _C_END))],
        out_specs=(_CHUNK_TILE_SPEC, _tile_specs(D_SSM), _tile_specs(D_ATTN), _tile_specs(KV_W),
                   _tile_specs(KV_W), _tile_specs(D_ATTN)),
        out_shape=out_shapes,
        scratch_shapes=[
            pltpu.VMEM((TILE_ROWS, D_SSM), BF16),
            pltpu.VMEM((TILE_CHUNKS // 2, PAIR_ROWS, D_SSM), BF16),
            pltpu.VMEM((CHUNK, LANE_TILES, LANES, LANES), BF16),
        ],
        compiler_params=pltpu.CompilerParams(
            dimension_semantics=("arbitrary", "arbitrary", "arbitrary"), vmem_limit_bytes=VMEM_LIMIT),
        name="proj",
    )(x, norm_w, w_in)


def _ssm_kernel(u_ref, toep_ref, bs_ref, cs_ref, are_ref, aim_ref, y_ref, s_scr, xf_scr, xb_scr,
                *, nsb, ncb):
    nch = ncb * TILE_CHUNKS
    tiles_per_mm = MM_ROWS // LANES
    seq_rows = nsb * SEQ_TILE

    def mm_blocks():
        for sb in range(nsb):
            for cq in range(ncb // tiles_per_mm):
                yield sb, cq * tiles_per_mm, pl.ds((sb * ncb + cq * tiles_per_mm) * LANES, MM_ROWS)

    def load_u(sb, cb0):
        return jnp.concatenate([u_ref[sb, cb0 + i] for i in range(tiles_per_mm)], axis=0)

    for sb, cb0, rows in mm_blocks():
        st = jnp.dot(load_u(sb, cb0), bs_ref[...], preferred_element_type=F32)
        s_scr[0, rows, :] = st[:, :LANES]
        s_scr[1, rows, :] = st[:, LANES:]

    def chunk_rows(sb, ch):
        return pl.ds(pl.multiple_of(sb * nch * SEQ_TILE + ch * SEQ_TILE, SEQ_TILE), SEQ_TILE)

    def load(ref, part, ch):
        return jnp.concatenate([ref[part, chunk_rows(sb, ch), :] for sb in range(nsb)], axis=0)

    def store(ref, part, ch, val):
        for sb in range(nsb):
            ref[part, chunk_rows(sb, ch), :] = val[sb * SEQ_TILE:(sb + 1) * SEQ_TILE, :]

    are = jnp.broadcast_to(are_ref[...], (seq_rows, LANES))
    aim = jnp.broadcast_to(aim_ref[...], (seq_rows, LANES))
    is_fwd = lax.broadcasted_iota(jnp.int32, (seq_rows, LANES), 1) < SSM_STATE

    def step(i, carry):
        xr, xi = carry
        j = nch - 1 - i
        store(xf_scr, 0, i, xr)
        store(xf_scr, 1, i, xi)
        store(xb_scr, 0, j, xr)
        store(xb_scr, 1, j, xi)
        s_re = jnp.where(is_fwd, load(s_scr, 0, i), load(s_scr, 0, j))
        s_im = jnp.where(is_fwd, load(s_scr, 1, i), load(s_scr, 1, j))
        return (are * xr - aim * xi + s_re, are * xi + aim * xr + s_im)

    zero = jnp.zeros((seq_rows, LANES), F32)
    lax.fori_loop(0, nch, step, (zero, zero), unroll=4)

    fwd_lane = lax.broadcasted_iota(jnp.int32, (MM_ROWS, LANES), 1) < SSM_STATE
    for sb, cb0, rows in mm_blocks():
        xin = jnp.concatenate(
            [jnp.where(fwd_lane, xf_scr[c, rows, :], xb_scr[c, rows, :]) for c in range(2)], axis=1).astype(BF16)
        y = (jnp.dot(load_u(sb, cb0), toep_ref[...], preferred_element_type=F32)
             + jnp.dot(xin, cs_ref[...], preferred_element_type=F32))
        for i in range(tiles_per_mm):
            y_ref[sb, cb0 + i] = y[i * LANES:(i + 1) * LANES, :].astype(BF16)


def _ssm_call(up, ops):
    toep, bs, cs, are, aim = ops
    _, nsb, ncb, _, _ = up.shape
    n_rows = nsb * ncb * LANES
    assert ncb % (MM_ROWS // LANES) == 0
    data_spec = pl.BlockSpec((None, nsb, ncb, LANES, CHUNK_W), lambda g: (g, 0, 0, 0, 0))
    grp = lambda *shape: pl.BlockSpec((None,) + shape, lambda g: (g, 0, 0))
    return pl.pallas_call(
        functools.partial(_ssm_kernel, nsb=nsb, ncb=ncb),
        grid=(N_GROUPS,),
        in_specs=[data_spec, grp(CHUNK_W, CHUNK_W), grp(CHUNK_W, STATE_W), grp(STATE_W, CHUNK_W),
                  grp(1, LANES), grp(1, LANES)],
        out_specs=data_spec,
        out_shape=jax.ShapeDtypeStruct(up.shape, BF16),
        scratch_shapes=[pltpu.VMEM((2, n_rows, LANES), F32)] * 3,
        compiler_params=pltpu.CompilerParams(
            dimension_semantics=("arbitrary",), vmem_limit_bytes=VMEM_LIMIT),
        name="ssm",
    )(up, toep, bs, cs, are, aim)


def _attn_kernel(sink_ref, q_ref, k_ref, v_ref, bias_ref, o_ref, ka_scr, kb_scr, wa_scr, wb_scr, *, seq_len):
    jb = pl.program_id(1)
    nb = seq_len // BLOCK

    @pl.when((pl.program_id(0) == 0) & (jb == 0))
    def _():
        zero_rows = jnp.zeros((HEAD_DIM, BLOCK), BF16)
        one_rows = jnp.ones((HEAD_DIM, BLOCK), BF16)

        def per_block(r, carry):
            for kh in range(N_KV_HEADS):
                wa_scr[kh, r, pl.ds(HEAD_DIM, 3 * HEAD_DIM), :] = jnp.concatenate(
                    [zero_rows, one_rows, zero_rows], axis=0)
                wb_scr[kh, r, pl.ds(0, HEAD_DIM), :] = zero_rows
                wb_scr[kh, r, pl.ds(2 * HEAD_DIM, 2 * HEAD_DIM), :] = jnp.concatenate([zero_rows, one_rows], axis=0)
            return carry

        lax.fori_loop(0, nb, per_block, 0)

    @pl.when(jb == 0)
    def _():
        low = lax.broadcasted_iota(jnp.int32, (BLOCK, LANES), 1) < HEAD_DIM

        def per_block(r, carry):
            rows = pl.ds(pl.multiple_of(r * BLOCK, BLOCK), BLOCK)
            k = k_ref[rows, :]
            k_swapped = pltpu.roll(k, HEAD_DIM, 1)
            zero = jnp.zeros_like(k)
            ka_scr[0, rows, :] = jnp.where(low, k, zero)
            kb_scr[0, rows, :] = jnp.where(low, zero, k_swapped)
            ka_scr[1, rows, :] = jnp.where(low, k_swapped, zero)
            kb_scr[1, rows, :] = jnp.where(low, zero, k)
            vt = v_ref[rows, :].T
            for kh in range(N_KV_HEADS):
                vt_kh = vt[kh * HEAD_DIM:(kh + 1) * HEAD_DIM]
                wa_scr[kh, r, pl.ds(0, HEAD_DIM), :] = vt_kh
                wb_scr[kh, r, pl.ds(HEAD_DIM, HEAD_DIM), :] = vt_kh
            return carry

        lax.fori_loop(0, nb, per_block, 0)

    col_tile0 = lax.broadcasted_iota(jnp.int32, (1, 2 * LANES), 1) < LANES
    def query_block(sb):
        n = jb * (Q_ROWS // BLOCK) + sb
        kb0 = jnp.clip(n - 1, 0, nb - KEYS // BLOCK)
        ks = pl.multiple_of(kb0 * BLOCK, BLOCK)
        var = jnp.where(n == 0, 0, jnp.where(n == nb - 1, 2, 1))
        qrows = pl.ds(pl.multiple_of(sb * BLOCK, BLOCK), BLOCK)
        for kh in range(N_KV_HEADS):
            kcat = jnp.concatenate([ka_scr[kh, pl.ds(ks, KEYS), :], kb_scr[kh, pl.ds(ks, KEYS), :]], axis=0)
            q2 = jnp.concatenate([q_ref[qrows, pl.ds((2 * kh + jj) * LANES, LANES)] for jj in range(2)], axis=0)
            t = lax.dot_general(kcat, q2, (((1,), (1,)), ((), ())), preferred_element_type=F32)
            t = t + bias_ref[var, kh]
            ps, sink_terms = [], []
            for hh in range(2):
                th = t[hh * KEYS:(hh + 1) * KEYS]
                sink = jnp.where(col_tile0, sink_ref[4 * kh + hh], sink_ref[4 * kh + 2 + hh])
                m = jnp.maximum(jnp.max(th, axis=0, keepdims=True), sink)
                ps.append(jnp.exp2(th - m).astype(BF16))
                sink_terms.append(jnp.exp2(sink - m))
            w = jnp.concatenate([wa_scr[kh, kb0 + i] for i in range(KEYS // BLOCK)]
                                + [wb_scr[kh, kb0 + i] for i in range(KEYS // BLOCK)], axis=1)
            o2 = jnp.dot(w, jnp.concatenate(ps, axis=0), preferred_element_type=F32)
            invs = [1.0 / (o2[LANES + hh * HEAD_DIM:LANES + hh * HEAD_DIM + SUBLANES] + sink_terms[hh])
                    for hh in range(2)]
            inv = jnp.concatenate([invs[hh] for hh in range(2) for _ in range(HEAD_DIM // SUBLANES)], axis=0)
            on = o2[:LANES] * inv
            for jj in range(2):
                o_ref[qrows, pl.ds((2 * kh + jj) * LANES, LANES)] = (
                    on[:, jj * LANES:(jj + 1) * LANES].T.astype(BF16))

    def query_blocks(i, carry):
        for sb in range(BLOCKS_PER_ITER):
            query_block(i * BLOCKS_PER_ITER + sb)
        return carry

    lax.fori_loop(0, Q_ROWS // (BLOCKS_PER_ITER * BLOCK), query_blocks, 0)


def _attn_call(q, k, v, bias, sink, batch, seq_len):
    assert seq_len % Q_ROWS == 0 and seq_len >= KEYS
    nq = seq_len // Q_ROWS
    nb = seq_len // BLOCK
    return pl.pallas_call(
        functools.partial(_attn_kernel, seq_len=seq_len),
        grid_spec=pltpu.PrefetchScalarGridSpec(
            num_scalar_prefetch=1,
            grid=(batch, nq),
            in_specs=[
                pl.BlockSpec((Q_ROWS, D_ATTN), lambda b, j, s: (b * nq + j, 0)),
                pl.BlockSpec((seq_len, KV_W), lambda b, j, s: (b, 0)),
                pl.BlockSpec((seq_len, KV_W), lambda b, j, s: (b, 0)),
                pl.BlockSpec((3, N_KV_HEADS, 2 * KEYS, 2 * LANES), lambda b, j, s: (0, 0, 0, 0)),
            ],
            out_specs=pl.BlockSpec((Q_ROWS, D_ATTN), lambda b, j, s: (b * nq + j, 0)),
            scratch_shapes=[
                pltpu.VMEM((N_KV_HEADS, seq_len, LANES), BF16),
                pltpu.VMEM((N_KV_HEADS, seq_len, LANES), BF16),
                pltpu.VMEM((N_KV_HEADS, nb, 2 * LANES, BLOCK), BF16),
                pltpu.VMEM((N_KV_HEADS, nb, 2 * LANES, BLOCK), BF16),
            ],
        ),
        out_shape=jax.ShapeDtypeStruct((batch * seq_len, D_ATTN), BF16),
        compiler_params=pltpu.CompilerParams(
            dimension_semantics=("arbitrary", "arbitrary"), vmem_limit_bytes=VMEM_LIMIT),
        name="attn",
    )(sink, q, k, v, bias)


def _gelu_tanh(x):
    c = math.sqrt(2.0 / math.pi)
    half = 0.5 * x
    return half + half * jnp.tanh(x * (c + (c * 0.044715) * (x * x)))


def _sigmoid(x):
    return 0.5 * jnp.tanh(0.5 * x) + 0.5


def _silu(x):
    return x * _sigmoid(x)


def _post_kernel(x_ref, yp_ref, zs_ref, ao_ref, za_ref, wglu_ref, bglu_ref, snw_ref, anw_ref,
                 wout_ref, fnw_ref, o_ref, y_scr, b_scr, t_scr):
    s = pl.program_id(2)

    @pl.when(s == 0)
    def _():
        for g in range(N_GROUPS):
            j, g3 = divmod(g, GROUPS_PER_LANE_TILE)
            for th in range(CHUNK_W // LANES):
                tt = yp_ref[g, :, pl.ds(th * LANES, LANES)].T
                for t3 in range(8):
                    t_scr[th * 8 + t3, j, pl.ds(g3 * SSM_GROUP, SSM_GROUP), :] = (
                        tt[t3 * SSM_GROUP:(t3 + 1) * SSM_GROUP, :])
        for t in range(CHUNK):
            for j in range(LANE_TILES):
                b_scr[t, :, pl.ds(j * LANES, LANES)] = t_scr[t, j].T
        perm = _pair_perm(t_major_out=False)
        for cp in range(TILE_CHUNKS // 2):
            a = jnp.concatenate(
                [b_scr[t, pl.ds(cp * 2 * SEQ_TILE, 2 * SEQ_TILE), :] for t in range(CHUNK)], axis=0)
            ys = jnp.dot(perm, a, preferred_element_type=F32)
            for c2 in range(2):
                for sq in range(SEQ_TILE):
                    r0 = (c2 * SEQ_TILE + sq) * CHUNK
                    y_scr[pl.ds(sq * TOK_TILE + (2 * cp + c2) * CHUNK, CHUNK), :] = ys[r0:r0 + CHUNK, :]

    def get(ref):
        return ref[...].reshape(SUB, ref.shape[-1]).astype(F32)

    y = y_scr[pl.ds(pl.multiple_of(s * SUB, SUB), SUB), :]
    g = _gelu_tanh(y)
    gate = _sigmoid(jnp.dot(g.astype(BF16), wglu_ref[...], preferred_element_type=F32) + bglu_ref[...])
    def gate_of(ref):
        return _silu(ref[...].reshape(SUB, ref.shape[-1])).astype(F32)

    n_ssm = _rms(g * gate, snw_ref[...]) * gate_of(zs_ref)
    n_attn = _rms(get(ao_ref), anw_ref[...]) * gate_of(za_ref)
    mixed = jnp.concatenate([n_ssm, n_attn], axis=1).astype(BF16)
    res = get(x_ref) + jnp.dot(mixed, wout_ref[...], preferred_element_type=F32)
    o_ref[...] = _rms(res, fnw_ref[...]).reshape(o_ref.shape)


def _post_call(x, yp, zs, ao, za, w_glu, b_glu, ssm_norm_w, attn_norm_w, w_out, final_norm_w):
    batch, seq_len, _ = x.shape
    nsb, ncb = batch // SEQ_TILE, seq_len // TOK_TILE
    return pl.pallas_call(
        _post_kernel,
        grid=(nsb, ncb, N_SUB),
        in_specs=[
            _tile_specs(D_MODEL), _CHUNK_TILE_SPEC, _tile_specs(D_SSM), _tile_specs(D_ATTN), _tile_specs(D_ATTN),
            _const_spec((D_SSM, D_SSM)), _const_spec((1, D_SSM)), _const_spec((1, D_SSM)),
            _const_spec((1, D_ATTN)), _const_spec((D_MODEL, D_MODEL)), _const_spec((1, D_MODEL)),
        ],
        out_specs=_tile_specs(D_MODEL),
        out_shape=jax.ShapeDtypeStruct((batch, seq_len, D_MODEL), F32),
        scratch_shapes=[
            pltpu.VMEM((TILE_ROWS, D_SSM), F32),
            pltpu.VMEM((CHUNK, LANES, D_SSM), BF16),
            pltpu.VMEM((CHUNK, LANE_TILES, LANES, LANES), BF16),
        ],
        compiler_params=pltpu.CompilerParams(
            dimension_semantics=("arbitrary", "arbitrary", "arbitrary"), vmem_limit_bytes=VMEM_LIMIT),
        name="post",
    )(x, yp, zs, ao, za, w_glu, b_glu, ssm_norm_w, attn_norm_w, w_out, final_norm_w)


def _cmul(ar, ai, br, bi):
    return ar * br - ai * bi, ar * bi + ai * br


def _ops_kernel(row_ref, col_ref, bt_ref, ct_ref, dsk_ref, toep_ref, bs_ref, cs_ref, a_ref):
    def discretise(lr, li, ls):
        dt = jnp.exp(ls)
        mag = jnp.exp(lr * dt)
        return mag * jnp.cos(li * dt), mag * jnp.sin(li * dt)

    lr, li = row_ref[0], row_ref[1]
    lb_re, lb_im = discretise(lr, li, row_ref[2])
    den = lr * lr + li * li
    nr = lb_re - 1.0
    coef_re = (nr * lr + lb_im * li) / den
    coef_im = (lb_im * lr - nr * li) / den
    bb_re, bb_im = _cmul(coef_re, coef_im, bt_ref[0], bt_ref[1])
    bb_re = jnp.concatenate([bb_re] * CHUNK, axis=0)
    bb_im = jnp.concatenate([bb_im] * CHUNK, axis=0)

    row_t = lax.broadcasted_iota(jnp.int32, (CHUNK_W, LANES), 0) // SSM_GROUP
    fwd_lane = lax.broadcasted_iota(jnp.int32, (CHUNK_W, LANES), 1) < SSM_STATE
    col_t = lax.broadcasted_iota(jnp.int32, (LANES, CHUNK_W), 1) // SSM_GROUP
    fwd_row = lax.broadcasted_iota(jnp.int32, (LANES, CHUNK_W), 0) < SSM_STATE
    ct_re, ct_im = ct_ref[0], ct_ref[1]
    w_re = jnp.zeros((CHUNK_W, LANES), F32)
    w_im = jnp.zeros((CHUNK_W, LANES), F32)
    toep = jnp.zeros((CHUNK_W, CHUNK_W), F32)
    for i in range(CHUNK):
        t_now = jnp.where(fwd_lane, i, CHUNK - 1 - i)
        w_re, w_im = _cmul(w_re, w_im, lb_re, lb_im)
        w_re = w_re + jnp.where(row_t == t_now, bb_re, 0.0)
        w_im = w_im + jnp.where(row_t == t_now, bb_im, 0.0)
        out_now = col_t == jnp.where(fwd_row, i, CHUNK - 1 - i)
        rhs = jnp.concatenate([jnp.where(out_now, ct_re, 0.0), jnp.where(out_now, -ct_im, 0.0)], axis=0)
        lhs = jnp.concatenate([w_re, w_im], axis=1)
        toep = toep + jnp.dot(lhs.astype(BF16), rhs.astype(BF16), preferred_element_type=F32)
    r = lax.broadcasted_iota(jnp.int32, (CHUNK_W, CHUNK_W), 0)
    c = lax.broadcasted_iota(jnp.int32, (CHUNK_W, CHUNK_W), 1)
    toep_ref[...] = (toep + jnp.where(r == c, dsk_ref[...], 0.0)).astype(BF16)
    bs_ref[...] = jnp.concatenate([w_re, w_im], axis=1).astype(BF16)

    pr, pi = discretise(col_ref[0], col_ref[1], col_ref[2])
    k = jnp.where(fwd_row, col_t + 1, CHUNK - col_t)
    acc_re = jnp.ones((LANES, CHUNK_W), F32)
    acc_im = jnp.zeros((LANES, CHUNK_W), F32)
    for bit in range(CHUNK.bit_length()):
        nre, nim = _cmul(acc_re, acc_im, pr, pi)
        take = (k & (1 << bit)) != 0
        acc_re = jnp.where(take, nre, acc_re)
        acc_im = jnp.where(take, nim, acc_im)
        pr, pi = _cmul(pr, pi, pr, pi)
    cp_re, cp_im = _cmul(ct_re, ct_im, acc_re, acc_im)
    cs_ref[...] = jnp.concatenate([cp_re, -cp_im], axis=0).astype(BF16)

    a_re, a_im = lb_re, lb_im
    for _ in range(CHUNK.bit_length() - 1):
        a_re, a_im = _cmul(a_re, a_im, a_re, a_im)
    a_ref[0] = a_re
    a_ref[1] = a_im


def _ssm_operators(lam_re, lam_im, log_step, b_re, b_im, c_re, c_im, d_skip):
    g_first = lambda a: jnp.transpose(a.astype(F32), (1, 0, 2)).reshape(N_GROUPS, 2 * SSM_STATE)
    params = jnp.stack([g_first(lam_re), g_first(lam_im),
                        jnp.repeat(log_step.astype(F32).T, SSM_STATE, axis=1)], axis=1)
    b_t = lambda b: jnp.transpose(b.astype(F32), (1, 3, 0, 2)).reshape(N_GROUPS, SSM_GROUP, 2 * SSM_STATE)
    c_t = lambda c: jnp.tile(
        jnp.transpose(c.astype(F32), (1, 0, 3, 2)).reshape(N_GROUPS, 2 * SSM_STATE, SSM_GROUP), (1, 1, CHUNK))
    bt = jnp.stack([b_t(b_re), b_t(b_im)], axis=1)
    ct = jnp.stack([c_t(c_re), c_t(c_im)], axis=1)
    dsk = jnp.tile(d_skip.astype(F32), (1, CHUNK))[:, None, :]
    grp = lambda *shape: pl.BlockSpec((None,) + shape, lambda g: (g,) + (0,) * len(shape))
    mat = jax.ShapeDtypeStruct((N_GROUPS, CHUNK_W, CHUNK_W), BF16)
    toep, bs, cs, a = pl.pallas_call(
        _ops_kernel,
        grid=(N_GROUPS,),
        in_specs=[grp(3, 1, LANES), grp(3, LANES, 1), grp(2, SSM_GROUP, LANES), grp(2, LANES, CHUNK_W),
                  grp(1, CHUNK_W)],
        out_specs=(grp(CHUNK_W, CHUNK_W), grp(CHUNK_W, STATE_W), grp(STATE_W, CHUNK_W), grp(2, 1, LANES)),
        out_shape=(mat, mat, mat, jax.ShapeDtypeStruct((N_GROUPS, 2, 1, LANES), F32)),
        compiler_params=pltpu.CompilerParams(dimension_semantics=("arbitrary",)),
        name="s5_ops",
    )(params[:, :, None, :], params[:, :, :, None], bt, ct, dsk)
    return toep, bs, cs, a[:, 0], a[:, 1]


def _t5_buckets_np(rel):
    half = N_BUCKETS // 2
    max_exact = half // 2
    ret = np.where(rel > 0, half, 0)
    n = np.abs(rel)
    nf = np.maximum(n, 1).astype(np.float64)
    large = max_exact + (np.log(nf / max_exact) / math.log(MAX_DISTANCE / max_exact)
                         * (half - max_exact)).astype(np.int32)
    large = np.minimum(large, half - 1)
    return ret + np.where(n < max_exact, n, large)


def _attn_tables(rel_bias):
    offsets = (0, -BLOCK, -2 * BLOCK)
    dist = np.arange(BIAS_PERIOD)
    dist = np.where(dist < KEYS, dist, dist - BIAS_PERIOD)
    rel = np.stack([dist + off for off in offsets])
    per_dist = rel_bias.astype(F32)[_t5_buckets_np(rel)] * LOG2_E
    per_dist = jnp.where((np.abs(rel) <= WINDOW)[:, :, None], per_dist, NEG_INF)
    per_dist = jnp.transpose(per_dist, (0, 2, 1))
    neg = jnp.concatenate([per_dist[:, :, :1], per_dist[:, :, :0:-1]], axis=-1)
    return pl.pallas_call(
        _bias_kernel,
        grid=(len(offsets),),
        in_specs=[pl.BlockSpec((None, N_HEADS, 1, BIAS_PERIOD), lambda v: (v, 0, 0, 0))],
        out_specs=pl.BlockSpec((None, N_KV_HEADS, 2 * KEYS, 2 * LANES), lambda v: (v, 0, 0, 0)),
        out_shape=jax.ShapeDtypeStruct((len(offsets), N_KV_HEADS, 2 * KEYS, 2 * LANES), F32),
        compiler_params=pltpu.CompilerParams(dimension_semantics=("arbitrary",)),
        name="bias",
    )(neg[:, :, None, :])


BIAS_PERIOD = 4 * BLOCK


def _bias_kernel(neg_ref, o_ref):
    for kh in range(N_KV_HEADS):
        for tile in range(2):
            for slot in range(2):
                rows = jnp.broadcast_to(neg_ref[4 * kh + 2 * tile + slot], (KEYS, BIAS_PERIOD))
                skewed = pltpu.roll(rows, 0, 1, stride=1, stride_axis=0)
                o_ref[kh, pl.ds(slot * KEYS, KEYS), pl.ds(tile * LANES, LANES)] = skewed[:, :BLOCK]


def kernel(x_prompt, x_sample, norm_w, w_in, lam_re, lam_im, log_step, b_re, b_im, c_re, c_im, d_skip,
           w_glu, b_glu, ssm_norm_w, sink, attn_norm_w, w_out, rel_bias, final_norm_w):
    assert norm_w.shape[0] == 1, "single-layer encoder"
    ops = _ssm_operators(lam_re[0], lam_im[0], log_step[0], b_re[0], b_im[0], c_re[0], c_im[0], d_skip[0])
    bias = _attn_tables(rel_bias)
    w_in_b = w_in[0].astype(BF16)
    w_glu_b = w_glu[0].astype(BF16)
    w_out_b = w_out[0].astype(BF16)
    sink_f = sink[0].astype(F32) * LOG2_E

    outs = []
    for x in (x_prompt, x_sample):
        batch, seq_len, _ = x.shape
        assert batch % SEQ_TILE == 0 and seq_len % TOK_TILE == 0
        n_tok = batch * seq_len
        up, zs, q, k, v, za = _proj_call(x, norm_w, w_in_b)
        yp = _ssm_call(up, ops)
        ao = _attn_call(q.reshape(n_tok, D_ATTN), k.reshape(n_tok, KV_W), v.reshape(n_tok, KV_W),
                        bias, sink_f, batch, seq_len)
        outs.append(_post_call(x, yp, zs, ao.reshape(batch, seq_len, D_ATTN), za, w_glu_b, b_glu, ssm_norm_w,
                               attn_norm_w, w_out_b, final_norm_w.reshape(1, D_MODEL)))
    return tuple(outs)
```

```python
import functools
import math

import numpy as np
import jax
import jax.numpy as jnp
from jax import lax
from jax.experimental import pallas as pl
from jax.experimental.pallas import tpu as pltpu

F32 = jnp.float32
BF16 = jnp.bfloat16

D_MODEL = 1024
D_SSM = 512
D_ATTN = 512
SSM_GROUP = 16
N_GROUPS = D_SSM // SSM_GROUP
SSM_STATE = 64
HEAD_DIM = 64
N_HEADS = D_ATTN // HEAD_DIM
N_KV_HEADS = 2
WINDOW = 128
BLOCK = 128
N_BUCKETS = 32
MAX_DISTANCE = 128
RMS_EPS = 1e-6
NEG_INF = -1e30
LOG2_E = math.log2(math.e)

LANES = 128
SUBLANES = 8
CHUNK = 16
CHUNK_W = CHUNK * SSM_GROUP
STATE_W = 4 * SSM_STATE
SEQ_TILE = SUBLANES
TILE_CHUNKS = LANES // SEQ_TILE
TOK_TILE = TILE_CHUNKS * CHUNK
TILE_ROWS = SEQ_TILE * TOK_TILE
SUB_SEQS = 4
SUB = SUB_SEQS * TOK_TILE
N_SUB = SEQ_TILE // SUB_SEQS
LANE_TILES = D_SSM // LANES
GROUPS_PER_LANE_TILE = LANES // SSM_GROUP
KEYS = 3 * BLOCK
Q_ROWS = 2048
BLOCKS_PER_ITER = 16
KV_W = N_KV_HEADS * HEAD_DIM
MM_ROWS = 512
VMEM_LIMIT = 56 * 1024 * 1024

_C_U, _C_ZS, _C_Q, _C_K, _C_V, _C_ZA, _C_END = 0, 512, 1024, 1536, 1664, 1792, 2304


def _rms(x, w):
    return (x * lax.rsqrt(jnp.mean(x * x, axis=-1, keepdims=True) + RMS_EPS)) * w


def _tile_specs(width):
    return pl.BlockSpec((SUB_SEQS, TOK_TILE, width), lambda sb, cb, s: (sb * N_SUB + s, cb, 0))


_CHUNK_TILE_SPEC = pl.BlockSpec((N_GROUPS, None, None, LANES, CHUNK_W), lambda sb, cb, s: (0, sb, cb, 0, 0))


def _const_spec(shape):
    return pl.BlockSpec(shape, lambda sb, cb, s: (0,) * len(shape))


PAIR_ROWS = 2 * SEQ_TILE * CHUNK


def _pair_perm(t_major_out):
    out_row = lax.broadcasted_iota(jnp.int32, (PAIR_ROWS, PAIR_ROWS), 0)
    in_row = lax.broadcasted_iota(jnp.int32, (PAIR_ROWS, PAIR_ROWS), 1)
    tcs, cst = (out_row, in_row) if t_major_out else (in_row, out_row)
    hit = (tcs // (2 * SEQ_TILE) == cst % CHUNK) & (tcs % (2 * SEQ_TILE) == cst // CHUNK)
    return jnp.where(hit, 1.0, 0.0).astype(BF16)


def _proj_kernel(x_ref, nw_ref, w_ref, up_ref, zs_ref, q_ref, k_ref, v_ref, za_ref, u_scr, b_scr, t_scr):
    s = pl.program_id(2)
    hb = _rms(x_ref[...].reshape(SUB, D_MODEL), nw_ref[...]).astype(BF16)

    def mm(lo, hi):
        return jnp.dot(hb, w_ref[:, lo:hi], preferred_element_type=F32)

    def put(ref, val):
        ref[...] = val.astype(BF16).reshape(ref.shape)

    uz = mm(_C_U, _C_Q)
    u_scr[pl.ds(pl.multiple_of(s * SUB, SUB), SUB), :] = uz[:, :D_SSM].astype(BF16)
    put(zs_ref, uz[:, D_SSM:])
    qkv = mm(_C_Q, _C_ZA)
    put(q_ref, qkv[:, :D_ATTN] * (HEAD_DIM ** -0.5 * LOG2_E))
    put(k_ref, qkv[:, D_ATTN:D_ATTN + KV_W])
    put(v_ref, qkv[:, D_ATTN + KV_W:])
    put(za_ref, mm(_C_ZA, _C_END))

    @pl.when(s == N_SUB - 1)
    def _():
        perm = _pair_perm(t_major_out=True)
        for cp in range(TILE_CHUNKS // 2):
            a = jnp.concatenate(
                [u_scr[pl.ds(sq * TOK_TILE + (2 * cp + c2) * CHUNK, CHUNK), :]
                 for c2 in range(2) for sq in range(SEQ_TILE)], axis=0)
            b_scr[cp] = jnp.dot(perm, a, preferred_element_type=F32).astype(BF16)
        for t in range(CHUNK):
            for j in range(LANE_TILES):
                blk = jnp.concatenate(
                    [b_scr[cp, pl.ds(t * 2 * SEQ_TILE, 2 * SEQ_TILE), pl.ds(j * LANES, LANES)]
                     for cp in range(TILE_CHUNKS // 2)], axis=0)
                t_scr[t, j] = blk.T
        for g in range(N_GROUPS):
            j, g3 = divmod(g, GROUPS_PER_LANE_TILE)
            for th in range(CHUNK_W // LANES):
                rows = jnp.concatenate(
                    [t_scr[th * 8 + t3, j, pl.ds(g3 * SSM_GROUP, SSM_GROUP), :] for t3 in range(8)], axis=0)
                up_ref[g, :, pl.ds(th * LANES, LANES)] = rows.T


def _proj_call(x, norm_w, w_in):
    batch, seq_len, _ = x.shape
    nsb, ncb = batch // SEQ_TILE, seq_len // TOK_TILE
    tok = lambda width: jax.ShapeDtypeStruct((batch, seq_len, width), BF16)
    out_shapes = (
        jax.ShapeDtypeStruct((N_GROUPS, nsb, ncb, LANES, CHUNK_W), BF16),
        tok(D_SSM),
        tok(D_ATTN),
        tok(KV_W),
        tok(KV_W),
        tok(D_ATTN),
    )
    return pl.pallas_call(
        _proj_kernel,
        grid=(nsb, ncb, N_SUB),
        in_specs=[_tile_specs(D_MODEL), _const_spec((1, D_MODEL)), _const_spec((D_MODEL, _C_END))],
        out_specs=(_CHUNK_TILE_SPEC, _tile_specs(D_SSM), _tile_specs(D_ATTN), _tile_specs(KV_W),
                   _tile_specs(KV_W), _tile_specs(D_ATTN)),
        out_shape=out_shapes,
        scratch_shapes=[
            pltpu.VMEM((TILE_ROWS, D_SSM), BF16),
            pltpu.VMEM((TILE_CHUNKS // 2, PAIR_ROWS, D_SSM), BF16),
            pltpu.VMEM((CHUNK, LANE_TILES, LANES, LANES), BF16),
        ],
        compiler_params=pltpu.CompilerParams(
            dimension_semantics=("arbitrary", "arbitrary", "arbitrary"), vmem_limit_bytes=VMEM_LIMIT),
        name="proj",
    )(x, norm_w, w_in)


def _ssm_kernel(u_ref, toep_ref, bs_ref, cs_ref, are_ref, aim_ref, y_ref, s_scr, xf_scr, xb_scr,
                *, nsb, ncb):
    nch = ncb * TILE_CHUNKS
    tiles_per_mm = MM_ROWS // LANES
    seq_rows = nsb * SEQ_TILE

    def mm_blocks():
        for sb in range(nsb):
            for cq in range(ncb // tiles_per_mm):
                yield sb, cq * tiles_per_mm, pl.ds((sb * ncb + cq * tiles_per_mm) * LANES, MM_ROWS)

    def load_u(sb, cb0):
        return jnp.concatenate([u_ref[sb, cb0 + i] for i in range(tiles_per_mm)], axis=0)

    for sb, cb0, rows in mm_blocks():
        st = jnp.dot(load_u(sb, cb0), bs_ref[...], preferred_element_type=F32)
        s_scr[0, rows, :] = st[:, :LANES]
        s_scr[1, rows, :] = st[:, LANES:]

    def chunk_rows(sb, ch):
        return pl.ds(pl.multiple_of(sb * nch * SEQ_TILE + ch * SEQ_TILE, SEQ_TILE), SEQ_TILE)

    def load(ref, part, ch):
        return jnp.concatenate([ref[part, chunk_rows(sb, ch), :] for sb in range(nsb)], axis=0)

    def store(ref, part, ch, val):
        for sb in range(nsb):
            ref[part, chunk_rows(sb, ch), :] = val[sb * SEQ_TILE:(sb + 1) * SEQ_TILE, :]

    are = jnp.broadcast_to(are_ref[...], (seq_rows, LANES))
    aim = jnp.broadcast_to(aim_ref[...], (seq_rows, LANES))
    is_fwd = lax.broadcasted_iota(jnp.int32, (seq_rows, LANES), 1) < SSM_STATE

    def step(i, carry):
        xr, xi = carry
        j = nch - 1 - i
        store(xf_scr, 0, i, xr)
        store(xf_scr, 1, i, xi)
        store(xb_scr, 0, j, xr)
        store(xb_scr, 1, j, xi)
        s_re = jnp.where(is_fwd, load(s_scr, 0, i), load(s_scr, 0, j))
        s_im = jnp.where(is_fwd, load(s_scr, 1, i), load(s_scr, 1, j))
        return (are * xr - aim * xi + s_re, are * xi + aim * xr + s_im)

    zero = jnp.zeros((seq_rows, LANES), F32)
    lax.fori_loop(0, nch, step, (zero, zero), unroll=4)

    fwd_lane = lax.broadcasted_iota(jnp.int32, (MM_ROWS, LANES), 1) < SSM_STATE
    for sb, cb0, rows in mm_blocks():
        xin = jnp.concatenate(
            [jnp.where(fwd_lane, xf_scr[c, rows, :], xb_scr[c, rows, :]) for c in range(2)], axis=1).astype(BF16)
        y = (jnp.dot(load_u(sb, cb0), toep_ref[...], preferred_element_type=F32)
             + jnp.dot(xin, cs_ref[...], preferred_element_type=F32))
        for i in range(tiles_per_mm):
            y_ref[sb, cb0 + i] = y[i * LANES:(i + 1) * LANES, :].astype(BF16)


def _ssm_call(up, ops):
    toep, bs, cs, are, aim = ops
    _, nsb, ncb, _, _ = up.shape
    n_rows = nsb * ncb * LANES
    assert ncb % (MM_ROWS // LANES) == 0
    data_spec = pl.BlockSpec((None, nsb, ncb, LANES, CHUNK_W), lambda g: (g, 0, 0, 0, 0))
    grp = lambda *shape: pl.BlockSpec((None,) + shape, lambda g: (g, 0, 0))
    return pl.pallas_call(
        functools.partial(_ssm_kernel, nsb=nsb, ncb=ncb),
        grid=(N_GROUPS,),
        in_specs=[data_spec, grp(CHUNK_W, CHUNK_W), grp(CHUNK_W, STATE_W), grp(STATE_W, CHUNK_W),
                  grp(1, LANES), grp(1, LANES)],
        out_specs=data_spec,
        out_shape=jax.ShapeDtypeStruct(up.shape, BF16),
        scratch_shapes=[pltpu.VMEM((2, n_rows, LANES), F32)] * 3,
        compiler_params=pltpu.CompilerParams(
            dimension_semantics=("arbitrary",), vmem_limit_bytes=VMEM_LIMIT),
        name="ssm",
    )(up, toep, bs, cs, are, aim)


def _attn_kernel(sink_ref, q_ref, k_ref, v_ref, bias_ref, o_ref, ka_scr, kb_scr, wa_scr, wb_scr, *, seq_len):
    jb = pl.program_id(1)
    nb = seq_len // BLOCK

    @pl.when((pl.program_id(0) == 0) & (jb == 0))
    def _():
        zero_rows = jnp.zeros((HEAD_DIM, BLOCK), BF16)
        one_rows = jnp.ones((HEAD_DIM, BLOCK), BF16)

        def per_block(r, carry):
            for kh in range(N_KV_HEADS):
                wa_scr[kh, r, pl.ds(HEAD_DIM, 3 * HEAD_DIM), :] = jnp.concatenate(
                    [zero_rows, one_rows, zero_rows], axis=0)
                wb_scr[kh, r, pl.ds(0, HEAD_DIM), :] = zero_rows
                wb_scr[kh, r, pl.ds(2 * HEAD_DIM, 2 * HEAD_DIM), :] = jnp.concatenate([zero_rows, one_rows], axis=0)
            return carry

        lax.fori_loop(0, nb, per_block, 0)

    @pl.when(jb == 0)
    def _():
        low = lax.broadcasted_iota(jnp.int32, (BLOCK, LANES), 1) < HEAD_DIM

        def per_block(r, carry):
            rows = pl.ds(pl.multiple_of(r * BLOCK, BLOCK), BLOCK)
            k = k_ref[rows, :]
            k_swapped = pltpu.roll(k, HEAD_DIM, 1)
            zero = jnp.zeros_like(k)
            ka_scr[0, rows, :] = jnp.where(low, k, zero)
            kb_scr[0, rows, :] = jnp.where(low, zero, k_swapped)
            ka_scr[1, rows, :] = jnp.where(low, k_swapped, zero)
            kb_scr[1, rows, :] = jnp.where(low, zero, k)
            vt = v_ref[rows, :].T
            for kh in range(N_KV_HEADS):
                vt_kh = vt[kh * HEAD_DIM:(kh + 1) * HEAD_DIM]
                wa_scr[kh, r, pl.ds(0, HEAD_DIM), :] = vt_kh
                wb_scr[kh, r, pl.ds(HEAD_DIM, HEAD_DIM), :] = vt_kh
            return carry

        lax.fori_loop(0, nb, per_block, 0)

    col_tile0 = lax.broadcasted_iota(jnp.int32, (1, 2 * LANES), 1) < LANES
    def query_block(sb):
        n = jb * (Q_ROWS // BLOCK) + sb
        kb0 = jnp.clip(n - 1, 0, nb - KEYS // BLOCK)
        ks = pl.multiple_of(kb0 * BLOCK, BLOCK)
        var = jnp.where(n == 0, 0, jnp.where(n == nb - 1, 2, 1))
        qrows = pl.ds(pl.multiple_of(sb * BLOCK, BLOCK), BLOCK)
        for kh in range(N_KV_HEADS):
            kcat = jnp.concatenate([ka_scr[kh, pl.ds(ks, KEYS), :], kb_scr[kh, pl.ds(ks, KEYS), :]], axis=0)
            q2 = jnp.concatenate([q_ref[qrows, pl.ds((2 * kh + jj) * LANES, LANES)] for jj in range(2)], axis=0)
            t = lax.dot_general(kcat, q2, (((1,), (1,)), ((), ())), preferred_element_type=F32)
            t = t + bias_ref[var, kh]
            ps, sink_terms = [], []
            for hh in range(2):
                th = t[hh * KEYS:(hh + 1) * KEYS]
                sink = jnp.where(col_tile0, sink_ref[4 * kh + hh], sink_ref[4 * kh + 2 + hh])
                m = jnp.maximum(jnp.max(th, axis=0, keepdims=True), sink)
                ps.append(jnp.exp2(th - m).astype(BF16))
                sink_terms.append(jnp.exp2(sink - m))
            w = jnp.concatenate([wa_scr[kh, kb0 + i] for i in range(KEYS // BLOCK)]
                                + [wb_scr[kh, kb0 + i] for i in range(KEYS // BLOCK)], axis=1)
            o2 = jnp.dot(w, jnp.concatenate(ps, axis=0), preferred_element_type=F32)
            invs = [1.0 / (o2[LANES + hh * HEAD_DIM:LANES + hh * HEAD_DIM + SUBLANES] + sink_terms[hh])
                    for hh in range(2)]
            inv = jnp.concatenate([invs[hh] for hh in range(2) for _ in range(HEAD_DIM // SUBLANES)], axis=0)
            on = o2[:LANES] * inv
            for jj in range(2):
                o_ref[qrows, pl.ds((2 * kh + jj) * LANES, LANES)] = (
                    on[:, jj * LANES:(jj + 1) * LANES].T.astype(BF16))

    def query_blocks(i, carry):
        for sb in range(BLOCKS_PER_ITER):
            query_block(i * BLOCKS_PER_ITER + sb)
        return carry

    lax.fori_loop(0, Q_ROWS // (BLOCKS_PER_ITER * BLOCK), query_blocks, 0)


def _attn_call(q, k, v, bias, sink, batch, seq_len):
    assert seq_len % Q_ROWS == 0 and seq_len >= KEYS
    nq = seq_len // Q_ROWS
    nb = seq_len // BLOCK
    return pl.pallas_call(
        functools.partial(_attn_kernel, seq_len=seq_len),
        grid_spec=pltpu.PrefetchScalarGridSpec(
            num_scalar_prefetch=1,
            grid=(batch, nq),
            in_specs=[
                pl.BlockSpec((Q_ROWS, D_ATTN), lambda b, j, s: (b * nq + j, 0)),
                pl.BlockSpec((seq_len, KV_W), lambda b, j, s: (b, 0)),
                pl.BlockSpec((seq_len, KV_W), lambda b, j, s: (b, 0)),
                pl.BlockSpec((3, N_KV_HEADS, 2 * KEYS, 2 * LANES), lambda b, j, s: (0, 0, 0, 0)),
            ],
            out_specs=pl.BlockSpec((Q_ROWS, D_ATTN), lambda b, j, s: (b * nq + j, 0)),
            scratch_shapes=[
                pltpu.VMEM((N_KV_HEADS, seq_len, LANES), BF16),
                pltpu.VMEM((N_KV_HEADS, seq_len, LANES), BF16),
                pltpu.VMEM((N_KV_HEADS, nb, 2 * LANES, BLOCK), BF16),
                pltpu.VMEM((N_KV_HEADS, nb, 2 * LANES, BLOCK), BF16),
            ],
        ),
        out_shape=jax.ShapeDtypeStruct((batch * seq_len, D_ATTN), BF16),
        compiler_params=pltpu.CompilerParams(
            dimension_semantics=("arbitrary", "arbitrary"), vmem_limit_bytes=VMEM_LIMIT),
        name="attn",
    )(sink, q, k, v, bias)


def _gelu_tanh(x):
    c = math.sqrt(2.0 / math.pi)
    half = 0.5 * x
    return half + half * jnp.tanh(x * (c + (c * 0.044715) * (x * x)))


def _sigmoid(x):
    return 0.5 * jnp.tanh(0.5 * x) + 0.5


def _silu(x):
    return x * _sigmoid(x)


def _post_kernel(x_ref, yp_ref, zs_ref, ao_ref, za_ref, wglu_ref, bglu_ref, snw_ref, anw_ref,
                 wout_ref, fnw_ref, o_ref, y_scr, b_scr, t_scr):
    s = pl.program_id(2)

    @pl.when(s == 0)
    def _():
        for g in range(N_GROUPS):
            j, g3 = divmod(g, GROUPS_PER_LANE_TILE)
            for th in range(CHUNK_W // LANES):
                tt = yp_ref[g, :, pl.ds(th * LANES, LANES)].T
                for t3 in range(8):
                    t_scr[th * 8 + t3, j, pl.ds(g3 * SSM_GROUP, SSM_GROUP), :] = (
                        tt[t3 * SSM_GROUP:(t3 + 1) * SSM_GROUP, :])
        for t in range(CHUNK):
            for j in range(LANE_TILES):
                b_scr[t, :, pl.ds(j * LANES, LANES)] = t_scr[t, j].T
        perm = _pair_perm(t_major_out=False)
        for cp in range(TILE_CHUNKS // 2):
            a = jnp.concatenate(
                [b_scr[t, pl.ds(cp * 2 * SEQ_TILE, 2 * SEQ_TILE), :] for t in range(CHUNK)], axis=0)
            ys = jnp.dot(perm, a, preferred_element_type=F32)
            for c2 in range(2):
                for sq in range(SEQ_TILE):
                    r0 = (c2 * SEQ_TILE + sq) * CHUNK
                    y_scr[pl.ds(sq * TOK_TILE + (2 * cp + c2) * CHUNK, CHUNK), :] = ys[r0:r0 + CHUNK, :]

    def get(ref):
        return ref[...].reshape(SUB, ref.shape[-1]).astype(F32)

    y = y_scr[pl.ds(pl.multiple_of(s * SUB, SUB), SUB), :]
    g = _gelu_tanh(y)
    gate = _sigmoid(jnp.dot(g.astype(BF16), wglu_ref[...], preferred_element_type=F32) + bglu_ref[...])
    def gate_of(ref):
        return _silu(ref[...].reshape(SUB, ref.shape[-1])).astype(F32)

    n_ssm = _rms(g * gate, snw_ref[...]) * gate_of(zs_ref)
    n_attn = _rms(get(ao_ref), anw_ref[...]) * gate_of(za_ref)
    mixed = jnp.concatenate([n_ssm, n_attn], axis=1).astype(BF16)
    res = get(x_ref) + jnp.dot(mixed, wout_ref[...], preferred_element_type=F32)
    o_ref[...] = _rms(res, fnw_ref[...]).reshape(o_ref.shape)


def _post_call(x, yp, zs, ao, za, w_glu, b_glu, ssm_norm_w, attn_norm_w, w_out, final_norm_w):
    batch, seq_len, _ = x.shape
    nsb, ncb = batch // SEQ_TILE, seq_len // TOK_TILE
    return pl.pallas_call(
        _post_kernel,
        grid=(nsb, ncb, N_SUB),
        in_specs=[
            _tile_specs(D_MODEL), _CHUNK_TILE_SPEC, _tile_specs(D_SSM), _tile_specs(D_ATTN), _tile_specs(D_ATTN),
            _const_spec((D_SSM, D_SSM)), _const_spec((1, D_SSM)), _const_spec((1, D_SSM)),
            _const_spec((1, D_ATTN)), _const_spec((D_MODEL, D_MODEL)), _const_spec((1, D_MODEL)),
        ],
        out_specs=_tile_specs(D_MODEL),
        out_shape=jax.ShapeDtypeStruct((batch, seq_len, D_MODEL), F32),
        scratch_shapes=[
            pltpu.VMEM((TILE_ROWS, D_SSM), F32),
            pltpu.VMEM((CHUNK, LANES, D_SSM), BF16),
            pltpu.VMEM((CHUNK, LANE_TILES, LANES, LANES), BF16),
        ],
        compiler_params=pltpu.CompilerParams(
            dimension_semantics=("arbitrary", "arbitrary", "arbitrary"), vmem_limit_bytes=VMEM_LIMIT),
        name="post",
    )(x, yp, zs, ao, za, w_glu, b_glu, ssm_norm_w, attn_norm_w, w_out, final_norm_w)


def _cmul(ar, ai, br, bi):
    return ar * br - ai * bi, ar * bi + ai * br


def _ops_kernel(row_ref, col_ref, bt_ref, ct_ref, dsk_ref, toep_ref, bs_ref, cs_ref, a_ref):
    def discretise(lr, li, ls):
        dt = jnp.exp(ls)
        mag = jnp.exp(lr * dt)
        return mag * jnp.cos(li * dt), mag * jnp.sin(li * dt)

    lr, li = row_ref[0], row_ref[1]
    lb_re, lb_im = discretise(lr, li, row_ref[2])
    den = lr * lr + li * li
    nr = lb_re - 1.0
    coef_re = (nr * lr + lb_im * li) / den
    coef_im = (lb_im * lr - nr * li) / den
    bb_re, bb_im = _cmul(coef_re, coef_im, bt_ref[0], bt_ref[1])
    bb_re = jnp.concatenate([bb_re] * CHUNK, axis=0)
    bb_im = jnp.concatenate([bb_im] * CHUNK, axis=0)

    row_t = lax.broadcasted_iota(jnp.int32, (CHUNK_W, LANES), 0) // SSM_GROUP
    fwd_lane = lax.broadcasted_iota(jnp.int32, (CHUNK_W, LANES), 1) < SSM_STATE
    col_t = lax.broadcasted_iota(jnp.int32, (LANES, CHUNK_W), 1) // SSM_GROUP
    fwd_row = lax.broadcasted_iota(jnp.int32, (LANES, CHUNK_W), 0) < SSM_STATE
    ct_re, ct_im = ct_ref[0], ct_ref[1]
    w_re = jnp.zeros((CHUNK_W, LANES), F32)
    w_im = jnp.zeros((CHUNK_W, LANES), F32)
    toep = jnp.zeros((CHUNK_W, CHUNK_W), F32)
    for i in range(CHUNK):
        t_now = jnp.where(fwd_lane, i, CHUNK - 1 - i)
        w_re, w_im = _cmul(w_re, w_im, lb_re, lb_im)
        w_re = w_re + jnp.where(row_t == t_now, bb_re, 0.0)
        w_im = w_im + jnp.where(row_t == t_now, bb_im, 0.0)
        out_now = col_t == jnp.where(fwd_row, i, CHUNK - 1 - i)
        rhs = jnp.concatenate([jnp.where(out_now, ct_re, 0.0), jnp.where(out_now, -ct_im, 0.0)], axis=0)
        lhs = jnp.concatenate([w_re, w_im], axis=1)
        toep = toep + jnp.dot(lhs.astype(BF16), rhs.astype(BF16), preferred_element_type=F32)
    r = lax.broadcasted_iota(jnp.int32, (CHUNK_W, CHUNK_W), 0)
    c = lax.broadcasted_iota(jnp.int32, (CHUNK_W, CHUNK_W), 1)
    toep_ref[...] = (toep + jnp.where(r == c, dsk_ref[...], 0.0)).astype(BF16)
    bs_ref[...] = jnp.concatenate([w_re, w_im], axis=1).astype(BF16)

    pr, pi = discretise(col_ref[0], col_ref[1], col_ref[2])
    k = jnp.where(fwd_row, col_t + 1, CHUNK - col_t)
    acc_re = jnp.ones((LANES, CHUNK_W), F32)
    acc_im = jnp.zeros((LANES, CHUNK_W), F32)
    for bit in range(CHUNK.bit_length()):
        nre, nim = _cmul(acc_re, acc_im, pr, pi)
        take = (k & (1 << bit)) != 0
        acc_re = jnp.where(take, nre, acc_re)
        acc_im = jnp.where(take, nim, acc_im)
        pr, pi = _cmul(pr, pi, pr, pi)
    cp_re, cp_im = _cmul(ct_re, ct_im, acc_re, acc_im)
    cs_ref[...] = jnp.concatenate([cp_re, -cp_im], axis=0).astype(BF16)

    a_re, a_im = lb_re, lb_im
    for _ in range(CHUNK.bit_length() - 1):
        a_re, a_im = _cmul(a_re, a_im, a_re, a_im)
    a_ref[0] = a_re
    a_ref[1] = a_im


def _ssm_operators(lam_re, lam_im, log_step, b_re, b_im, c_re, c_im, d_skip):
    g_first = lambda a: jnp.transpose(a.astype(F32), (1, 0, 2)).reshape(N_GROUPS, 2 * SSM_STATE)
    params = jnp.stack([g_first(lam_re), g_first(lam_im),
                        jnp.repeat(log_step.astype(F32).T, SSM_STATE, axis=1)], axis=1)
    b_t = lambda b: jnp.transpose(b.astype(F32), (1, 3, 0, 2)).reshape(N_GROUPS, SSM_GROUP, 2 * SSM_STATE)
    c_t = lambda c: jnp.tile(
        jnp.transpose(c.astype(F32), (1, 0, 3, 2)).reshape(N_GROUPS, 2 * SSM_STATE, SSM_GROUP), (1, 1, CHUNK))
    bt = jnp.stack([b_t(b_re), b_t(b_im)], axis=1)
    ct = jnp.stack([c_t(c_re), c_t(c_im)], axis=1)
    dsk = jnp.tile(d_skip.astype(F32), (1, CHUNK))[:, None, :]
    grp = lambda *shape: pl.BlockSpec((None,) + shape, lambda g: (g,) + (0,) * len(shape))
    mat = jax.ShapeDtypeStruct((N_GROUPS, CHUNK_W, CHUNK_W), BF16)
    toep, bs, cs, a = pl.pallas_call(
        _ops_kernel,
        grid=(N_GROUPS,),
        in_specs=[grp(3, 1, LANES), grp(3, LANES, 1), grp(2, SSM_GROUP, LANES), grp(2, LANES, CHUNK_W),
                  grp(1, CHUNK_W)],
        out_specs=(grp(CHUNK_W, CHUNK_W), grp(CHUNK_W, STATE_W), grp(STATE_W, CHUNK_W), grp(2, 1, LANES)),
        out_shape=(mat, mat, mat, jax.ShapeDtypeStruct((N_GROUPS, 2, 1, LANES), F32)),
        compiler_params=pltpu.CompilerParams(dimension_semantics=("arbitrary",)),
        name="s5_ops",
    )(params[:, :, None, :], params[:, :, :, None], bt, ct, dsk)
    return toep, bs, cs, a[:, 0], a[:, 1]


def _t5_buckets_np(rel):
    half = N_BUCKETS // 2
    max_exact = half // 2
    ret = np.where(rel > 0, half, 0)
    n = np.abs(rel)
    nf = np.maximum(n, 1).astype(np.float64)
    large = max_exact + (np.log(nf / max_exact) / math.log(MAX_DISTANCE / max_exact)
                         * (half - max_exact)).astype(np.int32)
    large = np.minimum(large, half - 1)
    return ret + np.where(n < max_exact, n, large)


def _attn_tables(rel_bias):
    offsets = (0, -BLOCK, -2 * BLOCK)
    dist = np.arange(BIAS_PERIOD)
    dist = np.where(dist < KEYS, dist, dist - BIAS_PERIOD)
    rel = np.stack([dist + off for off in offsets])
    per_dist = rel_bias.astype(F32)[_t5_buckets_np(rel)] * LOG2_E
    per_dist = jnp.where((np.abs(rel) <= WINDOW)[:, :, None], per_dist, NEG_INF)
    per_dist = jnp.transpose(per_dist, (0, 2, 1))
    neg = jnp.concatenate([per_dist[:, :, :1], per_dist[:, :, :0:-1]], axis=-1)
    return pl.pallas_call(
        _bias_kernel,
        grid=(len(offsets),),
        in_specs=[pl.BlockSpec((None, N_HEADS, 1, BIAS_PERIOD), lambda v: (v, 0, 0, 0))],
        out_specs=pl.BlockSpec((None, N_KV_HEADS, 2 * KEYS, 2 * LANES), lambda v: (v, 0, 0, 0)),
        out_shape=jax.ShapeDtypeStruct((len(offsets), N_KV_HEADS, 2 * KEYS, 2 * LANES), F32),
        compiler_params=pltpu.CompilerParams(dimension_semantics=("arbitrary",)),
        name="bias",
    )(neg[:, :, None, :])


BIAS_PERIOD = 4 * BLOCK


def _bias_kernel(neg_ref, o_ref):
    for kh in range(N_KV_HEADS):
        for tile in range(2):
            for slot in range(2):
                rows = jnp.broadcast_to(neg_ref[4 * kh + 2 * tile + slot], (KEYS, BIAS_PERIOD))
                skewed = pltpu.roll(rows, 0, 1, stride=1, stride_axis=0)
                o_ref[kh, pl.ds(slot * KEYS, KEYS), pl.ds(tile * LANES, LANES)] = skewed[:, :BLOCK]


def kernel(x_prompt, x_sample, norm_w, w_in, lam_re, lam_im, log_step, b_re, b_im, c_re, c_im, d_skip,
           w_glu, b_glu, ssm_norm_w, sink, attn_norm_w, w_out, rel_bias, final_norm_w):
    assert norm_w.shape[0] == 1, "single-layer encoder"
    ops = _ssm_operators(lam_re[0], lam_im[0], log_step[0], b_re[0], b_im[0], c_re[0], c_im[0], d_skip[0])
    bias = _attn_tables(rel_bias)
    w_in_b = w_in[0].astype(BF16)
    w_glu_b = w_glu[0].astype(BF16)
    w_out_b = w_out[0].astype(BF16)
    sink_f = sink[0].astype(F32) * LOG2_E

    outs = []
    for x in (x_prompt, x_sample):
        batch, seq_len, _ = x.shape
        assert batch % SEQ_TILE == 0 and seq_len % TOK_TILE == 0
        n_tok = batch * seq_len
        up, zs, q, k, v, za = _proj_call(x, norm_w, w_in_b)
        yp = _ssm_call(up, ops)
        ao = _attn_call(q.reshape(n_tok, D_ATTN), k.reshape(n_tok, KV_W), v.reshape(n_tok, KV_W),
                        bias, sink_f, batch, seq_len)
        outs.append(_post_call(x, yp, zs, ao.reshape(batch, seq_len, D_ATTN), za, w_glu_b, b_glu, ssm_norm_w,
                               attn_norm_w, w_out_b, final_norm_w.reshape(1, D_MODEL)))
    return tuple(outs)
```
